```python
import jax, jax.numpy as jnp
from jax import lax
import numpy as np

D_MODEL = 1024
BATCH = 4
SEQ = 8192
DEPTH = 1

N_META = 16
CHUNK = 128
HGRN_HEADS = 4
HGRN_HEAD_DIM = 128
HGRN_WIDTH = HGRN_HEADS * HGRN_HEAD_DIM
CONV_GROUPS = 4
CONV_WIDTH_CH = 512
CONV_K = 3
MIX_WIDTH = HGRN_WIDTH + CONV_WIDTH_CH
IN_PROJ_WIDTH = 4 * HGRN_WIDTH + 3 * CONV_WIDTH_CH
N_EXPERTS = 64
TOP_K = 8
EXPERT_FF = 256
SHARED_FF = 256
ROUTED_SCALE = 2.5
EXPERT_BLOCK = 128
EPS = 1e-6

kernel_name = 'hymba_hgrn2_shortconv_moe'


def rmsnorm(x, g):
    xf = x.astype(jnp.float32)
    y = xf * lax.rsqrt(jnp.mean(xf * xf, axis=-1, keepdims=True) + EPS)
    return (y * g.astype(jnp.float32)).astype(x.dtype)


def group_rmsnorm(x, g, n_groups):
    shp = x.shape
    xf = x.astype(jnp.float32).reshape(shp[:-1] + (n_groups, shp[-1] // n_groups))
    y = xf * lax.rsqrt(jnp.mean(xf * xf, axis=-1, keepdims=True) + EPS)
    return (y.reshape(shp) * g.astype(jnp.float32)).astype(x.dtype)


def hgrn2_mixer(q, f_pre, i_val, g_out, lb, norm_g):
    bsz, t_len, _ = q.shape
    lb = lb.astype(jnp.float32)
    z = f_pre.astype(jnp.float32)
    log_f = jnp.log(lb + (1.0 - lb) * jax.nn.sigmoid(z))
    k = (1.0 - lb) * jax.nn.sigmoid(-z)
    pad = CHUNK - N_META

    def to_chunks(a):
        a = jnp.pad(a.astype(jnp.float32), ((0, 0), (pad, 0), (0, 0)))
        n = a.shape[1] // CHUNK
        return a.reshape(bsz, n, CHUNK, HGRN_HEADS, HGRN_HEAD_DIM).transpose(1, 0, 3, 2, 4)

    qc, kc, vc, lfc = to_chunks(q), to_chunks(k), to_chunks(i_val), to_chunks(log_f)
    causal = jnp.tril(jnp.ones((CHUNK, CHUNK), dtype=bool))

    def step(state, inp):
        qb, kb, vb, lfb = inp
        b = jnp.cumsum(lfb, axis=2)
        diff = b[:, :, :, None, :] - b[:, :, None, :, :]
        decay = jnp.exp(jnp.where(causal[:, :, None], diff, -jnp.inf))
        scores = jnp.einsum('bhtd,bhsd,bhtsd->bhts', qb, kb, decay)
        o = (jnp.einsum('bhts,bhsv->bhtv', scores, vb)
             + jnp.einsum('bhtd,bhdv->bhtv', qb * jnp.exp(b), state))
        b_last = b[:, :, -1:, :]
        state = (jnp.exp(b_last[:, :, 0, :])[..., None] * state
                 + jnp.einsum('bhsd,bhsv->bhdv', kb * jnp.exp(b_last - b), vb))
        return state, o

    s0 = jnp.zeros((bsz, HGRN_HEADS, HGRN_HEAD_DIM, HGRN_HEAD_DIM), jnp.float32)
    _, o = lax.scan(step, s0, (qc, kc, vc, lfc))
    n = o.shape[0]
    o = o.transpose(1, 0, 3, 2, 4).reshape(bsz, n * CHUNK, HGRN_WIDTH)[:, pad:]
    o = group_rmsnorm(o, norm_g, HGRN_HEADS) * jax.nn.silu(g_out.astype(jnp.float32))
    return o.astype(q.dtype)


def short_conv_mixer(b_gate, c_gate, v, conv_w, norm_g):
    u = c_gate * v
    y = lax.conv_general_dilated(
        u, conv_w[:, None, :].astype(u.dtype), window_strides=(1,),
        padding=[(CONV_K - 1, 0)], dimension_numbers=('NWC', 'WIO', 'NWC'),
        feature_group_count=CONV_WIDTH_CH)
    return group_rmsnorm(b_gate * y, norm_g, CONV_GROUPS)


def swiglu(x, w_gate, w_up, w_down):
    return (jax.nn.silu(x @ w_gate) * (x @ w_up)) @ w_down


def moe_ffn(xn, w_router, router_bias, w_gate_e, w_up_e, w_down_e, w_gate_s, w_up_s, w_down_s):
    bsz, t_len, d = xn.shape
    xt = xn.reshape(-1, d)
    n_tok = xt.shape[0]
    scores = jax.nn.sigmoid((xt @ w_router).astype(jnp.float32))
    _, top_idx = lax.top_k(scores + router_bias.astype(jnp.float32), TOP_K)
    top_s = jnp.take_along_axis(scores, top_idx, axis=-1)
    gates = top_s / jnp.sum(top_s, axis=-1, keepdims=True) * ROUTED_SCALE

    m = n_tok * TOP_K
    flat_e = top_idx.reshape(-1)
    flat_tok = jnp.repeat(jnp.arange(n_tok, dtype=jnp.int32), TOP_K)
    flat_g = gates.reshape(-1)
    order = jnp.argsort(flat_e, stable=True)
    e_sorted, tok_sorted, g_sorted = flat_e[order], flat_tok[order], flat_g[order]
    counts = jnp.bincount(flat_e, length=N_EXPERTS)
    starts = jnp.cumsum(counts) - counts
    padded = (counts + EXPERT_BLOCK - 1) // EXPERT_BLOCK * EXPERT_BLOCK
    pad_ends = jnp.cumsum(padded)
    pad_starts = pad_ends - padded
    dest = pad_starts[e_sorted] + jnp.arange(m, dtype=jnp.int32) - starts[e_sorted]
    n_blocks = -(-m // EXPERT_BLOCK) + N_EXPERTS
    rows = n_blocks * EXPERT_BLOCK
    src_tok = jnp.zeros((rows,), jnp.int32).at[dest].set(tok_sorted)
    row_gate = jnp.zeros((rows,), jnp.float32).at[dest].set(g_sorted)
    block_expert = jnp.minimum(
        jnp.searchsorted(pad_ends, jnp.arange(n_blocks) * EXPERT_BLOCK, side='right'),
        N_EXPERTS - 1)

    def expert_block(args):
        tok_idx, g, e = args
        xb = xt[tok_idx]
        yb = swiglu(xb, w_gate_e[e], w_up_e[e], w_down_e[e])
        return yb * g[:, None].astype(yb.dtype)

    y_rows = lax.map(expert_block, (src_tok.reshape(n_blocks, EXPERT_BLOCK),
                                    row_gate.reshape(n_blocks, EXPERT_BLOCK),
                                    block_expert))
    routed = jnp.zeros_like(xt).at[src_tok].add(y_rows.reshape(rows, d))
    shared = swiglu(xt, w_gate_s, w_up_s, w_down_s)
    return (routed + shared).reshape(bsz, t_len, d)


def setup_inputs(seed: int = 0) -> dict:
    key = jax.random.key(seed)
    ks = jax.random.split(key, 20)
    nrm = jax.random.normal
    return {
        'x': nrm(ks[0], (BATCH, SEQ, D_MODEL), jnp.float32),
        'meta_tokens': nrm(ks[1], (N_META, D_MODEL), jnp.float32),
        'norm_mix_g': 1.0 + 0.02 * nrm(ks[2], (DEPTH, D_MODEL), jnp.float32),
        'w_in': nrm(ks[3], (DEPTH, D_MODEL, IN_PROJ_WIDTH), jnp.float32) * D_MODEL ** -0.5,
        'lb_table': 0.5 * nrm(ks[4], (DEPTH + 1, HGRN_WIDTH), jnp.float32),
        'hgrn_norm_g': 1.0 + 0.02 * nrm(ks[5], (DEPTH, HGRN_WIDTH), jnp.float32),
        'conv_w': nrm(ks[6], (DEPTH, CONV_K, CONV_WIDTH_CH), jnp.float32) * CONV_K ** -0.5,
        'conv_norm_g': 1.0 + 0.02 * nrm(ks[7], (DEPTH, CONV_WIDTH_CH), jnp.float32),
        'w_out': nrm(ks[8], (DEPTH, MIX_WIDTH, D_MODEL), jnp.float32) * MIX_WIDTH ** -0.5,
        'norm_ffn_g': 1.0 + 0.02 * nrm(ks[9], (DEPTH, D_MODEL), jnp.float32),
        'w_router': nrm(ks[10], (DEPTH, D_MODEL, N_EXPERTS), jnp.float32) * D_MODEL ** -0.5,
        'router_bias': 0.01 * nrm(ks[11], (DEPTH, N_EXPERTS), jnp.float32),
        'w_gate_e': nrm(ks[12], (DEPTH, N_EXPERTS, D_MODEL, EXPERT_FF), jnp.float32) * D_MODEL ** -0.5,
        'w_up_e': nrm(ks[13], (DEPTH, N_EXPERTS, D_MODEL, EXPERT_FF), jnp.float32) * D_MODEL ** -0.5,
        'w_down_e': nrm(ks[14], (DEPTH, N_EXPERTS, EXPERT_FF, D_MODEL), jnp.float32) * EXPERT_FF ** -0.5,
        'w_gate_s': nrm(ks[15], (DEPTH, D_MODEL, SHARED_FF), jnp.float32) * D_MODEL ** -0.5,
        'w_up_s': nrm(ks[16], (DEPTH, D_MODEL, SHARED_FF), jnp.float32) * D_MODEL ** -0.5,
        'w_down_s': nrm(ks[17], (DEPTH, SHARED_FF, D_MODEL), jnp.float32) * SHARED_FF ** -0.5,
        'norm_final_g': 1.0 + 0.02 * nrm(ks[18], (D_MODEL,), jnp.float32),
    }


def reference(x, meta_tokens, norm_mix_g, w_in, lb_table, hgrn_norm_g, conv_w, conv_norm_g,
              w_out, norm_ffn_g, w_router, router_bias, w_gate_e, w_up_e, w_down_e,
              w_gate_s, w_up_s, w_down_s, norm_final_g):
    bsz = x.shape[0]
    meta = jnp.broadcast_to(meta_tokens.astype(x.dtype)[None], (bsz, N_META, D_MODEL))
    h = jnp.concatenate([meta, x], axis=1)
    lower_bounds = jnp.cumsum(jax.nn.softmax(lb_table.astype(jnp.float32), axis=0), axis=0)
    split_at = np.cumsum([HGRN_WIDTH] * 4 + [CONV_WIDTH_CH] * 3)[:-1].tolist()
    for l in range(DEPTH):
        xn = rmsnorm(h, norm_mix_g[l])
        proj = xn @ w_in[l]
        q, f_pre, i_val, g_out, b_gate, c_gate, v = jnp.split(proj, split_at, axis=-1)
        y_hgrn = hgrn2_mixer(q, f_pre, i_val, g_out, lower_bounds[l], hgrn_norm_g[l])
        y_conv = short_conv_mixer(b_gate, c_gate, v, conv_w[l], conv_norm_g[l])
        h = h + jnp.concatenate([y_hgrn, y_conv], axis=-1) @ w_out[l]
        h = h + moe_ffn(rmsnorm(h, norm_ffn_g[l]), w_router[l], router_bias[l],
                        w_gate_e[l], w_up_e[l], w_down_e[l],
                        w_gate_s[l], w_up_s[l], w_down_s[l])
    return rmsnorm(h, norm_final_g)[:, N_META:]
```

```python
import functools

import numpy as np
import jax
import jax.numpy as jnp
from jax import lax
from jax.experimental import pallas as pl
from jax.experimental.pallas import tpu as pltpu

N_META = 16
CHUNK = 128
HEADS = 4
HEAD_DIM = 128
HGRN_W = HEADS * HEAD_DIM
CONV_W = 512
CONV_GROUPS = 4
CONV_K = 3
N_EXPERTS = 64
TOP_K = 8
ROUTED_SCALE = 2.5
EPS = 1e-6

V7X_VMEM_BYTES = 64 * 1024 * 1024
VMEM_LIMIT = V7X_VMEM_BYTES - 8 * 1024 * 1024

MIX_ROWS = 512
ROUTE_ROWS = 512
DISPATCH_ROWS = 512
EXPERT_ROWS = 512
COMBINE_ROWS = 256

HALF_SPANS = (64, 32, 16, 8, 4, 2, 1)
N_LEVELS = len(HALF_SPANS) + 1


def _decay_sum_matrix():
    a = np.zeros((N_LEVELS, CHUNK, CHUNK), np.float32)
    a[0] = np.tril(np.ones((CHUNK, CHUNK), np.float32))
    for i, m in enumerate(HALF_SPANS):
        for t in range(CHUNK):
            mid = (t // (2 * m)) * 2 * m + m
            if t >= mid:
                a[1 + i, t, mid:t + 1] = 1.0
            else:
                a[1 + i, t, t + 1:mid] = 1.0
    return a.reshape(N_LEVELS * CHUNK, CHUNK)


def _level_matrix():
    lv = np.full((CHUNK, CHUNK), -1, np.int32)
    for t in range(CHUNK):
        lv[t, t] = len(HALF_SPANS)
        for s in range(t):
            top = (t ^ s).bit_length() - 1
            lv[t, s] = HALF_SPANS.index(1 << top)
    return lv


def _rms(x, g):
    return x * lax.rsqrt(jnp.mean(x * x, axis=-1, keepdims=True) + EPS) * g


def _group_rms(x, g, width):
    outs = []
    for j in range(x.shape[-1] // width):
        xs = x[:, j * width:(j + 1) * width]
        outs.append(xs * lax.rsqrt(jnp.mean(xs * xs, axis=-1, keepdims=True) + EPS))
    return jnp.concatenate(outs, axis=-1) * g


def _sigmoid_pair(z):
    t = jnp.exp(-jnp.abs(z))
    inv = 1.0 / (1.0 + t)
    big, small = inv, t * inv
    pos = z >= 0
    return jnp.where(pos, big, small), jnp.where(pos, small, big)


def _dot(a, b):
    return jnp.dot(a, b, preferred_element_type=jnp.float32)


def _dot_nt(a, b):
    return lax.dot_general(a, b, (((1,), (1,)), ((), ())), preferred_element_type=jnp.float32)


def _dot_tn(a, b):
    return lax.dot_general(a, b, (((0,), (0,)), ((), ())), preferred_element_type=jnp.float32)


def _hgrn_chunk(q, z, iv, lb, amat, level, st_ref, first_valid_row):
    sig, sig_neg = _sigmoid_pair(z)
    lf = jnp.log(lb + (1.0 - lb) * sig)
    k = (1.0 - lb) * sig_neg
    row = lax.broadcasted_iota(jnp.int32, (CHUNK, HGRN_W), 0)
    if first_valid_row:
        valid = row >= first_valid_row
        lf = jnp.where(valid, lf, 0.0)
        k = jnp.where(valid, k, 0.0)

    h1 = lf.astype(jnp.bfloat16)
    r1 = lf - h1.astype(jnp.float32)
    h2 = r1.astype(jnp.bfloat16)
    h3 = (r1 - h2.astype(jnp.float32)).astype(jnp.bfloat16)
    e_all = _dot(amat, h1) + _dot(amat, h2) + _dot(amat, h3)

    b = e_all[0:CHUNK]
    b_last = b[CHUNK - 1:CHUNK]
    q_in = (q * jnp.exp(b)).astype(jnp.bfloat16)
    k_out = (k * jnp.exp(b_last - b)).astype(jnp.bfloat16)
    st_decay = jnp.exp(b_last)
    v_bf = iv.astype(jnp.bfloat16)

    q_lv = [q.astype(jnp.bfloat16)]
    k_lv = [k.astype(jnp.bfloat16)]
    for i, m in enumerate(HALF_SPANS):
        ex = jnp.exp(e_all[(1 + i) * CHUNK:(2 + i) * CHUNK])
        right = (row & m) != 0
        q_lv.append(jnp.where(right, q * ex, 0.0).astype(jnp.bfloat16))
        k_lv.append(jnp.where(right, 0.0, k * ex).astype(jnp.bfloat16))
    lv_of = [len(HALF_SPANS)] + list(range(len(HALF_SPANS)))

    outs = []
    for h in range(HEADS):
        cols = slice(h * HEAD_DIM, (h + 1) * HEAD_DIM)
        scores = jnp.zeros((CHUNK, CHUNK), jnp.float32)
        for ql, kl, lv in zip(q_lv, k_lv, lv_of):
            scores = jnp.where(level == lv, _dot_nt(ql[:, cols], kl[:, cols]), scores)
        st = st_ref[h]
        o = _dot(scores.astype(jnp.bfloat16), v_bf[:, cols]) + _dot_nt(q_in[:, cols], st.astype(jnp.bfloat16))
        st_ref[h] = st * st_decay[:, cols] + _dot_tn(v_bf[:, cols], k_out[:, cols])
        outs.append(o)
    return jnp.concatenate(outs, axis=-1)


def _mixer_kernel(x_ref, meta_ref, gmix_ref, win_ref, lbt_ref, ghg_ref, cw_ref, gcv_ref, wout_ref,
                  amat_ref, level_ref, h1_ref, proj_ref, o_ref, u_ref, st_ref):
    j = pl.program_id(1)
    rows = x_ref.shape[0]
    n_in = win_ref.shape[1]

    lbt = lbt_ref[...]
    lbe = jnp.exp(lbt - jnp.max(lbt, axis=0, keepdims=True))
    lb = lbe[0:1] / jnp.sum(lbe, axis=0, keepdims=True)

    amat = amat_ref[...]
    level = level_ref[...]
    gmix = gmix_ref[...]

    def project(xv, dst_rows):
        xn = _rms(xv, gmix).astype(jnp.bfloat16)
        for c0 in range(0, n_in, 512):
            proj_ref[dst_rows, c0:c0 + 512] = _dot(xn, win_ref[:, c0:c0 + 512])

    @pl.when(j == 0)
    def _():
        st_ref[...] = jnp.zeros_like(st_ref)
        project(meta_ref[...], pl.ds(0, CHUNK))
        pm = proj_ref[0:CHUNK, :]
        _hgrn_chunk(pm[:, 0:512], pm[:, 512:1024], pm[:, 1024:1536], lb, amat, level, st_ref,
                    CHUNK - N_META)
        u_ref[0:8, :] = (pm[:, 2560:3072] * pm[:, 3072:3584])[CHUNK - 8:CHUNK]

    project(x_ref[...], pl.ds(0, rows))

    def chunk_body(c, carry):
        r0 = pl.multiple_of(c * CHUNK, CHUNK)
        q = proj_ref[pl.ds(r0, CHUNK), 0:512]
        z = proj_ref[pl.ds(r0, CHUNK), 512:1024]
        iv = proj_ref[pl.ds(r0, CHUNK), 1024:1536]
        o_ref[pl.ds(r0, CHUNK), :] = _hgrn_chunk(q, z, iv, lb, amat, level, st_ref, 0)
        return carry

    lax.fori_loop(0, rows // CHUNK, chunk_body, 0)

    g_out = proj_ref[:, 1536:2048]
    g_sig, _ = _sigmoid_pair(g_out)
    y_hgrn = _group_rms(o_ref[...], ghg_ref[...], HEAD_DIM) * (g_out * g_sig)

    u = proj_ref[:, 2560:3072] * proj_ref[:, 3072:3584]
    u_ref[8:8 + rows, :] = u
    cw = cw_ref[...]
    y = cw[2:3] * u + cw[1:2] * u_ref[7:7 + rows, :] + cw[0:1] * u_ref[6:6 + rows, :]
    u_ref[0:8, :] = u[rows - 8:rows]
    y_conv = _group_rms(proj_ref[:, 2048:2560] * y, gcv_ref[...], CONV_W // CONV_GROUPS)

    mixed = jnp.concatenate([y_hgrn, y_conv], axis=-1).astype(jnp.bfloat16)
    h1_ref[...] = x_ref[...] + _dot(mixed, wout_ref[...])


def _mixer(x, meta_pad, gmix, w_in, lb_table, ghg, conv_w, gcv, w_out):
    bsz, seq, d = x.shape
    n_in = w_in.shape[1]
    rows = MIX_ROWS
    const = lambda *shape: pl.BlockSpec(shape, lambda b, j: (0,) * len(shape))
    return pl.pallas_call(
        _mixer_kernel,
        out_shape=jax.ShapeDtypeStruct((bsz, seq, d), jnp.float32),
        grid=(bsz, seq // rows),
        in_specs=[
            pl.BlockSpec((None, rows, d), lambda b, j: (b, j, 0)),
            const(CHUNK, d), const(1, d), const(d, n_in), const(*lb_table.shape), const(1, HGRN_W),
            const(CONV_K, CONV_W), const(1, CONV_W), const(d, d),
            const(N_LEVELS * CHUNK, CHUNK), const(CHUNK, CHUNK),
        ],
        out_specs=pl.BlockSpec((None, rows, d), lambda b, j: (b, j, 0)),
        scratch_shapes=[
            pltpu.VMEM((rows, n_in), jnp.float32),
            pltpu.VMEM((rows, HGRN_W), jnp.float32),
            pltpu.VMEM((rows + 8, CONV_W), jnp.float32),
            pltpu.VMEM((HEADS, HEAD_DIM, HEAD_DIM), jnp.float32),
        ],
        compiler_params=pltpu.CompilerParams(
            dimension_semantics=("arbitrary", "arbitrary"), vmem_limit_bytes=VMEM_LIMIT),
        name="mixer",
    )(x, meta_pad, gmix, w_in, lb_table, ghg, conv_w, gcv, w_out,
      jnp.asarray(_decay_sum_matrix(), jnp.bfloat16), jnp.asarray(_level_matrix()))


def _router_kernel(h1_ref, gffn_ref, wrt_ref, bias_ref, wgs_ref, wus_ref, wds_ref, tri_ref,
                   base_ref, xn_ref, idx_ref, gate_ref, rank_ref, counts_ref, carry_ref):
    i = pl.program_id(0)
    n_tok = h1_ref.shape[0]

    @pl.when(i == 0)
    def _():
        carry_ref[...] = jnp.zeros_like(carry_ref)

    h1 = h1_ref[...]
    xn = _rms(h1, gffn_ref[...])
    xn_ref[...] = xn
    xb = xn.astype(jnp.bfloat16)

    g_pre = _dot(xb, wgs_ref[...])
    gate_s, _ = _sigmoid_pair(g_pre)
    hid = (g_pre * gate_s) * _dot(xb, wus_ref[...])
    base_ref[...] = h1 + _dot(hid.astype(jnp.bfloat16), wds_ref[...])

    logits = lax.dot_general(wrt_ref[...], xn, (((1,), (1,)), ((), ())),
                             precision=lax.Precision.HIGHEST, preferred_element_type=jnp.float32)
    scores, _ = _sigmoid_pair(logits)
    sel = scores + bias_ref[...]
    eid = lax.broadcasted_iota(jnp.int32, (N_EXPERTS, n_tok), 0).astype(jnp.float32)
    picks, top_s = [], []
    for _ in range(TOP_K):
        best = jnp.max(sel, axis=0, keepdims=True)
        pick = jnp.min(jnp.where(sel == best, eid, float(N_EXPERTS)), axis=0, keepdims=True)
        hit = eid == pick
        top_s.append(jnp.sum(jnp.where(hit, scores, 0.0), axis=0, keepdims=True))
        sel = jnp.where(hit, -jnp.inf, sel)
        picks.append(pick)
    top_s = jnp.concatenate(top_s, axis=0)
    gate_ref[...] = top_s / jnp.sum(top_s, axis=0, keepdims=True) * ROUTED_SCALE
    idx_ref[...] = jnp.concatenate(picks, axis=0).astype(jnp.int32)

    chosen = jnp.zeros((N_EXPERTS, n_tok), jnp.float32)
    for pick in picks:
        chosen = chosen + jnp.where(eid == pick, 1.0, 0.0)
    incl = _dot(chosen.astype(jnp.bfloat16), tri_ref[...])
    before = carry_ref[...] + incl - 1.0
    ranks = [jnp.sum(jnp.where(eid == pick, before, 0.0), axis=0, keepdims=True) for pick in picks]
    rank_ref[...] = jnp.concatenate(ranks, axis=0).astype(jnp.int32)
    carry_ref[...] = carry_ref[...] + incl[:, n_tok - 1:n_tok]
    counts_ref[...] = jnp.broadcast_to(carry_ref[...], counts_ref.shape).astype(jnp.int32)


def _router(h1, gffn, w_router_t, bias, wgs, wus, wds):
    n, d = h1.shape
    t = ROUTE_ROWS
    ff = wgs.shape[1]
    tri = jnp.asarray(np.triu(np.ones((t, t), np.float32)), jnp.bfloat16)
    const = lambda *shape: pl.BlockSpec(shape, lambda i: (0,) * len(shape))
    tok = lambda width: pl.BlockSpec((t, width), lambda i: (i, 0))
    slot = pl.BlockSpec((TOP_K, t), lambda i: (0, i))
    return pl.pallas_call(
        _router_kernel,
        out_shape=(
            jax.ShapeDtypeStruct((n, d), jnp.float32),
            jax.ShapeDtypeStruct((n, d), jnp.float32),
            jax.ShapeDtypeStruct((TOP_K, n), jnp.int32),
            jax.ShapeDtypeStruct((TOP_K, n), jnp.float32),
            jax.ShapeDtypeStruct((TOP_K, n), jnp.int32),
            jax.ShapeDtypeStruct((N_EXPERTS, 128), jnp.int32),
        ),
        grid=(n // t,),
        in_specs=[tok(d), const(1, d), const(N_EXPERTS, d), const(N_EXPERTS, 1),
                  const(d, ff), const(d, ff), const(ff, d), const(t, t)],
        out_specs=(tok(d), tok(d), slot, slot, slot, const(N_EXPERTS, 128)),
        scratch_shapes=[pltpu.VMEM((N_EXPERTS, 1), jnp.float32)],
        compiler_params=pltpu.CompilerParams(
            dimension_semantics=("arbitrary",), vmem_limit_bytes=VMEM_LIMIT),
        name="router",
    )(h1, gffn, w_router_t, bias, wgs, wus, wds, tri)


def _row_copy(src_hbm, src_row, dst_hbm, dst_row, sem):
    return pltpu.make_async_copy(src_hbm.at[pl.ds(src_row, 1)], dst_hbm.at[pl.ds(dst_row, 1)], sem)


def _dispatch_kernel(starts_ref, idx_ref, rank_ref, xn_hbm, xs_in_hbm, xs_hbm, sem):
    del xs_in_hbm
    n_tok = idx_ref.shape[1]
    base = pl.program_id(0) * n_tok

    def issue(t, carry):
        for s in range(TOP_K):
            dst = starts_ref[idx_ref[s, t]] + rank_ref[s, t]
            _row_copy(xn_hbm, base + t, xs_hbm, dst, sem).start()
        return carry

    lax.fori_loop(0, n_tok, issue, 0)

    def drain(t, carry):
        for s in range(TOP_K):
            _row_copy(xn_hbm, 0, xs_hbm, 0, sem).wait()
        return carry

    lax.fori_loop(0, n_tok, drain, 0)


def _dispatch(starts, idx, rank, xn, xs_zero):
    n, d = xn.shape
    t = DISPATCH_ROWS
    slot = pl.BlockSpec((TOP_K, t), lambda i, starts: (0, i), memory_space=pltpu.SMEM)
    return pl.pallas_call(
        _dispatch_kernel,
        out_shape=jax.ShapeDtypeStruct(xs_zero.shape, xs_zero.dtype),
        grid_spec=pltpu.PrefetchScalarGridSpec(
            num_scalar_prefetch=1,
            grid=(n // t,),
            in_specs=[slot, slot, pl.BlockSpec(memory_space=pl.ANY), pl.BlockSpec(memory_space=pl.ANY)],
            out_specs=pl.BlockSpec(memory_space=pl.ANY),
            scratch_shapes=[pltpu.SemaphoreType.DMA(())],
        ),
        input_output_aliases={4: 0},
        compiler_params=pltpu.CompilerParams(dimension_semantics=("arbitrary",)),
        name="dispatch",
    )(starts, idx, rank, xn, xs_zero)


def _experts_kernel(blk_e_ref, blk_rows_ref, xs_ref, wg_ref, wu_ref, wd_ref, y_ref):
    i = pl.program_id(0)

    @pl.when(blk_rows_ref[i] > 0)
    def _():
        xb = xs_ref[...].astype(jnp.bfloat16)
        g = _dot(xb, wg_ref[...].astype(jnp.bfloat16))
        u = _dot(xb, wu_ref[...].astype(jnp.bfloat16))
        g_sig, _ = _sigmoid_pair(g)
        hid = ((g * g_sig) * u).astype(jnp.bfloat16)
        y_ref[...] = _dot(hid, wd_ref[...].astype(jnp.bfloat16))

    @pl.when(blk_rows_ref[i] == 0)
    def _():
        y_ref[...] = jnp.zeros_like(y_ref)


def _experts(blk_e, blk_rows, xs, wg, wu, wd):
    rows, d = xs.shape
    ff = wg.shape[2]
    r = EXPERT_ROWS
    return pl.pallas_call(
        _experts_kernel,
        out_shape=jax.ShapeDtypeStruct((rows, d), jnp.float32),
        grid_spec=pltpu.PrefetchScalarGridSpec(
            num_scalar_prefetch=2,
            grid=(rows // r,),
            in_specs=[
                pl.BlockSpec((r, d), lambda i, be, br: (i, 0)),
                pl.BlockSpec((None, d, ff), lambda i, be, br: (be[i], 0, 0)),
                pl.BlockSpec((None, d, ff), lambda i, be, br: (be[i], 0, 0)),
                pl.BlockSpec((None, ff, d), lambda i, be, br: (be[i], 0, 0)),
            ],
            out_specs=pl.BlockSpec((r, d), lambda i, be, br: (i, 0)),
        ),
        compiler_params=pltpu.CompilerParams(
            dimension_semantics=("arbitrary",), vmem_limit_bytes=VMEM_LIMIT),
        name="experts",
    )(blk_e, blk_rows, xs, wg, wu, wd)


def _combine_kernel(starts_ref, idx_ref, rank_ref, idx_nx_ref, rank_nx_ref, base_ref, gates_ref,
                    gfin_ref, y_hbm, out_ref, buf_ref, sem):
    i = pl.program_id(0)
    n_steps = pl.num_programs(0)
    n_tok = base_ref.shape[0]

    def copy(idx_r, rank_r, slot, t, s):
        row = starts_ref[idx_r[s, t]] + rank_r[s, t]
        return pltpu.make_async_copy(y_hbm.at[pl.ds(row, 1)], buf_ref.at[slot, s, pl.ds(t, 1)],
                                     sem.at[slot])

    def issue(idx_r, rank_r, slot):
        def body(t, carry):
            for s in range(TOP_K):
                copy(idx_r, rank_r, slot, t, s).start()
            return carry
        lax.fori_loop(0, n_tok, body, 0)

    slot = i % 2

    @pl.when(i == 0)
    def _():
        issue(idx_ref, rank_ref, 0)

    @pl.when(i + 1 < n_steps)
    def _():
        issue(idx_nx_ref, rank_nx_ref, 1 - slot)

    def drain(t, carry):
        for s in range(TOP_K):
            copy(idx_ref, rank_ref, slot, t, s).wait()
        return carry
    lax.fori_loop(0, n_tok, drain, 0)

    acc = base_ref[...]
    gates = gates_ref[...]
    for s in range(TOP_K):
        acc = acc + gates[:, s:s + 1] * buf_ref[slot, s]
    out_ref[...] = _rms(acc, gfin_ref[...])


def _combine(starts, idx, rank, base, gates, gfin, y):
    n, d = base.shape
    t = COMBINE_ROWS
    n_steps = n // t
    cur = pl.BlockSpec((TOP_K, t), lambda i, starts: (0, i), memory_space=pltpu.SMEM)
    nxt = pl.BlockSpec((TOP_K, t), lambda i, starts: (0, jnp.minimum(i + 1, n_steps - 1)),
                       memory_space=pltpu.SMEM)
    return pl.pallas_call(
        _combine_kernel,
        out_shape=jax.ShapeDtypeStruct((n, d), jnp.float32),
        grid_spec=pltpu.PrefetchScalarGridSpec(
            num_scalar_prefetch=1,
            grid=(n_steps,),
            in_specs=[cur, cur, nxt, nxt,
                      pl.BlockSpec((t, d), lambda i, starts: (i, 0)),
                      pl.BlockSpec((t, TOP_K), lambda i, starts: (i, 0)),
                      pl.BlockSpec((1, d), lambda i, starts: (0, 0)),
                      pl.BlockSpec(memory_space=pl.ANY)],
            out_specs=pl.BlockSpec((t, d), lambda i, starts: (i, 0)),
            scratch_shapes=[pltpu.VMEM((2, TOP_K, t, d), jnp.float32),
                            pltpu.SemaphoreType.DMA((2,))],
        ),
        compiler_params=pltpu.CompilerParams(
            dimension_semantics=("arbitrary",), vmem_limit_bytes=VMEM_LIMIT),
        name="combine",
    )(starts, idx, rank, idx, rank, base, gates, gfin, y)


def kernel(x, meta_tokens, norm_mix_g, w_in, lb_table, hgrn_norm_g, conv_w, conv_norm_g, w_out,
           norm_ffn_g, w_router, router_bias, w_gate_e, w_up_e, w_down_e, w_gate_s, w_up_s, w_down_s,
           norm_final_g):
    bsz, seq, d = x.shape
    n = bsz * seq
    bf = jnp.bfloat16
    assert seq % MIX_ROWS == 0 and MIX_ROWS % CHUNK == 0
    assert n % ROUTE_ROWS == 0 and n % DISPATCH_ROWS == 0 and n % COMBINE_ROWS == 0
    assert (n * TOP_K) % EXPERT_ROWS == 0

    meta_pad = jnp.zeros((CHUNK, d), jnp.float32).at[CHUNK - N_META:].set(meta_tokens)
    h1 = _mixer(x, meta_pad, norm_mix_g[0:1], w_in[0].astype(bf), lb_table, hgrn_norm_g[0:1],
                conv_w[0], conv_norm_g[0:1], w_out[0].astype(bf))

    base, xn, idx, gate, rank, counts = _router(
        h1.reshape(n, d), norm_ffn_g[0:1], w_router[0].T, router_bias[0][:, None],
        w_gate_s[0].astype(bf), w_up_s[0].astype(bf), w_down_s[0].astype(bf))

    r = EXPERT_ROWS
    n_blocks = (n * TOP_K) // r + N_EXPERTS
    counts = counts[:, 0]
    padded = (counts + r - 1) // r * r
    ends = jnp.cumsum(padded)
    starts = (ends - padded).astype(jnp.int32)
    blk_row0 = jnp.arange(n_blocks, dtype=jnp.int32) * r
    blk_e = jnp.minimum(jnp.searchsorted(ends, blk_row0, side='right'), N_EXPERTS - 1).astype(jnp.int32)
    blk_rows = jnp.clip(counts[blk_e] - (blk_row0 - starts[blk_e]), 0, r).astype(jnp.int32)

    xs = _dispatch(starts, idx, rank, xn, jnp.zeros((n_blocks * r, d), jnp.float32))
    y = _experts(blk_e, blk_rows, xs, w_gate_e[0], w_up_e[0], w_down_e[0])
    out = _combine(starts, idx, rank, base, gate.T, norm_final_g[None, :], y)
    return out.reshape(bsz, seq, d)
```

```python
import functools

import numpy as np
import jax
import jax.numpy as jnp
from jax import lax
from jax.experimental import pallas as pl
from jax.experimental.pallas import tpu as pltpu

N_META = 16
CHUNK = 128
HEADS = 4
HEAD_DIM = 128
HGRN_W = HEADS * HEAD_DIM
CONV_W = 512
CONV_GROUPS = 4
CONV_K = 3
N_EXPERTS = 64
TOP_K = 8
ROUTED_SCALE = 2.5
EPS = 1e-6

V7X_VMEM_BYTES = 64 * 1024 * 1024
VMEM_LIMIT = V7X_VMEM_BYTES - 8 * 1024 * 1024

MIX_ROWS = 512
ROUTE_ROWS = 512
DISPATCH_ROWS = 512
EXPERT_ROWS = 512
COMBINE_ROWS = 256

ROW_TILE = 8

HALF_SPANS = (64, 32, 16, 8, 4, 2, 1)
N_LEVELS = len(HALF_SPANS) + 1


def _decay_sum_matrix():
    a = np.zeros((N_LEVELS, CHUNK, CHUNK), np.float32)
    a[0] = np.tril(np.ones((CHUNK, CHUNK), np.float32))
    for i, m in enumerate(HALF_SPANS):
        for t in range(CHUNK):
            mid = (t // (2 * m)) * 2 * m + m
            if t >= mid:
                a[1 + i, t, mid:t + 1] = 1.0
            else:
                a[1 + i, t, t + 1:mid] = 1.0
    return a.reshape(N_LEVELS * CHUNK, CHUNK)


def _level_matrix():
    lv = np.full((CHUNK, CHUNK), -1, np.int32)
    for t in range(CHUNK):
        lv[t, t] = len(HALF_SPANS)
        for s in range(t):
            top = (t ^ s).bit_length() - 1
            lv[t, s] = HALF_SPANS.index(1 << top)
    return lv


def _rms(x, g):
    return x * lax.rsqrt(jnp.mean(x * x, axis=-1, keepdims=True) + EPS) * g


def _group_rms(x, g, width):
    outs = []
    for j in range(x.shape[-1] // width):
        xs = x[:, j * width:(j + 1) * width]
        outs.append(xs * lax.rsqrt(jnp.mean(xs * xs, axis=-1, keepdims=True) + EPS))
    return jnp.concatenate(outs, axis=-1) * g


def _sigmoid_pair(z):
    t = jnp.exp(-jnp.abs(z))
    inv = 1.0 / (1.0 + t)
    big, small = inv, t * inv
    pos = z >= 0
    return jnp.where(pos, big, small), jnp.where(pos, small, big)


def _store_row_tiles(ref, x):
    t, w = x.shape[0], ref.shape[1]
    for c in range(ROW_TILE):
        ref[pl.ds(c, t, stride=ROW_TILE), :] = x[:, c * w:(c + 1) * w]


def _load_row_tiles(ref, t):
    return [ref[pl.ds(c, t, stride=ROW_TILE), :] for c in range(ROW_TILE)]


def _dot(a, b):
    return jnp.dot(a, b, preferred_element_type=jnp.float32)


def _dot_nt(a, b):
    return lax.dot_general(a, b, (((1,), (1,)), ((), ())), preferred_element_type=jnp.float32)


def _dot_tn(a, b):
    return lax.dot_general(a, b, (((0,), (0,)), ((), ())), preferred_element_type=jnp.float32)


def _hgrn_chunk(q, z, iv, lb, amat, level, st_ref, first_valid_row):
    sig, sig_neg = _sigmoid_pair(z)
    lf = jnp.log(lb + (1.0 - lb) * sig)
    k = (1.0 - lb) * sig_neg
    row = lax.broadcasted_iota(jnp.int32, (CHUNK, HGRN_W), 0)
    if first_valid_row:
        valid = row >= first_valid_row
        lf = jnp.where(valid, lf, 0.0)
        k = jnp.where(valid, k, 0.0)

    h1 = lf.astype(jnp.bfloat16)
    r1 = lf - h1.astype(jnp.float32)
    h2 = r1.astype(jnp.bfloat16)
    h3 = (r1 - h2.astype(jnp.float32)).astype(jnp.bfloat16)
    e_all = _dot(amat, h1) + _dot(amat, h2) + _dot(amat, h3)

    b = e_all[0:CHUNK]
    b_last = b[CHUNK - 1:CHUNK]
    q_in = (q * jnp.exp(b)).astype(jnp.bfloat16)
    k_out = (k * jnp.exp(b_last - b)).astype(jnp.bfloat16)
    st_decay = jnp.exp(b_last)
    v_bf = iv.astype(jnp.bfloat16)

    q_lv = [q.astype(jnp.bfloat16)]
    k_lv = [k.astype(jnp.bfloat16)]
    for i, m in enumerate(HALF_SPANS):
        ex = jnp.exp(e_all[(1 + i) * CHUNK:(2 + i) * CHUNK])
        right = (row & m) != 0
        q_lv.append(jnp.where(right, q * ex, 0.0).astype(jnp.bfloat16))
        k_lv.append(jnp.where(right, 0.0, k * ex).astype(jnp.bfloat16))
    lv_of = [len(HALF_SPANS)] + list(range(len(HALF_SPANS)))

    outs = []
    for h in range(HEADS):
        cols = slice(h * HEAD_DIM, (h + 1) * HEAD_DIM)
        scores = jnp.zeros((CHUNK, CHUNK), jnp.float32)
        for ql, kl, lv in zip(q_lv, k_lv, lv_of):
            scores = jnp.where(level == lv, _dot_nt(ql[:, cols], kl[:, cols]), scores)
        st = st_ref[h]
        o = _dot(scores.astype(jnp.bfloat16), v_bf[:, cols]) + _dot_nt(q_in[:, cols], st.astype(jnp.bfloat16))
        st_ref[h] = st * st_decay[:, cols] + _dot_tn(v_bf[:, cols], k_out[:, cols])
        outs.append(o)
    return jnp.concatenate(outs, axis=-1)


def _mixer_kernel(x_ref, meta_ref, gmix_ref, win_ref, lbt_ref, ghg_ref, cw_ref, gcv_ref, wout_ref,
                  amat_ref, level_ref, h1_ref, proj_ref, o_ref, u_ref, st_ref):
    j = pl.program_id(1)
    rows = x_ref.shape[0]
    n_in = win_ref.shape[1]

    lbt = lbt_ref[...]
    lbe = jnp.exp(lbt - jnp.max(lbt, axis=0, keepdims=True))
    lb = lbe[0:1] / jnp.sum(lbe, axis=0, keepdims=True)

    amat = amat_ref[...]
    level = level_ref[...]
    gmix = gmix_ref[...]

    def project(xv, dst_rows):
        xn = _rms(xv, gmix).astype(jnp.bfloat16)
        for c0 in range(0, n_in, 512):
            proj_ref[dst_rows, c0:c0 + 512] = _dot(xn, win_ref[:, c0:c0 + 512])

    @pl.when(j == 0)
    def _():
        st_ref[...] = jnp.zeros_like(st_ref)
        project(meta_ref[...], pl.ds(0, CHUNK))
        pm = proj_ref[0:CHUNK, :]
        _hgrn_chunk(pm[:, 0:512], pm[:, 512:1024], pm[:, 1024:1536], lb, amat, level, st_ref,
                    CHUNK - N_META)
        u_ref[0:8, :] = (pm[:, 2560:3072] * pm[:, 3072:3584])[CHUNK - 8:CHUNK]

    project(x_ref[...], pl.ds(0, rows))

    def chunk_body(c, carry):
        r0 = pl.multiple_of(c * CHUNK, CHUNK)
        q = proj_ref[pl.ds(r0, CHUNK), 0:512]
        z = proj_ref[pl.ds(r0, CHUNK), 512:1024]
        iv = proj_ref[pl.ds(r0, CHUNK), 1024:1536]
        o_ref[pl.ds(r0, CHUNK), :] = _hgrn_chunk(q, z, iv, lb, amat, level, st_ref, 0)
        return carry

    lax.fori_loop(0, rows // CHUNK, chunk_body, 0)

    g_out = proj_ref[:, 1536:2048]
    g_sig, _ = _sigmoid_pair(g_out)
    y_hgrn = _group_rms(o_ref[...], ghg_ref[...], HEAD_DIM) * (g_out * g_sig)

    u = proj_ref[:, 2560:3072] * proj_ref[:, 3072:3584]
    u_ref[8:8 + rows, :] = u
    cw = cw_ref[...]
    y = cw[2:3] * u + cw[1:2] * u_ref[7:7 + rows, :] + cw[0:1] * u_ref[6:6 + rows, :]
    u_ref[0:8, :] = u[rows - 8:rows]
    y_conv = _group_rms(proj_ref[:, 2048:2560] * y, gcv_ref[...], CONV_W // CONV_GROUPS)

    mixed = jnp.concatenate([y_hgrn, y_conv], axis=-1).astype(jnp.bfloat16)
    h1_ref[...] = x_ref[...] + _dot(mixed, wout_ref[...])


def _mixer(x, meta_pad, gmix, w_in, lb_table, ghg, conv_w, gcv, w_out):
    bsz, seq, d = x.shape
    n_in = w_in.shape[1]
    rows = MIX_ROWS
    const = lambda *shape: pl.BlockSpec(shape, lambda b, j: (0,) * len(shape))
    return pl.pallas_call(
        _mixer_kernel,
        out_shape=jax.ShapeDtypeStruct((bsz, seq, d), jnp.float32),
        grid=(bsz, seq // rows),
        in_specs=[
            pl.BlockSpec((None, rows, d), lambda b, j: (b, j, 0)),
            const(CHUNK, d), const(1, d), const(d, n_in), const(*lb_table.shape), const(1, HGRN_W),
            const(CONV_K, CONV_W), const(1, CONV_W), const(d, d),
            const(N_LEVELS * CHUNK, CHUNK), const(CHUNK, CHUNK),
        ],
        out_specs=pl.BlockSpec((None, rows, d), lambda b, j: (b, j, 0)),
        scratch_shapes=[
            pltpu.VMEM((rows, n_in), jnp.float32),
            pltpu.VMEM((rows, HGRN_W), jnp.float32),
            pltpu.VMEM((rows + 8, CONV_W), jnp.float32),
            pltpu.VMEM((HEADS, HEAD_DIM, HEAD_DIM), jnp.float32),
        ],
        compiler_params=pltpu.CompilerParams(
            dimension_semantics=("arbitrary", "arbitrary"), vmem_limit_bytes=VMEM_LIMIT),
        name="mixer",
    )(x, meta_pad, gmix, w_in, lb_table, ghg, conv_w, gcv, w_out,
      jnp.asarray(_decay_sum_matrix(), jnp.bfloat16), jnp.asarray(_level_matrix()))


def _router_kernel(h1_ref, gffn_ref, wrt_ref, bias_ref, wgs_ref, wus_ref, wds_ref, tri_ref,
                   base_ref, xn_ref, idx_ref, gate_ref, rank_ref, counts_ref, carry_ref):
    i = pl.program_id(0)
    n_tok = h1_ref.shape[0]

    @pl.when(i == 0)
    def _():
        carry_ref[...] = jnp.zeros_like(carry_ref)

    h1 = h1_ref[...]
    xn = _rms(h1, gffn_ref[...])
    _store_row_tiles(xn_ref, xn)
    xb = xn.astype(jnp.bfloat16)

    g_pre = _dot(xb, wgs_ref[...])
    gate_s, _ = _sigmoid_pair(g_pre)
    hid = (g_pre * gate_s) * _dot(xb, wus_ref[...])
    base_ref[...] = h1 + _dot(hid.astype(jnp.bfloat16), wds_ref[...])

    logits = lax.dot_general(wrt_ref[...], xn, (((1,), (1,)), ((), ())),
                             precision=lax.Precision.HIGHEST, preferred_element_type=jnp.float32)
    scores, _ = _sigmoid_pair(logits)
    sel = scores + bias_ref[...]
    eid = lax.broadcasted_iota(jnp.int32, (N_EXPERTS, n_tok), 0).astype(jnp.float32)
    picks, top_s = [], []
    for _ in range(TOP_K):
        best = jnp.max(sel, axis=0, keepdims=True)
        pick = jnp.min(jnp.where(sel == best, eid, float(N_EXPERTS)), axis=0, keepdims=True)
        hit = eid == pick
        top_s.append(jnp.sum(jnp.where(hit, scores, 0.0), axis=0, keepdims=True))
        sel = jnp.where(hit, -jnp.inf, sel)
        picks.append(pick)
    top_s = jnp.concatenate(top_s, axis=0)
    gate_ref[...] = top_s / jnp.sum(top_s, axis=0, keepdims=True) * ROUTED_SCALE
    idx_ref[...] = jnp.concatenate(picks, axis=0).astype(jnp.int32)

    chosen = jnp.zeros((N_EXPERTS, n_tok), jnp.float32)
    for pick in picks:
        chosen = chosen + jnp.where(eid == pick, 1.0, 0.0)
    incl = _dot(chosen.astype(jnp.bfloat16), tri_ref[...])
    before = carry_ref[...] + incl - 1.0
    ranks = [jnp.sum(jnp.where(eid == pick, before, 0.0), axis=0, keepdims=True) for pick in picks]
    rank_ref[...] = jnp.concatenate(ranks, axis=0).astype(jnp.int32)
    carry_ref[...] = carry_ref[...] + incl[:, n_tok - 1:n_tok]
    counts_ref[...] = jnp.broadcast_to(carry_ref[...], counts_ref.shape).astype(jnp.int32)


def _router(h1, gffn, w_router_t, bias, wgs, wus, wds):
    n, d = h1.shape
    t = ROUTE_ROWS
    ff = wgs.shape[1]
    tri = jnp.asarray(np.triu(np.ones((t, t), np.float32)), jnp.bfloat16)
    const = lambda *shape: pl.BlockSpec(shape, lambda i: (0,) * len(shape))
    tok = lambda width: pl.BlockSpec((t, width), lambda i: (i, 0))
    slot = pl.BlockSpec((TOP_K, t), lambda i: (0, i))
    return pl.pallas_call(
        _router_kernel,
        out_shape=(
            jax.ShapeDtypeStruct((n, d), jnp.float32),
            jax.ShapeDtypeStruct((n * ROW_TILE, d // ROW_TILE), jnp.float32),
            jax.ShapeDtypeStruct((TOP_K, n), jnp.int32),
            jax.ShapeDtypeStruct((TOP_K, n), jnp.float32),
            jax.ShapeDtypeStruct((TOP_K, n), jnp.int32),
            jax.ShapeDtypeStruct((N_EXPERTS, 128), jnp.int32),
        ),
        grid=(n // t,),
        in_specs=[tok(d), const(1, d), const(N_EXPERTS, d), const(N_EXPERTS, 1),
                  const(d, ff), const(d, ff), const(ff, d), const(t, t)],
        out_specs=(tok(d), pl.BlockSpec((t * ROW_TILE, d // ROW_TILE), lambda i: (i, 0)), slot, slot, slot,
                   const(N_EXPERTS, 128)),
        scratch_shapes=[pltpu.VMEM((N_EXPERTS, 1), jnp.float32)],
        compiler_params=pltpu.CompilerParams(
            dimension_semantics=("arbitrary",), vmem_limit_bytes=VMEM_LIMIT),
        name="router",
    )(h1, gffn, w_router_t, bias, wgs, wus, wds, tri)


def _row_tile(ref, row):
    return ref.at[pl.ds(pl.multiple_of(row * ROW_TILE, ROW_TILE), ROW_TILE)]


def _dispatch_kernel(starts_ref, idx_ref, rank_ref, xn_ref, xs_hbm, sem):
    n_tok = idx_ref.shape[1]

    def copy(t, s):
        dst = starts_ref[idx_ref[s, t]] + rank_ref[s, t]
        return pltpu.make_async_copy(_row_tile(xn_ref, t), _row_tile(xs_hbm, dst), sem)

    def issue(t, carry):
        for s in range(TOP_K):
            copy(t, s).start()
        return carry

    lax.fori_loop(0, n_tok, issue, 0)

    def drain(t, carry):
        for s in range(TOP_K):
            copy(t, s).wait()
        return carry

    lax.fori_loop(0, n_tok, drain, 0)


def _dispatch(starts, idx, rank, xn, n_rows):
    w = xn.shape[1]
    t = DISPATCH_ROWS
    slot = pl.BlockSpec((TOP_K, t), lambda i, starts: (0, i), memory_space=pltpu.SMEM)
    return pl.pallas_call(
        _dispatch_kernel,
        out_shape=jax.ShapeDtypeStruct((n_rows * ROW_TILE, w), jnp.float32),
        grid_spec=pltpu.PrefetchScalarGridSpec(
            num_scalar_prefetch=1,
            grid=(xn.shape[0] // (t * ROW_TILE),),
            in_specs=[slot, slot, pl.BlockSpec((t * ROW_TILE, w), lambda i, starts: (i, 0))],
            out_specs=pl.BlockSpec(memory_space=pl.ANY),
            scratch_shapes=[pltpu.SemaphoreType.DMA(())],
        ),
        compiler_params=pltpu.CompilerParams(dimension_semantics=("arbitrary",)),
        name="dispatch",
    )(starts, idx, rank, xn)


def _experts_kernel(blk_e_ref, blk_rows_ref, xs_ref, wg_ref, wu_ref, wd_ref, y_ref):
    i = pl.program_id(0)
    r = EXPERT_ROWS
    n_valid = blk_rows_ref[i]

    @pl.when(n_valid > 0)
    def _():
        keep = lax.broadcasted_iota(jnp.int32, (r, xs_ref.shape[1]), 0) < n_valid
        xb = jnp.concatenate([jnp.where(keep, c, 0.0).astype(jnp.bfloat16) for c in _load_row_tiles(xs_ref, r)],
                             axis=-1)
        g = _dot(xb, wg_ref[...].astype(jnp.bfloat16))
        u = _dot(xb, wu_ref[...].astype(jnp.bfloat16))
        g_sig, _ = _sigmoid_pair(g)
        hid = ((g * g_sig) * u).astype(jnp.bfloat16)
        _store_row_tiles(y_ref, _dot(hid, wd_ref[...].astype(jnp.bfloat16)))

    @pl.when(n_valid == 0)
    def _():
        y_ref[...] = jnp.zeros_like(y_ref)


def _experts(blk_e, blk_rows, xs, wg, wu, wd):
    w = xs.shape[1]
    d, ff = wg.shape[1], wg.shape[2]
    r = EXPERT_ROWS
    rows = pl.BlockSpec((r * ROW_TILE, w), lambda i, be, br: (i, 0))
    return pl.pallas_call(
        _experts_kernel,
        out_shape=jax.ShapeDtypeStruct(xs.shape, jnp.float32),
        grid_spec=pltpu.PrefetchScalarGridSpec(
            num_scalar_prefetch=2,
            grid=(xs.shape[0] // (r * ROW_TILE),),
            in_specs=[
                rows,
                pl.BlockSpec((None, d, ff), lambda i, be, br: (be[i], 0, 0)),
                pl.BlockSpec((None, d, ff), lambda i, be, br: (be[i], 0, 0)),
                pl.BlockSpec((None, ff, d), lambda i, be, br: (be[i], 0, 0)),
            ],
            out_specs=rows,
        ),
        compiler_params=pltpu.CompilerParams(
            dimension_semantics=("arbitrary",), vmem_limit_bytes=VMEM_LIMIT),
        name="experts",
    )(blk_e, blk_rows, xs, wg, wu, wd)


def _combine_kernel(starts_ref, idx_ref, rank_ref, idx_nx_ref, rank_nx_ref, base_ref, gates_ref,
                    gfin_ref, y_hbm, out_ref, buf_ref, sem):
    i = pl.program_id(0)
    n_steps = pl.num_programs(0)
    n_tok = base_ref.shape[0]
    w = y_hbm.shape[1]

    def copy(idx_r, rank_r, slot, t, s):
        row = starts_ref[idx_r[s, t]] + rank_r[s, t]
        return pltpu.make_async_copy(_row_tile(y_hbm, row), _row_tile(buf_ref.at[slot, s], t), sem.at[slot])

    def issue(idx_r, rank_r, slot):
        def body(t, carry):
            for s in range(TOP_K):
                copy(idx_r, rank_r, slot, t, s).start()
            return carry
        lax.fori_loop(0, n_tok, body, 0)

    slot = i % 2

    @pl.when(i == 0)
    def _():
        issue(idx_ref, rank_ref, 0)

    @pl.when(i + 1 < n_steps)
    def _():
        issue(idx_nx_ref, rank_nx_ref, 1 - slot)

    def drain(t, carry):
        for s in range(TOP_K):
            copy(idx_ref, rank_ref, slot, t, s).wait()
        return carry
    lax.fori_loop(0, n_tok, drain, 0)

    gates = gates_ref[...]
    chunks = []
    for c in range(ROW_TILE):
        acc = base_ref[:, c * w:(c + 1) * w]
        for s in range(TOP_K):
            acc = acc + gates[:, s:s + 1] * buf_ref[slot, s, pl.ds(c, n_tok, stride=ROW_TILE), :]
        chunks.append(acc)
    out_ref[...] = _rms(jnp.concatenate(chunks, axis=-1), gfin_ref[...])


def _combine(starts, idx, rank, base, gates, gfin, y):
    n, d = base.shape
    w = y.shape[1]
    t = COMBINE_ROWS
    n_steps = n // t
    cur = pl.BlockSpec((TOP_K, t), lambda i, starts: (0, i), memory_space=pltpu.SMEM)
    nxt = pl.BlockSpec((TOP_K, t), lambda i, starts: (0, jnp.minimum(i + 1, n_steps - 1)),
                       memory_space=pltpu.SMEM)
    return pl.pallas_call(
        _combine_kernel,
        out_shape=jax.ShapeDtypeStruct((n, d), jnp.float32),
        grid_spec=pltpu.PrefetchScalarGridSpec(
            num_scalar_prefetch=1,
            grid=(n_steps,),
            in_specs=[cur, cur, nxt, nxt,
                      pl.BlockSpec((t, d), lambda i, starts: (i, 0)),
                      pl.BlockSpec((t, TOP_K), lambda i, starts: (i, 0)),
                      pl.BlockSpec((1, d), lambda i, starts: (0, 0)),
                      pl.BlockSpec(memory_space=pl.ANY)],
            out_specs=pl.BlockSpec((t, d), lambda i, starts: (i, 0)),
            scratch_shapes=[pltpu.VMEM((2, TOP_K, t * ROW_TILE, w), jnp.float32),
                            pltpu.SemaphoreType.DMA((2,))],
        ),
        compiler_params=pltpu.CompilerParams(
            dimension_semantics=("arbitrary",), vmem_limit_bytes=VMEM_LIMIT),
        name="combine",
    )(starts, idx, rank, idx, rank, base, gates, gfin, y)


def kernel(x, meta_tokens, norm_mix_g, w_in, lb_table, hgrn_norm_g, conv_w, conv_norm_g, w_out,
           norm_ffn_g, w_router, router_bias, w_gate_e, w_up_e, w_down_e, w_gate_s, w_up_s, w_down_s,
           norm_final_g):
    bsz, seq, d = x.shape
    n = bsz * seq
    bf = jnp.bfloat16
    assert seq % MIX_ROWS == 0 and MIX_ROWS % CHUNK == 0
    assert n % ROUTE_ROWS == 0 and n % DISPATCH_ROWS == 0 and n % COMBINE_ROWS == 0
    assert (n * TOP_K) % EXPERT_ROWS == 0

    meta_pad = jnp.zeros((CHUNK, d), jnp.float32).at[CHUNK - N_META:].set(meta_tokens)
    h1 = _mixer(x, meta_pad, norm_mix_g[0:1], w_in[0].astype(bf), lb_table, hgrn_norm_g[0:1],
                conv_w[0], conv_norm_g[0:1], w_out[0].astype(bf))

    base, xn, idx, gate, rank, counts = _router(
        h1.reshape(n, d), norm_ffn_g[0:1], w_router[0].T, router_bias[0][:, None],
        w_gate_s[0].astype(bf), w_up_s[0].astype(bf), w_down_s[0].astype(bf))

    r = EXPERT_ROWS
    n_blocks = (n * TOP_K) // r + N_EXPERTS
    counts = counts[:, 0]
    padded = (counts + r - 1) // r * r
    ends = jnp.cumsum(padded)
    starts = (ends - padded).astype(jnp.int32)
    blk_row0 = jnp.arange(n_blocks, dtype=jnp.int32) * r
    blk_e = jnp.minimum(jnp.sum((ends[None, :] <= blk_row0[:, None]).astype(jnp.int32), axis=1), N_EXPERTS - 1)
    onehot = (blk_e[:, None] == jnp.arange(N_EXPERTS, dtype=jnp.int32)[None, :]).astype(jnp.int32)
    blk_rows = jnp.clip(onehot @ counts - (blk_row0 - onehot @ starts), 0, r).astype(jnp.int32)

    xs = _dispatch(starts, idx, rank, xn, n_blocks * r)
    y = _experts(blk_e, blk_rows, xs, w_gate_e[0], w_up_e[0], w_down_e[0])
    out = _combine(starts, idx, rank, base, gate.T, norm_final_g[None, :], y)
    return out.reshape(bsz, seq, d)
```

```python
import functools

import numpy as np
import jax
import jax.numpy as jnp
from jax import lax
from jax.experimental import pallas as pl
from jax.experimental.pallas import tpu as pltpu

N_META = 16
CHUNK = 128
HEADS = 4
HEAD_DIM = 128
HGRN_W = HEADS * HEAD_DIM
CONV_W = 512
CONV_GROUPS = 4
CONV_K = 3
N_EXPERTS = 64
TOP_K = 8
ROUTED_SCALE = 2.5
EPS = 1e-6

V7X_VMEM_BYTES = 64 * 1024 * 1024
VMEM_LIMIT = V7X_VMEM_BYTES - 8 * 1024 * 1024

MIX_ROWS = 512
ROUTE_ROWS = 512
DISPATCH_ROWS = 512
EXPERT_ROWS = 512
COMBINE_ROWS = 256

LANES = 128
ROW_TILE = 4

HALF_SPANS = (64, 32, 16, 8, 4, 2, 1)
N_LEVELS = len(HALF_SPANS) + 1


def _decay_sum_matrix():
    a = np.zeros((N_LEVELS, CHUNK, CHUNK), np.float32)
    a[0] = np.tril(np.ones((CHUNK, CHUNK), np.float32))
    for i, m in enumerate(HALF_SPANS):
        for t in range(CHUNK):
            mid = (t // (2 * m)) * 2 * m + m
            if t >= mid:
                a[1 + i, t, mid:t + 1] = 1.0
            else:
                a[1 + i, t, t + 1:mid] = 1.0
    return a.reshape(N_LEVELS * CHUNK, CHUNK)


def _level_matrix():
    lv = np.full((CHUNK, CHUNK), -1, np.int32)
    for t in range(CHUNK):
        lv[t, t] = len(HALF_SPANS)
        for s in range(t):
            top = (t ^ s).bit_length() - 1
            lv[t, s] = HALF_SPANS.index(1 << top)
    return lv


def _rms(x, g):
    return x * lax.rsqrt(jnp.mean(x * x, axis=-1, keepdims=True) + EPS) * g


def _group_rms(x, g, width):
    outs = []
    for j in range(x.shape[-1] // width):
        xs = x[:, j * width:(j + 1) * width]
        outs.append(xs * lax.rsqrt(jnp.mean(xs * xs, axis=-1, keepdims=True) + EPS))
    return jnp.concatenate(outs, axis=-1) * g


def _sigmoid_pair(z):
    t = jnp.exp(-jnp.abs(z))
    inv = 1.0 / (1.0 + t)
    big, small = inv, t * inv
    pos = z >= 0
    return jnp.where(pos, big, small), jnp.where(pos, small, big)


def _pack_rows(x):
    half = x.shape[1] // 2
    bits = lambda v: lax.bitcast_convert_type(v.astype(jnp.bfloat16).astype(jnp.float32), jnp.uint32)
    return (bits(x[:, :half]) >> 16) | (bits(x[:, half:]) & jnp.uint32(0xFFFF0000))


def _unpack_words(w):
    lo = lax.bitcast_convert_type(w << 16, jnp.float32)
    hi = lax.bitcast_convert_type(w & jnp.uint32(0xFFFF0000), jnp.float32)
    return lo, hi


def _store_row_tiles(ref, words):
    t = words.shape[0]
    for c in range(ROW_TILE):
        ref[pl.ds(c, t, stride=ROW_TILE), :] = words[:, c * LANES:(c + 1) * LANES]


def _load_row_tiles(ref, t):
    parts = [_unpack_words(ref[pl.ds(c, t, stride=ROW_TILE), :]) for c in range(ROW_TILE)]
    return [p[0] for p in parts], [p[1] for p in parts]


def _dot(a, b):
    return jnp.dot(a, b, preferred_element_type=jnp.float32)


def _dot_nt(a, b):
    return lax.dot_general(a, b, (((1,), (1,)), ((), ())), preferred_element_type=jnp.float32)


def _dot_tn(a, b):
    return lax.dot_general(a, b, (((0,), (0,)), ((), ())), preferred_element_type=jnp.float32)


def _hgrn_chunk(q, z, iv, lb, amat, level, st_ref, first_valid_row):
    sig, sig_neg = _sigmoid_pair(z)
    lf = jnp.log(lb + (1.0 - lb) * sig)
    k = (1.0 - lb) * sig_neg
    row = lax.broadcasted_iota(jnp.int32, (CHUNK, HGRN_W), 0)
    if first_valid_row:
        valid = row >= first_valid_row
        lf = jnp.where(valid, lf, 0.0)
        k = jnp.where(valid, k, 0.0)

    h1 = lf.astype(jnp.bfloat16)
    r1 = lf - h1.astype(jnp.float32)
    h2 = r1.astype(jnp.bfloat16)
    h3 = (r1 - h2.astype(jnp.float32)).astype(jnp.bfloat16)
    e_all = _dot(amat, h1) + _dot(amat, h2) + _dot(amat, h3)

    b = e_all[0:CHUNK]
    b_last = b[CHUNK - 1:CHUNK]
    q_in = (q * jnp.exp(b)).astype(jnp.bfloat16)
    k_out = (k * jnp.exp(b_last - b)).astype(jnp.bfloat16)
    st_decay = jnp.exp(b_last)
    v_bf = iv.astype(jnp.bfloat16)

    q_lv = [q.astype(jnp.bfloat16)]
    k_lv = [k.astype(jnp.bfloat16)]
    for i, m in enumerate(HALF_SPANS):
        ex = jnp.exp(e_all[(1 + i) * CHUNK:(2 + i) * CHUNK])
        right = (row & m) != 0
        q_lv.append(jnp.where(right, q * ex, 0.0).astype(jnp.bfloat16))
        k_lv.append(jnp.where(right, 0.0, k * ex).astype(jnp.bfloat16))
    lv_of = [len(HALF_SPANS)] + list(range(len(HALF_SPANS)))

    outs = []
    for h in range(HEADS):
        cols = slice(h * HEAD_DIM, (h + 1) * HEAD_DIM)
        scores = jnp.zeros((CHUNK, CHUNK), jnp.float32)
        for ql, kl, lv in zip(q_lv, k_lv, lv_of):
            scores = jnp.where(level == lv, _dot_nt(ql[:, cols], kl[:, cols]), scores)
        st = st_ref[h]
        o = _dot(scores.astype(jnp.bfloat16), v_bf[:, cols]) + _dot_nt(q_in[:, cols], st.astype(jnp.bfloat16))
        st_ref[h] = st * st_decay[:, cols] + _dot_tn(v_bf[:, cols], k_out[:, cols])
        outs.append(o)
    return jnp.concatenate(outs, axis=-1)


def _mixer_kernel(x_ref, meta_ref, gmix_ref, win_ref, lbt_ref, ghg_ref, cw_ref, gcv_ref, wout_ref,
                  amat_ref, level_ref, h1_ref, proj_ref, o_ref, u_ref, st_ref):
    j = pl.program_id(1)
    rows = x_ref.shape[0]
    n_in = win_ref.shape[1]

    lbt = lbt_ref[...]
    lbe = jnp.exp(lbt - jnp.max(lbt, axis=0, keepdims=True))
    lb = lbe[0:1] / jnp.sum(lbe, axis=0, keepdims=True)

    amat = amat_ref[...]
    level = level_ref[...]
    gmix = gmix_ref[...]

    def project(xv, dst_rows):
        xn = _rms(xv, gmix).astype(jnp.bfloat16)
        for c0 in range(0, n_in, 512):
            proj_ref[dst_rows, c0:c0 + 512] = _dot(xn, win_ref[:, c0:c0 + 512])

    @pl.when(j == 0)
    def _():
        st_ref[...] = jnp.zeros_like(st_ref)
        project(meta_ref[...], pl.ds(0, CHUNK))
        pm = proj_ref[0:CHUNK, :]
        _hgrn_chunk(pm[:, 0:512], pm[:, 512:1024], pm[:, 1024:1536], lb, amat, level, st_ref,
                    CHUNK - N_META)
        u_ref[0:8, :] = (pm[:, 2560:3072] * pm[:, 3072:3584])[CHUNK - 8:CHUNK]

    project(x_ref[...], pl.ds(0, rows))

    def chunk_body(c, carry):
        r0 = pl.multiple_of(c * CHUNK, CHUNK)
        q = proj_ref[pl.ds(r0, CHUNK), 0:512]
        z = proj_ref[pl.ds(r0, CHUNK), 512:1024]
        iv = proj_ref[pl.ds(r0, CHUNK), 1024:1536]
        o_ref[pl.ds(r0, CHUNK), :] = _hgrn_chunk(q, z, iv, lb, amat, level, st_ref, 0)
        return carry

    lax.fori_loop(0, rows // CHUNK, chunk_body, 0)

    g_out = proj_ref[:, 1536:2048]
    g_sig, _ = _sigmoid_pair(g_out)
    y_hgrn = _group_rms(o_ref[...], ghg_ref[...], HEAD_DIM) * (g_out * g_sig)

    u = proj_ref[:, 2560:3072] * proj_ref[:, 3072:3584]
    u_ref[8:8 + rows, :] = u
    cw = cw_ref[...]
    y = cw[2:3] * u + cw[1:2] * u_ref[7:7 + rows, :] + cw[0:1] * u_ref[6:6 + rows, :]
    u_ref[0:8, :] = u[rows - 8:rows]
    y_conv = _group_rms(proj_ref[:, 2048:2560] * y, gcv_ref[...], CONV_W // CONV_GROUPS)

    mixed = jnp.concatenate([y_hgrn, y_conv], axis=-1).astype(jnp.bfloat16)
    h1_ref[...] = x_ref[...] + _dot(mixed, wout_ref[...])


def _mixer(x, meta_pad, gmix, w_in, lb_table, ghg, conv_w, gcv, w_out):
    bsz, seq, d = x.shape
    n_in = w_in.shape[1]
    rows = MIX_ROWS
    const = lambda *shape: pl.BlockSpec(shape, lambda b, j: (0,) * len(shape))
    return pl.pallas_call(
        _mixer_kernel,
        out_shape=jax.ShapeDtypeStruct((bsz, seq, d), jnp.float32),
        grid=(bsz, seq // rows),
        in_specs=[
            pl.BlockSpec((None, rows, d), lambda b, j: (b, j, 0)),
            const(CHUNK, d), const(1, d), const(d, n_in), const(*lb_table.shape), const(1, HGRN_W),
            const(CONV_K, CONV_W), const(1, CONV_W), const(d, d),
            const(N_LEVELS * CHUNK, CHUNK), const(CHUNK, CHUNK),
        ],
        out_specs=pl.BlockSpec((None, rows, d), lambda b, j: (b, j, 0)),
        scratch_shapes=[
            pltpu.VMEM((rows, n_in), jnp.float32),
            pltpu.VMEM((rows, HGRN_W), jnp.float32),
            pltpu.VMEM((rows + 8, CONV_W), jnp.float32),
            pltpu.VMEM((HEADS, HEAD_DIM, HEAD_DIM), jnp.float32),
        ],
        compiler_params=pltpu.CompilerParams(
            dimension_semantics=("arbitrary", "arbitrary"), vmem_limit_bytes=VMEM_LIMIT),
        name="mixer",
    )(x, meta_pad, gmix, w_in, lb_table, ghg, conv_w, gcv, w_out,
      jnp.asarray(_decay_sum_matrix(), jnp.bfloat16), jnp.asarray(_level_matrix()))


def _router_kernel(h1_ref, gffn_ref, wrt_ref, bias_ref, wgs_ref, wus_ref, wds_ref, tri_ref,
                   base_ref, xn_ref, idx_ref, gate_ref, rank_ref, counts_ref, carry_ref):
    i = pl.program_id(0)
    n_tok = h1_ref.shape[0]

    @pl.when(i == 0)
    def _():
        carry_ref[...] = jnp.zeros_like(carry_ref)

    h1 = h1_ref[...]
    xn = _rms(h1, gffn_ref[...])
    _store_row_tiles(xn_ref, _pack_rows(xn))
    xb = xn.astype(jnp.bfloat16)

    g_pre = _dot(xb, wgs_ref[...])
    gate_s, _ = _sigmoid_pair(g_pre)
    hid = (g_pre * gate_s) * _dot(xb, wus_ref[...])
    base_ref[...] = h1 + _dot(hid.astype(jnp.bfloat16), wds_ref[...])

    logits = lax.dot_general(wrt_ref[...], xn, (((1,), (1,)), ((), ())),
                             precision=lax.Precision.HIGHEST, preferred_element_type=jnp.float32)
    scores, _ = _sigmoid_pair(logits)
    sel = scores + bias_ref[...]
    eid = lax.broadcasted_iota(jnp.int32, (N_EXPERTS, n_tok), 0).astype(jnp.float32)
    picks, top_s = [], []
    for _ in range(TOP_K):
        best = jnp.max(sel, axis=0, keepdims=True)
        pick = jnp.min(jnp.where(sel == best, eid, float(N_EXPERTS)), axis=0, keepdims=True)
        hit = eid == pick
        top_s.append(jnp.sum(jnp.where(hit, scores, 0.0), axis=0, keepdims=True))
        sel = jnp.where(hit, -jnp.inf, sel)
        picks.append(pick)
    top_s = jnp.concatenate(top_s, axis=0)
    gate_ref[...] = top_s / jnp.sum(top_s, axis=0, keepdims=True) * ROUTED_SCALE
    idx_ref[...] = jnp.concatenate(picks, axis=0).astype(jnp.int32)

    chosen = jnp.zeros((N_EXPERTS, n_tok), jnp.float32)
    for pick in picks:
        chosen = chosen + jnp.where(eid == pick, 1.0, 0.0)
    incl = _dot(chosen.astype(jnp.bfloat16), tri_ref[...])
    before = carry_ref[...] + incl - 1.0
    ranks = [jnp.sum(jnp.where(eid == pick, before, 0.0), axis=0, keepdims=True) for pick in picks]
    rank_ref[...] = jnp.concatenate(ranks, axis=0).astype(jnp.int32)
    carry_ref[...] = carry_ref[...] + incl[:, n_tok - 1:n_tok]
    counts_ref[...] = jnp.broadcast_to(carry_ref[...], counts_ref.shape).astype(jnp.int32)


def _router(h1, gffn, w_router_t, bias, wgs, wus, wds):
    n, d = h1.shape
    t = ROUTE_ROWS
    ff = wgs.shape[1]
    tri = jnp.asarray(np.triu(np.ones((t, t), np.float32)), jnp.bfloat16)
    const = lambda *shape: pl.BlockSpec(shape, lambda i: (0,) * len(shape))
    tok = lambda width: pl.BlockSpec((t, width), lambda i: (i, 0))
    slot = pl.BlockSpec((TOP_K, t), lambda i: (0, i))
    return pl.pallas_call(
        _router_kernel,
        out_shape=(
            jax.ShapeDtypeStruct((n, d), jnp.float32),
            jax.ShapeDtypeStruct((n * ROW_TILE, LANES), jnp.uint32),
            jax.ShapeDtypeStruct((TOP_K, n), jnp.int32),
            jax.ShapeDtypeStruct((TOP_K, n), jnp.float32),
            jax.ShapeDtypeStruct((TOP_K, n), jnp.int32),
            jax.ShapeDtypeStruct((N_EXPERTS, 128), jnp.int32),
        ),
        grid=(n // t,),
        in_specs=[tok(d), const(1, d), const(N_EXPERTS, d), const(N_EXPERTS, 1),
                  const(d, ff), const(d, ff), const(ff, d), const(t, t)],
        out_specs=(tok(d), pl.BlockSpec((t * ROW_TILE, LANES), lambda i: (i, 0)), slot, slot, slot,
                   const(N_EXPERTS, 128)),
        scratch_shapes=[pltpu.VMEM((N_EXPERTS, 1), jnp.float32)],
        compiler_params=pltpu.CompilerParams(
            dimension_semantics=("arbitrary",), vmem_limit_bytes=VMEM_LIMIT),
        name="router",
    )(h1, gffn, w_router_t, bias, wgs, wus, wds, tri)


def _row_tile(ref, row):
    return ref.at[pl.ds(pl.multiple_of(row * ROW_TILE, ROW_TILE), ROW_TILE)]


def _dispatch_kernel(starts_ref, idx_ref, rank_ref, xn_ref, xs_hbm, sem):
    n_tok = idx_ref.shape[1]

    def copy(t, s):
        dst = starts_ref[idx_ref[s, t]] + rank_ref[s, t]
        return pltpu.make_async_copy(_row_tile(xn_ref, t), _row_tile(xs_hbm, dst), sem)

    def issue(t, carry):
        for s in range(TOP_K):
            copy(t, s).start()
        return carry

    lax.fori_loop(0, n_tok, issue, 0)

    def drain(t, carry):
        for s in range(TOP_K):
            copy(t, s).wait()
        return carry

    lax.fori_loop(0, n_tok, drain, 0)


def _dispatch(starts, idx, rank, xn, n_rows):
    w = xn.shape[1]
    t = DISPATCH_ROWS
    slot = pl.BlockSpec((TOP_K, t), lambda i, starts: (0, i), memory_space=pltpu.SMEM)
    return pl.pallas_call(
        _dispatch_kernel,
        out_shape=jax.ShapeDtypeStruct((n_rows * ROW_TILE, w), xn.dtype),
        grid_spec=pltpu.PrefetchScalarGridSpec(
            num_scalar_prefetch=1,
            grid=(xn.shape[0] // (t * ROW_TILE),),
            in_specs=[slot, slot, pl.BlockSpec((t * ROW_TILE, w), lambda i, starts: (i, 0))],
            out_specs=pl.BlockSpec(memory_space=pl.ANY),
            scratch_shapes=[pltpu.SemaphoreType.DMA(())],
        ),
        compiler_params=pltpu.CompilerParams(dimension_semantics=("arbitrary",)),
        name="dispatch",
    )(starts, idx, rank, xn)


def _experts_kernel(blk_e_ref, blk_rows_ref, xs_ref, wg_ref, wu_ref, wd_ref, y_ref, wgb_ref, wub_ref, wdb_ref):
    i = pl.program_id(0)
    r = EXPERT_ROWS
    n_valid = blk_rows_ref[i]

    @pl.when((i == 0) | (blk_e_ref[i] != blk_e_ref[jnp.maximum(i - 1, 0)]))
    def _():
        wgb_ref[...] = wg_ref[...].astype(jnp.bfloat16)
        wub_ref[...] = wu_ref[...].astype(jnp.bfloat16)
        wdb_ref[...] = wd_ref[...].astype(jnp.bfloat16)

    @pl.when(n_valid > 0)
    def _():
        keep = lax.broadcasted_iota(jnp.int32, (r, LANES), 0) < n_valid
        lo, hi = _load_row_tiles(xs_ref, r)
        xb = jnp.concatenate([jnp.where(keep, c, 0.0).astype(jnp.bfloat16) for c in lo + hi], axis=-1)
        g = _dot(xb, wgb_ref[...])
        u = _dot(xb, wub_ref[...])
        g_sig, _ = _sigmoid_pair(g)
        hid = ((g * g_sig) * u).astype(jnp.bfloat16)
        _store_row_tiles(y_ref, _pack_rows(_dot(hid, wdb_ref[...])))

    @pl.when(n_valid == 0)
    def _():
        y_ref[...] = jnp.zeros_like(y_ref)


def _experts(blk_e, blk_rows, xs, wg, wu, wd):
    w = xs.shape[1]
    d, ff = wg.shape[1], wg.shape[2]
    r = EXPERT_ROWS
    rows = pl.BlockSpec((r * ROW_TILE, w), lambda i, be, br: (i, 0))
    return pl.pallas_call(
        _experts_kernel,
        out_shape=jax.ShapeDtypeStruct(xs.shape, xs.dtype),
        grid_spec=pltpu.PrefetchScalarGridSpec(
            num_scalar_prefetch=2,
            grid=(xs.shape[0] // (r * ROW_TILE),),
            in_specs=[
                rows,
                pl.BlockSpec((None, d, ff), lambda i, be, br: (be[i], 0, 0)),
                pl.BlockSpec((None, d, ff), lambda i, be, br: (be[i], 0, 0)),
                pl.BlockSpec((None, ff, d), lambda i, be, br: (be[i], 0, 0)),
            ],
            out_specs=rows,
            scratch_shapes=[pltpu.VMEM((d, ff), jnp.bfloat16), pltpu.VMEM((d, ff), jnp.bfloat16),
                            pltpu.VMEM((ff, d), jnp.bfloat16)],
        ),
        compiler_params=pltpu.CompilerParams(
            dimension_semantics=("arbitrary",), vmem_limit_bytes=VMEM_LIMIT),
        name="experts",
    )(blk_e, blk_rows, xs, wg, wu, wd)


def _combine_kernel(starts_ref, idx_ref, rank_ref, idx_nx_ref, rank_nx_ref, base_ref, gates_ref,
                    gfin_ref, y_hbm, out_ref, buf_ref, sem):
    i = pl.program_id(0)
    n_steps = pl.num_programs(0)
    n_tok = base_ref.shape[0]

    def copy(idx_r, rank_r, slot, t, s):
        row = starts_ref[idx_r[s, t]] + rank_r[s, t]
        return pltpu.make_async_copy(_row_tile(y_hbm, row), _row_tile(buf_ref.at[slot, s], t), sem.at[slot])

    def issue(idx_r, rank_r, slot):
        def body(t, carry):
            for s in range(TOP_K):
                copy(idx_r, rank_r, slot, t, s).start()
            return carry
        lax.fori_loop(0, n_tok, body, 0)

    slot = i % 2

    @pl.when(i == 0)
    def _():
        issue(idx_ref, rank_ref, 0)

    @pl.when(i + 1 < n_steps)
    def _():
        issue(idx_nx_ref, rank_nx_ref, 1 - slot)

    def drain(t, carry):
        for s in range(TOP_K):
            copy(idx_ref, rank_ref, slot, t, s).wait()
        return carry
    lax.fori_loop(0, n_tok, drain, 0)

    gates = gates_ref[...]
    half = ROW_TILE * LANES
    lo_chunks, hi_chunks = [], []
    for c in range(ROW_TILE):
        acc_lo = base_ref[:, c * LANES:(c + 1) * LANES]
        acc_hi = base_ref[:, half + c * LANES:half + (c + 1) * LANES]
        for s in range(TOP_K):
            lo, hi = _unpack_words(buf_ref[slot, s, pl.ds(c, n_tok, stride=ROW_TILE), :])
            acc_lo = acc_lo + gates[:, s:s + 1] * lo
            acc_hi = acc_hi + gates[:, s:s + 1] * hi
        lo_chunks.append(acc_lo)
        hi_chunks.append(acc_hi)
    out_ref[...] = _rms(jnp.concatenate(lo_chunks + hi_chunks, axis=-1), gfin_ref[...])


def _combine(starts, idx, rank, base, gates, gfin, y):
    n, d = base.shape
    w = y.shape[1]
    t = COMBINE_ROWS
    n_steps = n // t
    cur = pl.BlockSpec((TOP_K, t), lambda i, starts: (0, i), memory_space=pltpu.SMEM)
    nxt = pl.BlockSpec((TOP_K, t), lambda i, starts: (0, jnp.minimum(i + 1, n_steps - 1)),
                       memory_space=pltpu.SMEM)
    return pl.pallas_call(
        _combine_kernel,
        out_shape=jax.ShapeDtypeStruct((n, d), jnp.float32),
        grid_spec=pltpu.PrefetchScalarGridSpec(
            num_scalar_prefetch=1,
            grid=(n_steps,),
            in_specs=[cur, cur, nxt, nxt,
                      pl.BlockSpec((t, d), lambda i, starts: (i, 0)),
                      pl.BlockSpec((t, TOP_K), lambda i, starts: (i, 0)),
                      pl.BlockSpec((1, d), lambda i, starts: (0, 0)),
                      pl.BlockSpec(memory_space=pl.ANY)],
            out_specs=pl.BlockSpec((t, d), lambda i, starts: (i, 0)),
            scratch_shapes=[pltpu.VMEM((2, TOP_K, t * ROW_TILE, w), y.dtype),
                            pltpu.SemaphoreType.DMA((2,))],
        ),
        compiler_params=pltpu.CompilerParams(
            dimension_semantics=("arbitrary",), vmem_limit_bytes=VMEM_LIMIT),
        name="combine",
    )(starts, idx, rank, idx, rank, base, gates, gfin, y)


def kernel(x, meta_tokens, norm_mix_g, w_in, lb_table, hgrn_norm_g, conv_w, conv_norm_g, w_out,
           norm_ffn_g, w_router, router_bias, w_gate_e, w_up_e, w_down_e, w_gate_s, w_up_s, w_down_s,
           norm_final_g):
    bsz, seq, d = x.shape
    n = bsz * seq
    bf = jnp.bfloat16
    assert seq % MIX_ROWS == 0 and MIX_ROWS % CHUNK == 0
    assert n % ROUTE_ROWS == 0 and n % DISPATCH_ROWS == 0 and n % COMBINE_ROWS == 0
    assert (n * TOP_K) % EXPERT_ROWS == 0

    meta_pad = jnp.zeros((CHUNK, d), jnp.float32).at[CHUNK - N_META:].set(meta_tokens)
    h1 = _mixer(x, meta_pad, norm_mix_g[0:1], w_in[0].astype(bf), lb_table, hgrn_norm_g[0:1],
                conv_w[0], conv_norm_g[0:1], w_out[0].astype(bf))

    base, xn, idx, gate, rank, counts = _router(
        h1.reshape(n, d), norm_ffn_g[0:1], w_router[0].T, router_bias[0][:, None],
        w_gate_s[0].astype(bf), w_up_s[0].astype(bf), w_down_s[0].astype(bf))

    r = EXPERT_ROWS
    n_blocks = (n * TOP_K) // r + N_EXPERTS
    counts = counts[:, 0]
    padded = (counts + r - 1) // r * r
    ends = jnp.cumsum(padded)
    starts = (ends - padded).astype(jnp.int32)
    blk_row0 = jnp.arange(n_blocks, dtype=jnp.int32) * r
    blk_e = jnp.minimum(jnp.sum((ends[None, :] <= blk_row0[:, None]).astype(jnp.int32), axis=1), N_EXPERTS - 1)
    onehot = (blk_e[:, None] == jnp.arange(N_EXPERTS, dtype=jnp.int32)[None, :]).astype(jnp.int32)
    blk_rows = jnp.clip(onehot @ counts - (blk_row0 - onehot @ starts), 0, r).astype(jnp.int32)

    xs = _dispatch(starts, idx, rank, xn, n_blocks * r)
    y = _experts(blk_e, blk_rows, xs, w_gate_e[0], w_up_e[0], w_down_e[0])
    out = _combine(starts, idx, rank, base, gate.T, norm_final_g[None, :], y)
    return out.reshape(bsz, seq, d)
```

```python
import functools

import numpy as np
import jax
import jax.numpy as jnp
from jax import lax
from jax.experimental import pallas as pl
from jax.experimental.pallas import tpu as pltpu

N_META = 16
CHUNK = 128
HEADS = 4
HEAD_DIM = 128
HGRN_W = HEADS * HEAD_DIM
CONV_W = 512
CONV_GROUPS = 4
CONV_K = 3
N_EXPERTS = 64
TOP_K = 8
ROUTED_SCALE = 2.5
EPS = 1e-6

V7X_VMEM_BYTES = 64 * 1024 * 1024
VMEM_LIMIT = V7X_VMEM_BYTES - 8 * 1024 * 1024

MIX_ROWS = 512
ROUTE_ROWS = 512
DISPATCH_ROWS = 512
EXPERT_ROWS = 512
COMBINE_ROWS = 256

LANES = 128
ROW_TILE = 4

HALF_SPANS = (64, 32, 16, 8, 4, 2, 1)
N_LEVELS = len(HALF_SPANS) + 1


def _decay_sum_matrix():
    a = np.zeros((N_LEVELS, CHUNK, CHUNK), np.float32)
    a[0] = np.tril(np.ones((CHUNK, CHUNK), np.float32))
    for i, m in enumerate(HALF_SPANS):
        for t in range(CHUNK):
            mid = (t // (2 * m)) * 2 * m + m
            if t >= mid:
                a[1 + i, t, mid:t + 1] = 1.0
            else:
                a[1 + i, t, t + 1:mid] = 1.0
    return a.reshape(N_LEVELS * CHUNK, CHUNK)


def _level_matrix():
    lv = np.full((CHUNK, CHUNK), -1, np.int32)
    for t in range(CHUNK):
        lv[t, t] = len(HALF_SPANS)
        for s in range(t):
            top = (t ^ s).bit_length() - 1
            lv[t, s] = HALF_SPANS.index(1 << top)
    return lv


def _rms(x, g):
    return x * lax.rsqrt(jnp.mean(x * x, axis=-1, keepdims=True) + EPS) * g


def _group_rms(x, g, width):
    outs = []
    for j in range(x.shape[-1] // width):
        xs = x[:, j * width:(j + 1) * width]
        outs.append(xs * lax.rsqrt(jnp.mean(xs * xs, axis=-1, keepdims=True) + EPS))
    return jnp.concatenate(outs, axis=-1) * g


def _sigmoid_pair(z):
    t = jnp.exp(-jnp.abs(z))
    inv = 1.0 / (1.0 + t)
    big, small = inv, t * inv
    pos = z >= 0
    return jnp.where(pos, big, small), jnp.where(pos, small, big)


def _pack_rows(x):
    half = x.shape[1] // 2
    bits = lambda v: lax.bitcast_convert_type(v.astype(jnp.bfloat16).astype(jnp.float32), jnp.uint32)
    return (bits(x[:, :half]) >> 16) | (bits(x[:, half:]) & jnp.uint32(0xFFFF0000))


def _unpack_words(w):
    lo = lax.bitcast_convert_type(w << 16, jnp.float32)
    hi = lax.bitcast_convert_type(w & jnp.uint32(0xFFFF0000), jnp.float32)
    return lo, hi


def _store_row_tiles(ref, words):
    t = words.shape[0]
    for c in range(ROW_TILE):
        ref[pl.ds(c, t, stride=ROW_TILE), :] = words[:, c * LANES:(c + 1) * LANES]


def _load_row_tiles(ref, t):
    parts = [_unpack_words(ref[pl.ds(c, t, stride=ROW_TILE), :]) for c in range(ROW_TILE)]
    return [p[0] for p in parts], [p[1] for p in parts]


def _dot(a, b):
    return jnp.dot(a, b, preferred_element_type=jnp.float32)


def _dot_nt(a, b):
    return lax.dot_general(a, b, (((1,), (1,)), ((), ())), preferred_element_type=jnp.float32)


def _dot_tn(a, b):
    return lax.dot_general(a, b, (((0,), (0,)), ((), ())), preferred_element_type=jnp.float32)


def _hgrn_chunk(q, z, iv, lb, amat, level, st_ref, first_valid_row):
    sig, sig_neg = _sigmoid_pair(z)
    lf = jnp.log(lb + (1.0 - lb) * sig)
    k = (1.0 - lb) * sig_neg
    row = lax.broadcasted_iota(jnp.int32, (CHUNK, HGRN_W), 0)
    if first_valid_row:
        valid = row >= first_valid_row
        lf = jnp.where(valid, lf, 0.0)
        k = jnp.where(valid, k, 0.0)

    h1 = lf.astype(jnp.bfloat16)
    r1 = lf - h1.astype(jnp.float32)
    h2 = r1.astype(jnp.bfloat16)
    h3 = (r1 - h2.astype(jnp.float32)).astype(jnp.bfloat16)
    e_all = _dot(amat, h1) + _dot(amat, h2) + _dot(amat, h3)

    b = e_all[0:CHUNK]
    b_last = b[CHUNK - 1:CHUNK]
    q_in = (q * jnp.exp(b)).astype(jnp.bfloat16)
    k_out = (k * jnp.exp(b_last - b)).astype(jnp.bfloat16)
    st_decay = jnp.exp(b_last)
    v_bf = iv.astype(jnp.bfloat16)

    q_lv = [q.astype(jnp.bfloat16)]
    k_lv = [k.astype(jnp.bfloat16)]
    for i, m in enumerate(HALF_SPANS):
        ex = jnp.exp(e_all[(1 + i) * CHUNK:(2 + i) * CHUNK])
        right = (row & m) != 0
        q_lv.append(jnp.where(right, q * ex, 0.0).astype(jnp.bfloat16))
        k_lv.append(jnp.where(right, 0.0, k * ex).astype(jnp.bfloat16))
    lv_of = [len(HALF_SPANS)] + list(range(len(HALF_SPANS)))

    outs = []
    for h in range(HEADS):
        cols = slice(h * HEAD_DIM, (h + 1) * HEAD_DIM)
        scores = jnp.zeros((CHUNK, CHUNK), jnp.float32)
        for ql, kl, lv in zip(q_lv, k_lv, lv_of):
            scores = jnp.where(level == lv, _dot_nt(ql[:, cols], kl[:, cols]), scores)
        st = st_ref[h]
        o = _dot(scores.astype(jnp.bfloat16), v_bf[:, cols]) + _dot_nt(q_in[:, cols], st.astype(jnp.bfloat16))
        st_ref[h] = st * st_decay[:, cols] + _dot_tn(v_bf[:, cols], k_out[:, cols])
        outs.append(o)
    return jnp.concatenate(outs, axis=-1)


def _mixer_kernel(x_ref, meta_ref, gmix_ref, win_ref, lbt_ref, ghg_ref, cw_ref, gcv_ref, wout_ref,
                  amat_ref, level_ref, h1_ref, proj_ref, o_ref, u_ref, st_ref):
    j = pl.program_id(1)
    rows = x_ref.shape[0]
    n_in = win_ref.shape[1]

    lbt = lbt_ref[...]
    lbe = jnp.exp(lbt - jnp.max(lbt, axis=0, keepdims=True))
    lb = lbe[0:1] / jnp.sum(lbe, axis=0, keepdims=True)

    amat = amat_ref[...]
    level = level_ref[...]
    gmix = gmix_ref[...]

    def project(xv, dst_rows):
        xn = _rms(xv, gmix).astype(jnp.bfloat16)
        for c0 in range(0, n_in, 512):
            proj_ref[dst_rows, c0:c0 + 512] = _dot(xn, win_ref[:, c0:c0 + 512])

    @pl.when(j == 0)
    def _():
        st_ref[...] = jnp.zeros_like(st_ref)
        project(meta_ref[...], pl.ds(0, CHUNK))
        pm = proj_ref[0:CHUNK, :]
        _hgrn_chunk(pm[:, 0:512], pm[:, 512:1024], pm[:, 1024:1536], lb, amat, level, st_ref,
                    CHUNK - N_META)
        u_ref[0:8, :] = (pm[:, 2560:3072] * pm[:, 3072:3584])[CHUNK - 8:CHUNK]

    project(x_ref[...], pl.ds(0, rows))

    def chunk_body(c, carry):
        r0 = pl.multiple_of(c * CHUNK, CHUNK)
        q = proj_ref[pl.ds(r0, CHUNK), 0:512]
        z = proj_ref[pl.ds(r0, CHUNK), 512:1024]
        iv = proj_ref[pl.ds(r0, CHUNK), 1024:1536]
        o_ref[pl.ds(r0, CHUNK), :] = _hgrn_chunk(q, z, iv, lb, amat, level, st_ref, 0)
        return carry

    lax.fori_loop(0, rows // CHUNK, chunk_body, 0)

    g_out = proj_ref[:, 1536:2048]
    g_sig, _ = _sigmoid_pair(g_out)
    y_hgrn = _group_rms(o_ref[...], ghg_ref[...], HEAD_DIM) * (g_out * g_sig)

    u = proj_ref[:, 2560:3072] * proj_ref[:, 3072:3584]
    u_ref[8:8 + rows, :] = u
    cw = cw_ref[...]
    y = cw[2:3] * u + cw[1:2] * u_ref[7:7 + rows, :] + cw[0:1] * u_ref[6:6 + rows, :]
    u_ref[0:8, :] = u[rows - 8:rows]
    y_conv = _group_rms(proj_ref[:, 2048:2560] * y, gcv_ref[...], CONV_W // CONV_GROUPS)

    mixed = jnp.concatenate([y_hgrn, y_conv], axis=-1).astype(jnp.bfloat16)
    h1_ref[...] = x_ref[...] + _dot(mixed, wout_ref[...])


def _mixer(x, meta_pad, gmix, w_in, lb_table, ghg, conv_w, gcv, w_out):
    bsz, seq, d = x.shape
    n_in = w_in.shape[1]
    rows = MIX_ROWS
    const = lambda *shape: pl.BlockSpec(shape, lambda b, j: (0,) * len(shape))
    return pl.pallas_call(
        _mixer_kernel,
        out_shape=jax.ShapeDtypeStruct((bsz, seq, d), jnp.float32),
        grid=(bsz, seq // rows),
        in_specs=[
            pl.BlockSpec((None, rows, d), lambda b, j: (b, j, 0)),
            const(CHUNK, d), const(1, d), const(d, n_in), const(*lb_table.shape), const(1, HGRN_W),
            const(CONV_K, CONV_W), const(1, CONV_W), const(d, d),
            const(N_LEVELS * CHUNK, CHUNK), const(CHUNK, CHUNK),
        ],
        out_specs=pl.BlockSpec((None, rows, d), lambda b, j: (b, j, 0)),
        scratch_shapes=[
            pltpu.VMEM((rows, n_in), jnp.float32),
            pltpu.VMEM((rows, HGRN_W), jnp.float32),
            pltpu.VMEM((rows + 8, CONV_W), jnp.float32),
            pltpu.VMEM((HEADS, HEAD_DIM, HEAD_DIM), jnp.float32),
        ],
        compiler_params=pltpu.CompilerParams(
            dimension_semantics=("arbitrary", "arbitrary"), vmem_limit_bytes=VMEM_LIMIT),
        name="mixer",
    )(x, meta_pad, gmix, w_in, lb_table, ghg, conv_w, gcv, w_out,
      jnp.asarray(_decay_sum_matrix(), jnp.bfloat16), jnp.asarray(_level_matrix()))


def _router_kernel(h1_ref, gffn_ref, wrt_ref, bias_ref, wgs_ref, wus_ref, wds_ref, tri_ref,
                   base_ref, xn_ref, idx_ref, gate_ref, rank_ref, counts_ref, carry_ref):
    i = pl.program_id(0)
    n_tok = h1_ref.shape[0]

    @pl.when(i == 0)
    def _():
        carry_ref[...] = jnp.zeros_like(carry_ref)

    h1 = h1_ref[...]
    xn = _rms(h1, gffn_ref[...])
    _store_row_tiles(xn_ref, _pack_rows(xn))
    xb = xn.astype(jnp.bfloat16)

    g_pre = _dot(xb, wgs_ref[...])
    gate_s, _ = _sigmoid_pair(g_pre)
    hid = (g_pre * gate_s) * _dot(xb, wus_ref[...])
    base_ref[...] = h1 + _dot(hid.astype(jnp.bfloat16), wds_ref[...])

    logits = lax.dot_general(wrt_ref[...], xn, (((1,), (1,)), ((), ())),
                             precision=lax.Precision.HIGHEST, preferred_element_type=jnp.float32)
    scores, _ = _sigmoid_pair(logits)
    sel = scores + bias_ref[...]
    eid = lax.broadcasted_iota(jnp.int32, (N_EXPERTS, n_tok), 0).astype(jnp.float32)
    picks, top_s = [], []
    for _ in range(TOP_K):
        best = jnp.max(sel, axis=0, keepdims=True)
        pick = jnp.min(jnp.where(sel == best, eid, float(N_EXPERTS)), axis=0, keepdims=True)
        hit = eid == pick
        top_s.append(jnp.sum(jnp.where(hit, scores, 0.0), axis=0, keepdims=True))
        sel = jnp.where(hit, -jnp.inf, sel)
        picks.append(pick)
    top_s = jnp.concatenate(top_s, axis=0)
    gate_ref[...] = top_s / jnp.sum(top_s, axis=0, keepdims=True) * ROUTED_SCALE
    idx_ref[...] = jnp.concatenate(picks, axis=0).astype(jnp.int32)

    chosen = jnp.zeros((N_EXPERTS, n_tok), jnp.float32)
    for pick in picks:
        chosen = chosen + jnp.where(eid == pick, 1.0, 0.0)
    incl = _dot(chosen.astype(jnp.bfloat16), tri_ref[...])
    before = carry_ref[...] + incl - 1.0
    ranks = [jnp.sum(jnp.where(eid == pick, before, 0.0), axis=0, keepdims=True) for pick in picks]
    rank_ref[...] = jnp.concatenate(ranks, axis=0).astype(jnp.int32)
    carry_ref[...] = carry_ref[...] + incl[:, n_tok - 1:n_tok]
    counts_ref[...] = jnp.broadcast_to(carry_ref[...], counts_ref.shape).astype(jnp.int32)


def _router(h1, gffn, w_router_t, bias, wgs, wus, wds):
    n, d = h1.shape
    t = ROUTE_ROWS
    ff = wgs.shape[1]
    tri = jnp.asarray(np.triu(np.ones((t, t), np.float32)), jnp.bfloat16)
    const = lambda *shape: pl.BlockSpec(shape, lambda i: (0,) * len(shape))
    tok = lambda width: pl.BlockSpec((t, width), lambda i: (i, 0))
    slot = pl.BlockSpec((TOP_K, t), lambda i: (0, i))
    return pl.pallas_call(
        _router_kernel,
        out_shape=(
            jax.ShapeDtypeStruct((n, d), jnp.float32),
            jax.ShapeDtypeStruct((n * ROW_TILE, LANES), jnp.uint32),
            jax.ShapeDtypeStruct((TOP_K, n), jnp.int32),
            jax.ShapeDtypeStruct((TOP_K, n), jnp.float32),
            jax.ShapeDtypeStruct((TOP_K, n), jnp.int32),
            jax.ShapeDtypeStruct((N_EXPERTS, 128), jnp.int32),
        ),
        grid=(n // t,),
        in_specs=[tok(d), const(1, d), const(N_EXPERTS, d), const(N_EXPERTS, 1),
                  const(d, ff), const(d, ff), const(ff, d), const(t, t)],
        out_specs=(tok(d), pl.BlockSpec((t * ROW_TILE, LANES), lambda i: (i, 0)), slot, slot, slot,
                   const(N_EXPERTS, 128)),
        scratch_shapes=[pltpu.VMEM((N_EXPERTS, 1), jnp.float32)],
        compiler_params=pltpu.CompilerParams(
            dimension_semantics=("arbitrary",), vmem_limit_bytes=VMEM_LIMIT),
        name="router",
    )(h1, gffn, w_router_t, bias, wgs, wus, wds, tri)


def _slab(ref, first_sublane):
    return ref.at[pl.ds(pl.multiple_of(first_sublane, ROW_TILE), ROW_TILE)]


def _dispatch_kernel(dst_ref, xn_ref, xs_hbm, sem):
    n_tok = xn_ref.shape[0] // ROW_TILE

    def issue(t, carry):
        for s in range(TOP_K):
            pltpu.make_async_copy(_slab(xn_ref, t * ROW_TILE), _slab(xs_hbm, dst_ref[0, t * TOP_K + s]),
                                  sem).start()
        return carry

    lax.fori_loop(0, n_tok, issue, 0)

    for s in range(TOP_K):
        pltpu.make_async_copy(xn_ref, xs_hbm.at[pl.ds(0, xn_ref.shape[0])], sem).wait()


def _dispatch(dst, xn, n_rows):
    w = xn.shape[1]
    t = DISPATCH_ROWS
    return pl.pallas_call(
        _dispatch_kernel,
        out_shape=jax.ShapeDtypeStruct((n_rows * ROW_TILE, w), xn.dtype),
        grid_spec=pltpu.PrefetchScalarGridSpec(
            num_scalar_prefetch=0,
            grid=(xn.shape[0] // (t * ROW_TILE),),
            in_specs=[pl.BlockSpec((None, 1, t * TOP_K), lambda i: (i, 0, 0), memory_space=pltpu.SMEM),
                      pl.BlockSpec((t * ROW_TILE, w), lambda i: (i, 0))],
            out_specs=pl.BlockSpec(memory_space=pl.ANY),
            scratch_shapes=[pltpu.SemaphoreType.DMA(())],
        ),
        compiler_params=pltpu.CompilerParams(dimension_semantics=("arbitrary",)),
        name="dispatch",
    )(dst, xn)


def _experts_kernel(blk_e_ref, blk_rows_ref, xs_ref, wg_ref, wu_ref, wd_ref, y_ref, wgb_ref, wub_ref, wdb_ref):
    i = pl.program_id(0)
    r = EXPERT_ROWS
    n_valid = blk_rows_ref[i]

    @pl.when((i == 0) | (blk_e_ref[i] != blk_e_ref[jnp.maximum(i - 1, 0)]))
    def _():
        wgb_ref[...] = wg_ref[...].astype(jnp.bfloat16)
        wub_ref[...] = wu_ref[...].astype(jnp.bfloat16)
        wdb_ref[...] = wd_ref[...].astype(jnp.bfloat16)

    @pl.when(n_valid > 0)
    def _():
        keep = lax.broadcasted_iota(jnp.int32, (r, LANES), 0) < n_valid
        lo, hi = _load_row_tiles(xs_ref, r)
        xb = jnp.concatenate([jnp.where(keep, c, 0.0).astype(jnp.bfloat16) for c in lo + hi], axis=-1)
        g = _dot(xb, wgb_ref[...])
        u = _dot(xb, wub_ref[...])
        g_sig, _ = _sigmoid_pair(g)
        hid = ((g * g_sig) * u).astype(jnp.bfloat16)
        _store_row_tiles(y_ref, _pack_rows(_dot(hid, wdb_ref[...])))

    @pl.when(n_valid == 0)
    def _():
        y_ref[...] = jnp.zeros_like(y_ref)


def _experts(blk_e, blk_rows, xs, wg, wu, wd):
    w = xs.shape[1]
    d, ff = wg.shape[1], wg.shape[2]
    r = EXPERT_ROWS
    rows = pl.BlockSpec((r * ROW_TILE, w), lambda i, be, br: (i, 0))
    return pl.pallas_call(
        _experts_kernel,
        out_shape=jax.ShapeDtypeStruct(xs.shape, xs.dtype),
        grid_spec=pltpu.PrefetchScalarGridSpec(
            num_scalar_prefetch=2,
            grid=(xs.shape[0] // (r * ROW_TILE),),
            in_specs=[
                rows,
                pl.BlockSpec((None, d, ff), lambda i, be, br: (be[i], 0, 0)),
                pl.BlockSpec((None, d, ff), lambda i, be, br: (be[i], 0, 0)),
                pl.BlockSpec((None, ff, d), lambda i, be, br: (be[i], 0, 0)),
            ],
            out_specs=rows,
            scratch_shapes=[pltpu.VMEM((d, ff), jnp.bfloat16), pltpu.VMEM((d, ff), jnp.bfloat16),
                            pltpu.VMEM((ff, d), jnp.bfloat16)],
        ),
        compiler_params=pltpu.CompilerParams(
            dimension_semantics=("arbitrary",), vmem_limit_bytes=VMEM_LIMIT),
        name="experts",
    )(blk_e, blk_rows, xs, wg, wu, wd)


def _combine_kernel(src_ref, src_nx_ref, base_ref, gates_ref, gfin_ref, y_hbm, out_ref, buf_ref, sem):
    i = pl.program_id(0)
    n_steps = pl.num_programs(0)
    n_tok = base_ref.shape[0]
    slot_rows = n_tok * ROW_TILE

    def issue(src_r, half):
        def body(t, carry):
            for s in range(TOP_K):
                pltpu.make_async_copy(_slab(y_hbm, src_r[0, t * TOP_K + s]),
                                      _slab(buf_ref.at[half], s * slot_rows + t * ROW_TILE), sem.at[half]).start()
            return carry
        lax.fori_loop(0, n_tok, body, 0)

    half = i % 2

    @pl.when(i == 0)
    def _():
        issue(src_ref, 0)

    @pl.when(i + 1 < n_steps)
    def _():
        issue(src_nx_ref, 1 - half)

    pltpu.make_async_copy(y_hbm.at[pl.ds(0, TOP_K * slot_rows)], buf_ref.at[half], sem.at[half]).wait()

    gates = gates_ref[...]
    width = ROW_TILE * LANES
    lo_chunks, hi_chunks = [], []
    for c in range(ROW_TILE):
        acc_lo = base_ref[:, c * LANES:(c + 1) * LANES]
        acc_hi = base_ref[:, width + c * LANES:width + (c + 1) * LANES]
        for s in range(TOP_K):
            lo, hi = _unpack_words(buf_ref[half, pl.ds(s * slot_rows + c, n_tok, stride=ROW_TILE), :])
            acc_lo = acc_lo + gates[:, s:s + 1] * lo
            acc_hi = acc_hi + gates[:, s:s + 1] * hi
        lo_chunks.append(acc_lo)
        hi_chunks.append(acc_hi)
    out_ref[...] = _rms(jnp.concatenate(lo_chunks + hi_chunks, axis=-1), gfin_ref[...])


def _combine(src, base, gates, gfin, y):
    n, d = base.shape
    w = y.shape[1]
    t = COMBINE_ROWS
    n_steps = n // t
    cur = pl.BlockSpec((None, 1, t * TOP_K), lambda i: (i, 0, 0), memory_space=pltpu.SMEM)
    nxt = pl.BlockSpec((None, 1, t * TOP_K), lambda i: (jnp.minimum(i + 1, n_steps - 1), 0, 0),
                       memory_space=pltpu.SMEM)
    return pl.pallas_call(
        _combine_kernel,
        out_shape=jax.ShapeDtypeStruct((n, d), jnp.float32),
        grid_spec=pltpu.PrefetchScalarGridSpec(
            num_scalar_prefetch=0,
            grid=(n_steps,),
            in_specs=[cur, nxt,
                      pl.BlockSpec((t, d), lambda i: (i, 0)),
                      pl.BlockSpec((t, TOP_K), lambda i: (i, 0)),
                      pl.BlockSpec((1, d), lambda i: (0, 0)),
                      pl.BlockSpec(memory_space=pl.ANY)],
            out_specs=pl.BlockSpec((t, d), lambda i: (i, 0)),
            scratch_shapes=[pltpu.VMEM((2, TOP_K * t * ROW_TILE, w), y.dtype),
                            pltpu.SemaphoreType.DMA((2,))],
        ),
        compiler_params=pltpu.CompilerParams(
            dimension_semantics=("arbitrary",), vmem_limit_bytes=VMEM_LIMIT),
        name="combine",
    )(src, src, base, gates, gfin, y)


def kernel(x, meta_tokens, norm_mix_g, w_in, lb_table, hgrn_norm_g, conv_w, conv_norm_g, w_out,
           norm_ffn_g, w_router, router_bias, w_gate_e, w_up_e, w_down_e, w_gate_s, w_up_s, w_down_s,
           norm_final_g):
    bsz, seq, d = x.shape
    n = bsz * seq
    bf = jnp.bfloat16
    assert seq % MIX_ROWS == 0 and MIX_ROWS % CHUNK == 0
    assert n % ROUTE_ROWS == 0 and n % DISPATCH_ROWS == 0 and n % COMBINE_ROWS == 0
    assert (n * TOP_K) % EXPERT_ROWS == 0

    meta_pad = jnp.zeros((CHUNK, d), jnp.float32).at[CHUNK - N_META:].set(meta_tokens)
    h1 = _mixer(x, meta_pad, norm_mix_g[0:1], w_in[0].astype(bf), lb_table, hgrn_norm_g[0:1],
                conv_w[0], conv_norm_g[0:1], w_out[0].astype(bf))

    base, xn, idx, gate, rank, counts = _router(
        h1.reshape(n, d), norm_ffn_g[0:1], w_router[0].T, router_bias[0][:, None],
        w_gate_s[0].astype(bf), w_up_s[0].astype(bf), w_down_s[0].astype(bf))

    r = EXPERT_ROWS
    n_blocks = (n * TOP_K) // r + N_EXPERTS
    counts = counts[:, 0]
    padded = (counts + r - 1) // r * r
    ends = jnp.cumsum(padded)
    starts = (ends - padded).astype(jnp.int32)
    blk_row0 = jnp.arange(n_blocks, dtype=jnp.int32) * r
    blk_e = jnp.minimum(jnp.sum((ends[None, :] <= blk_row0[:, None]).astype(jnp.int32), axis=1), N_EXPERTS - 1)
    onehot = (blk_e[:, None] == jnp.arange(N_EXPERTS, dtype=jnp.int32)[None, :]).astype(jnp.int32)
    blk_rows = jnp.clip(onehot @ counts - (blk_row0 - onehot @ starts), 0, r).astype(jnp.int32)

    dest = rank + jnp.sum(jnp.where(idx[None] == jnp.arange(N_EXPERTS, dtype=jnp.int32)[:, None, None],
                                    starts[:, None, None], 0), axis=0)
    slab = (dest * ROW_TILE).T
    xs = _dispatch(slab.reshape(n // DISPATCH_ROWS, 1, DISPATCH_ROWS * TOP_K), xn, n_blocks * r)
    y = _experts(blk_e, blk_rows, xs, w_gate_e[0], w_up_e[0], w_down_e[0])
    out = _combine(slab.reshape(n // COMBINE_ROWS, 1, COMBINE_ROWS * TOP_K), base, gate.T,
                   norm_final_g[None, :], y)
    return out.reshape(bsz, seq, d)
```

```python
import functools

import numpy as np
import jax
import jax.numpy as jnp
from jax import lax
from jax.experimental import pallas as pl
from jax.experimental.pallas import tpu as pltpu

N_META = 16
CHUNK = 128
HEADS = 4
HEAD_DIM = 128
HGRN_W = HEADS * HEAD_DIM
CONV_W = 512
CONV_GROUPS = 4
CONV_K = 3
N_EXPERTS = 64
TOP_K = 8
ROUTED_SCALE = 2.5
EPS = 1e-6

V7X_VMEM_BYTES = 64 * 1024 * 1024
VMEM_LIMIT = V7X_VMEM_BYTES - 8 * 1024 * 1024

MIX_ROWS = 512
ROUTE_ROWS = 512
DISPATCH_ROWS = 512
EXPERT_ROWS = 512
COMBINE_ROWS = 256
DMA_QUEUES = 2

LANES = 128
ROW_TILE = 4

HALF_SPANS = (64, 32, 16, 8, 4, 2, 1)
N_LEVELS = len(HALF_SPANS) + 1


def _decay_sum_matrix():
    a = np.zeros((N_LEVELS, CHUNK, CHUNK), np.float32)
    a[0] = np.tril(np.ones((CHUNK, CHUNK), np.float32))
    for i, m in enumerate(HALF_SPANS):
        for t in range(CHUNK):
            mid = (t // (2 * m)) * 2 * m + m
            if t >= mid:
                a[1 + i, t, mid:t + 1] = 1.0
            else:
                a[1 + i, t, t + 1:mid] = 1.0
    return a.reshape(N_LEVELS * CHUNK, CHUNK)


def _level_matrix():
    lv = np.full((CHUNK, CHUNK), -1, np.int32)
    for t in range(CHUNK):
        lv[t, t] = len(HALF_SPANS)
        for s in range(t):
            top = (t ^ s).bit_length() - 1
            lv[t, s] = HALF_SPANS.index(1 << top)
    return lv


def _rms(x, g):
    return x * lax.rsqrt(jnp.mean(x * x, axis=-1, keepdims=True) + EPS) * g


def _group_rms(x, g, width):
    outs = []
    for j in range(x.shape[-1] // width):
        xs = x[:, j * width:(j + 1) * width]
        outs.append(xs * lax.rsqrt(jnp.mean(xs * xs, axis=-1, keepdims=True) + EPS))
    return jnp.concatenate(outs, axis=-1) * g


def _sigmoid_pair(z):
    t = jnp.exp(-jnp.abs(z))
    inv = 1.0 / (1.0 + t)
    big, small = inv, t * inv
    pos = z >= 0
    return jnp.where(pos, big, small), jnp.where(pos, small, big)


def _pack_rows(x):
    half = x.shape[1] // 2
    bits = lambda v: lax.bitcast_convert_type(v.astype(jnp.bfloat16).astype(jnp.float32), jnp.uint32)
    return (bits(x[:, :half]) >> 16) | (bits(x[:, half:]) & jnp.uint32(0xFFFF0000))


def _unpack_words(w):
    lo = lax.bitcast_convert_type(w << 16, jnp.float32)
    hi = lax.bitcast_convert_type(w & jnp.uint32(0xFFFF0000), jnp.float32)
    return lo, hi


def _store_row_tiles(ref, words):
    t = words.shape[0]
    for c in range(ROW_TILE):
        ref[pl.ds(c, t, stride=ROW_TILE), :] = words[:, c * LANES:(c + 1) * LANES]


def _load_row_tiles(ref, t):
    parts = [_unpack_words(ref[pl.ds(c, t, stride=ROW_TILE), :]) for c in range(ROW_TILE)]
    return [p[0] for p in parts], [p[1] for p in parts]


def _dot(a, b):
    return jnp.dot(a, b, preferred_element_type=jnp.float32)


def _dot_nt(a, b):
    return lax.dot_general(a, b, (((1,), (1,)), ((), ())), preferred_element_type=jnp.float32)


def _dot_tn(a, b):
    return lax.dot_general(a, b, (((0,), (0,)), ((), ())), preferred_element_type=jnp.float32)


def _hgrn_chunk(q, z, iv, lb, amat, level, st_ref, first_valid_row):
    sig, sig_neg = _sigmoid_pair(z)
    lf = jnp.log(lb + (1.0 - lb) * sig)
    k = (1.0 - lb) * sig_neg
    row = lax.broadcasted_iota(jnp.int32, (CHUNK, HGRN_W), 0)
    if first_valid_row:
        valid = row >= first_valid_row
        lf = jnp.where(valid, lf, 0.0)
        k = jnp.where(valid, k, 0.0)

    h1 = lf.astype(jnp.bfloat16)
    r1 = lf - h1.astype(jnp.float32)
    h2 = r1.astype(jnp.bfloat16)
    h3 = (r1 - h2.astype(jnp.float32)).astype(jnp.bfloat16)
    e_all = _dot(amat, h1) + _dot(amat, h2) + _dot(amat, h3)

    b = e_all[0:CHUNK]
    b_last = b[CHUNK - 1:CHUNK]
    q_in = (q * jnp.exp(b)).astype(jnp.bfloat16)
    k_out = (k * jnp.exp(b_last - b)).astype(jnp.bfloat16)
    st_decay = jnp.exp(b_last)
    v_bf = iv.astype(jnp.bfloat16)

    q_lv = [q.astype(jnp.bfloat16)]
    k_lv = [k.astype(jnp.bfloat16)]
    for i, m in enumerate(HALF_SPANS):
        ex = jnp.exp(e_all[(1 + i) * CHUNK:(2 + i) * CHUNK])
        right = (row & m) != 0
        q_lv.append(jnp.where(right, q * ex, 0.0).astype(jnp.bfloat16))
        k_lv.append(jnp.where(right, 0.0, k * ex).astype(jnp.bfloat16))
    lv_of = [len(HALF_SPANS)] + list(range(len(HALF_SPANS)))

    outs = []
    for h in range(HEADS):
        cols = slice(h * HEAD_DIM, (h + 1) * HEAD_DIM)
        scores = jnp.zeros((CHUNK, CHUNK), jnp.float32)
        for ql, kl, lv in zip(q_lv, k_lv, lv_of):
            scores = jnp.where(level == lv, _dot_nt(ql[:, cols], kl[:, cols]), scores)
        st = st_ref[h]
        o = _dot(scores.astype(jnp.bfloat16), v_bf[:, cols]) + _dot_nt(q_in[:, cols], st.astype(jnp.bfloat16))
        st_ref[h] = st * st_decay[:, cols] + _dot_tn(v_bf[:, cols], k_out[:, cols])
        outs.append(o)
    return jnp.concatenate(outs, axis=-1)


def _mixer_kernel(x_ref, meta_ref, gmix_ref, win_ref, lbt_ref, ghg_ref, cw_ref, gcv_ref, wout_ref,
                  amat_ref, level_ref, h1_ref, proj_ref, o_ref, u_ref, st_ref):
    j = pl.program_id(1)
    rows = x_ref.shape[0]
    n_in = win_ref.shape[1]

    lbt = lbt_ref[...]
    lbe = jnp.exp(lbt - jnp.max(lbt, axis=0, keepdims=True))
    lb = lbe[0:1] / jnp.sum(lbe, axis=0, keepdims=True)

    amat = amat_ref[...]
    level = level_ref[...]
    gmix = gmix_ref[...]

    def project(xv, dst_rows):
        xn = _rms(xv, gmix).astype(jnp.bfloat16)
        for c0 in range(0, n_in, 512):
            proj_ref[dst_rows, c0:c0 + 512] = _dot(xn, win_ref[:, c0:c0 + 512])

    @pl.when(j == 0)
    def _():
        st_ref[...] = jnp.zeros_like(st_ref)
        project(meta_ref[...], pl.ds(0, CHUNK))
        pm = proj_ref[0:CHUNK, :]
        _hgrn_chunk(pm[:, 0:512], pm[:, 512:1024], pm[:, 1024:1536], lb, amat, level, st_ref,
                    CHUNK - N_META)
        u_ref[0:8, :] = (pm[:, 2560:3072] * pm[:, 3072:3584])[CHUNK - 8:CHUNK]

    project(x_ref[...], pl.ds(0, rows))

    def chunk_body(c, carry):
        r0 = pl.multiple_of(c * CHUNK, CHUNK)
        q = proj_ref[pl.ds(r0, CHUNK), 0:512]
        z = proj_ref[pl.ds(r0, CHUNK), 512:1024]
        iv = proj_ref[pl.ds(r0, CHUNK), 1024:1536]
        o_ref[pl.ds(r0, CHUNK), :] = _hgrn_chunk(q, z, iv, lb, amat, level, st_ref, 0)
        return carry

    lax.fori_loop(0, rows // CHUNK, chunk_body, 0)

    g_out = proj_ref[:, 1536:2048]
    g_sig, _ = _sigmoid_pair(g_out)
    y_hgrn = _group_rms(o_ref[...], ghg_ref[...], HEAD_DIM) * (g_out * g_sig)

    u = proj_ref[:, 2560:3072] * proj_ref[:, 3072:3584]
    u_ref[8:8 + rows, :] = u
    cw = cw_ref[...]
    y = cw[2:3] * u + cw[1:2] * u_ref[7:7 + rows, :] + cw[0:1] * u_ref[6:6 + rows, :]
    u_ref[0:8, :] = u[rows - 8:rows]
    y_conv = _group_rms(proj_ref[:, 2048:2560] * y, gcv_ref[...], CONV_W // CONV_GROUPS)

    mixed = jnp.concatenate([y_hgrn, y_conv], axis=-1).astype(jnp.bfloat16)
    h1_ref[...] = x_ref[...] + _dot(mixed, wout_ref[...])


def _mixer(x, meta_pad, gmix, w_in, lb_table, ghg, conv_w, gcv, w_out):
    bsz, seq, d = x.shape
    n_in = w_in.shape[1]
    rows = MIX_ROWS
    const = lambda *shape: pl.BlockSpec(shape, lambda b, j: (0,) * len(shape))
    return pl.pallas_call(
        _mixer_kernel,
        out_shape=jax.ShapeDtypeStruct((bsz, seq, d), jnp.float32),
        grid=(bsz, seq // rows),
        in_specs=[
            pl.BlockSpec((None, rows, d), lambda b, j: (b, j, 0)),
            const(CHUNK, d), const(1, d), const(d, n_in), const(*lb_table.shape), const(1, HGRN_W),
            const(CONV_K, CONV_W), const(1, CONV_W), const(d, d),
            const(N_LEVELS * CHUNK, CHUNK), const(CHUNK, CHUNK),
        ],
        out_specs=pl.BlockSpec((None, rows, d), lambda b, j: (b, j, 0)),
        scratch_shapes=[
            pltpu.VMEM((rows, n_in), jnp.float32),
            pltpu.VMEM((rows, HGRN_W), jnp.float32),
            pltpu.VMEM((rows + 8, CONV_W), jnp.float32),
            pltpu.VMEM((HEADS, HEAD_DIM, HEAD_DIM), jnp.float32),
        ],
        compiler_params=pltpu.CompilerParams(
            dimension_semantics=("arbitrary", "arbitrary"), vmem_limit_bytes=VMEM_LIMIT),
        name="mixer",
    )(x, meta_pad, gmix, w_in, lb_table, ghg, conv_w, gcv, w_out,
      jnp.asarray(_decay_sum_matrix(), jnp.bfloat16), jnp.asarray(_level_matrix()))


def _router_kernel(h1_ref, gffn_ref, wrt_ref, bias_ref, wgs_ref, wus_ref, wds_ref, tri_ref,
                   base_ref, xn_ref, idx_ref, gate_ref, rank_ref, counts_ref, carry_ref):
    i = pl.program_id(0)
    n_tok = h1_ref.shape[0]

    @pl.when(i == 0)
    def _():
        carry_ref[...] = jnp.zeros_like(carry_ref)

    h1 = h1_ref[...]
    xn = _rms(h1, gffn_ref[...])
    _store_row_tiles(xn_ref, _pack_rows(xn))
    xb = xn.astype(jnp.bfloat16)

    g_pre = _dot(xb, wgs_ref[...])
    gate_s, _ = _sigmoid_pair(g_pre)
    hid = (g_pre * gate_s) * _dot(xb, wus_ref[...])
    base_ref[...] = h1 + _dot(hid.astype(jnp.bfloat16), wds_ref[...])

    logits = lax.dot_general(wrt_ref[...], xn, (((1,), (1,)), ((), ())),
                             precision=lax.Precision.HIGHEST, preferred_element_type=jnp.float32)
    scores, _ = _sigmoid_pair(logits)
    sel = scores + bias_ref[...]
    eid = lax.broadcasted_iota(jnp.int32, (N_EXPERTS, n_tok), 0).astype(jnp.float32)
    picks, top_s = [], []
    for _ in range(TOP_K):
        best = jnp.max(sel, axis=0, keepdims=True)
        pick = jnp.min(jnp.where(sel == best, eid, float(N_EXPERTS)), axis=0, keepdims=True)
        hit = eid == pick
        top_s.append(jnp.sum(jnp.where(hit, scores, 0.0), axis=0, keepdims=True))
        sel = jnp.where(hit, -jnp.inf, sel)
        picks.append(pick)
    top_s = jnp.concatenate(top_s, axis=0)
    gate_ref[...] = top_s / jnp.sum(top_s, axis=0, keepdims=True) * ROUTED_SCALE
    idx_ref[...] = jnp.concatenate(picks, axis=0).astype(jnp.int32)

    chosen = jnp.zeros((N_EXPERTS, n_tok), jnp.float32)
    for pick in picks:
        chosen = chosen + jnp.where(eid == pick, 1.0, 0.0)
    incl = _dot(chosen.astype(jnp.bfloat16), tri_ref[...])
    before = carry_ref[...] + incl - 1.0
    ranks = [jnp.sum(jnp.where(eid == pick, before, 0.0), axis=0, keepdims=True) for pick in picks]
    rank_ref[...] = jnp.concatenate(ranks, axis=0).astype(jnp.int32)
    carry_ref[...] = carry_ref[...] + incl[:, n_tok - 1:n_tok]
    counts_ref[...] = jnp.broadcast_to(carry_ref[...], counts_ref.shape).astype(jnp.int32)


def _router(h1, gffn, w_router_t, bias, wgs, wus, wds):
    n, d = h1.shape
    t = ROUTE_ROWS
    ff = wgs.shape[1]
    tri = jnp.asarray(np.triu(np.ones((t, t), np.float32)), jnp.bfloat16)
    const = lambda *shape: pl.BlockSpec(shape, lambda i: (0,) * len(shape))
    tok = lambda width: pl.BlockSpec((t, width), lambda i: (i, 0))
    slot = pl.BlockSpec((TOP_K, t), lambda i: (0, i))
    return pl.pallas_call(
        _router_kernel,
        out_shape=(
            jax.ShapeDtypeStruct((n, d), jnp.float32),
            jax.ShapeDtypeStruct((n * ROW_TILE, LANES), jnp.uint32),
            jax.ShapeDtypeStruct((TOP_K, n), jnp.int32),
            jax.ShapeDtypeStruct((TOP_K, n), jnp.float32),
            jax.ShapeDtypeStruct((TOP_K, n), jnp.int32),
            jax.ShapeDtypeStruct((N_EXPERTS, 128), jnp.int32),
        ),
        grid=(n // t,),
        in_specs=[tok(d), const(1, d), const(N_EXPERTS, d), const(N_EXPERTS, 1),
                  const(d, ff), const(d, ff), const(ff, d), const(t, t)],
        out_specs=(tok(d), pl.BlockSpec((t * ROW_TILE, LANES), lambda i: (i, 0)), slot, slot, slot,
                   const(N_EXPERTS, 128)),
        scratch_shapes=[pltpu.VMEM((N_EXPERTS, 1), jnp.float32)],
        compiler_params=pltpu.CompilerParams(
            dimension_semantics=("arbitrary",), vmem_limit_bytes=VMEM_LIMIT),
        name="router",
    )(h1, gffn, w_router_t, bias, wgs, wus, wds, tri)


def _slab(ref, first_sublane):
    return ref.at[pl.ds(pl.multiple_of(first_sublane, ROW_TILE), ROW_TILE)]


def _dispatch_kernel(dst_ref, xn_ref, xs_hbm, sem):
    n_tok = xn_ref.shape[0] // ROW_TILE

    def issue(t, carry):
        for s in range(TOP_K):
            pltpu.make_async_copy(_slab(xn_ref, t * ROW_TILE), _slab(xs_hbm, dst_ref[0, t * TOP_K + s]),
                                  sem).start(priority=s % DMA_QUEUES)
        return carry

    lax.fori_loop(0, n_tok, issue, 0)

    for s in range(TOP_K):
        pltpu.make_async_copy(xn_ref, xs_hbm.at[pl.ds(0, xn_ref.shape[0])], sem).wait()


def _dispatch(dst, xn, n_rows):
    w = xn.shape[1]
    t = DISPATCH_ROWS
    return pl.pallas_call(
        _dispatch_kernel,
        out_shape=jax.ShapeDtypeStruct((n_rows * ROW_TILE, w), xn.dtype),
        grid_spec=pltpu.PrefetchScalarGridSpec(
            num_scalar_prefetch=0,
            grid=(xn.shape[0] // (t * ROW_TILE),),
            in_specs=[pl.BlockSpec((None, 1, t * TOP_K), lambda i: (i, 0, 0), memory_space=pltpu.SMEM),
                      pl.BlockSpec((t * ROW_TILE, w), lambda i: (i, 0))],
            out_specs=pl.BlockSpec(memory_space=pl.ANY),
            scratch_shapes=[pltpu.SemaphoreType.DMA(())],
        ),
        compiler_params=pltpu.CompilerParams(dimension_semantics=("arbitrary",)),
        name="dispatch",
    )(dst, xn)


def _experts_kernel(blk_e_ref, blk_rows_ref, xs_ref, wg_ref, wu_ref, wd_ref, y_ref, wgb_ref, wub_ref, wdb_ref):
    i = pl.program_id(0)
    r = EXPERT_ROWS
    n_valid = blk_rows_ref[i]

    @pl.when((i == 0) | (blk_e_ref[i] != blk_e_ref[jnp.maximum(i - 1, 0)]))
    def _():
        wgb_ref[...] = wg_ref[...].astype(jnp.bfloat16)
        wub_ref[...] = wu_ref[...].astype(jnp.bfloat16)
        wdb_ref[...] = wd_ref[...].astype(jnp.bfloat16)

    @pl.when(n_valid > 0)
    def _():
        keep = lax.broadcasted_iota(jnp.int32, (r, LANES), 0) < n_valid
        lo, hi = _load_row_tiles(xs_ref, r)
        xb = jnp.concatenate([jnp.where(keep, c, 0.0).astype(jnp.bfloat16) for c in lo + hi], axis=-1)
        g = _dot(xb, wgb_ref[...])
        u = _dot(xb, wub_ref[...])
        g_sig, _ = _sigmoid_pair(g)
        hid = ((g * g_sig) * u).astype(jnp.bfloat16)
        _store_row_tiles(y_ref, _pack_rows(_dot(hid, wdb_ref[...])))

    @pl.when(n_valid == 0)
    def _():
        y_ref[...] = jnp.zeros_like(y_ref)


def _experts(blk_e, blk_rows, xs, wg, wu, wd):
    w = xs.shape[1]
    d, ff = wg.shape[1], wg.shape[2]
    r = EXPERT_ROWS
    rows = pl.BlockSpec((r * ROW_TILE, w), lambda i, be, br: (i, 0))
    return pl.pallas_call(
        _experts_kernel,
        out_shape=jax.ShapeDtypeStruct(xs.shape, xs.dtype),
        grid_spec=pltpu.PrefetchScalarGridSpec(
            num_scalar_prefetch=2,
            grid=(xs.shape[0] // (r * ROW_TILE),),
            in_specs=[
                rows,
                pl.BlockSpec((None, d, ff), lambda i, be, br: (be[i], 0, 0)),
                pl.BlockSpec((None, d, ff), lambda i, be, br: (be[i], 0, 0)),
                pl.BlockSpec((None, ff, d), lambda i, be, br: (be[i], 0, 0)),
            ],
            out_specs=rows,
            scratch_shapes=[pltpu.VMEM((d, ff), jnp.bfloat16), pltpu.VMEM((d, ff), jnp.bfloat16),
                            pltpu.VMEM((ff, d), jnp.bfloat16)],
        ),
        compiler_params=pltpu.CompilerParams(
            dimension_semantics=("arbitrary",), vmem_limit_bytes=VMEM_LIMIT),
        name="experts",
    )(blk_e, blk_rows, xs, wg, wu, wd)


def _combine_kernel(src_ref, src_nx_ref, base_ref, gates_ref, gfin_ref, y_hbm, out_ref, buf_ref, sem):
    i = pl.program_id(0)
    n_steps = pl.num_programs(0)
    n_tok = base_ref.shape[0]
    slot_rows = n_tok * ROW_TILE

    def issue(src_r, half):
        def body(t, carry):
            for s in range(TOP_K):
                pltpu.make_async_copy(_slab(y_hbm, src_r[0, t * TOP_K + s]),
                                      _slab(buf_ref.at[half], s * slot_rows + t * ROW_TILE),
                                      sem.at[half]).start(priority=s % DMA_QUEUES)
            return carry
        lax.fori_loop(0, n_tok, body, 0)

    half = i % 2

    @pl.when(i == 0)
    def _():
        issue(src_ref, 0)

    @pl.when(i + 1 < n_steps)
    def _():
        issue(src_nx_ref, 1 - half)

    pltpu.make_async_copy(y_hbm.at[pl.ds(0, TOP_K * slot_rows)], buf_ref.at[half], sem.at[half]).wait()

    gates = gates_ref[...]
    width = ROW_TILE * LANES
    lo_chunks, hi_chunks = [], []
    for c in range(ROW_TILE):
        acc_lo = base_ref[:, c * LANES:(c + 1) * LANES]
        acc_hi = base_ref[:, width + c * LANES:width + (c + 1) * LANES]
        for s in range(TOP_K):
            lo, hi = _unpack_words(buf_ref[half, pl.ds(s * slot_rows + c, n_tok, stride=ROW_TILE), :])
            acc_lo = acc_lo + gates[:, s:s + 1] * lo
            acc_hi = acc_hi + gates[:, s:s + 1] * hi
        lo_chunks.append(acc_lo)
        hi_chunks.append(acc_hi)
    out_ref[...] = _rms(jnp.concatenate(lo_chunks + hi_chunks, axis=-1), gfin_ref[...])


def _combine(src, base, gates, gfin, y):
    n, d = base.shape
    w = y.shape[1]
    t = COMBINE_ROWS
    n_steps = n // t
    cur = pl.BlockSpec((None, 1, t * TOP_K), lambda i: (i, 0, 0), memory_space=pltpu.SMEM)
    nxt = pl.BlockSpec((None, 1, t * TOP_K), lambda i: (jnp.minimum(i + 1, n_steps - 1), 0, 0),
                       memory_space=pltpu.SMEM)
    return pl.pallas_call(
        _combine_kernel,
        out_shape=jax.ShapeDtypeStruct((n, d), jnp.float32),
        grid_spec=pltpu.PrefetchScalarGridSpec(
            num_scalar_prefetch=0,
            grid=(n_steps,),
            in_specs=[cur, nxt,
                      pl.BlockSpec((t, d), lambda i: (i, 0)),
                      pl.BlockSpec((t, TOP_K), lambda i: (i, 0)),
                      pl.BlockSpec((1, d), lambda i: (0, 0)),
                      pl.BlockSpec(memory_space=pl.ANY)],
            out_specs=pl.BlockSpec((t, d), lambda i: (i, 0)),
            scratch_shapes=[pltpu.VMEM((2, TOP_K * t * ROW_TILE, w), y.dtype),
                            pltpu.SemaphoreType.DMA((2,))],
        ),
        compiler_params=pltpu.CompilerParams(
            dimension_semantics=("arbitrary",), vmem_limit_bytes=VMEM_LIMIT),
        name="combine",
    )(src, src, base, gates, gfin, y)


def kernel(x, meta_tokens, norm_mix_g, w_in, lb_table, hgrn_norm_g, conv_w, conv_norm_g, w_out,
           norm_ffn_g, w_router, router_bias, w_gate_e, w_up_e, w_down_e, w_gate_s, w_up_s, w_down_s,
           norm_final_g):
    bsz, seq, d = x.shape
    n = bsz * seq
    bf = jnp.bfloat16
    assert seq % MIX_ROWS == 0 and MIX_ROWS % CHUNK == 0
    assert n % ROUTE_ROWS == 0 and n % DISPATCH_ROWS == 0 and n % COMBINE_ROWS == 0
    assert (n * TOP_K) % EXPERT_ROWS == 0

    meta_pad = jnp.zeros((CHUNK, d), jnp.float32).at[CHUNK - N_META:].set(meta_tokens)
    h1 = _mixer(x, meta_pad, norm_mix_g[0:1], w_in[0].astype(bf), lb_table, hgrn_norm_g[0:1],
                conv_w[0], conv_norm_g[0:1], w_out[0].astype(bf))

    base, xn, idx, gate, rank, counts = _router(
        h1.reshape(n, d), norm_ffn_g[0:1], w_router[0].T, router_bias[0][:, None],
        w_gate_s[0].astype(bf), w_up_s[0].astype(bf), w_down_s[0].astype(bf))

    r = EXPERT_ROWS
    n_blocks = (n * TOP_K) // r + N_EXPERTS
    counts = counts[:, 0]
    padded = (counts + r - 1) // r * r
    ends = jnp.cumsum(padded)
    starts = (ends - padded).astype(jnp.int32)
    blk_row0 = jnp.arange(n_blocks, dtype=jnp.int32) * r
    blk_e = jnp.minimum(jnp.sum((ends[None, :] <= blk_row0[:, None]).astype(jnp.int32), axis=1), N_EXPERTS - 1)
    onehot = (blk_e[:, None] == jnp.arange(N_EXPERTS, dtype=jnp.int32)[None, :]).astype(jnp.int32)
    blk_rows = jnp.clip(onehot @ counts - (blk_row0 - onehot @ starts), 0, r).astype(jnp.int32)

    dest = rank + jnp.sum(jnp.where(idx[None] == jnp.arange(N_EXPERTS, dtype=jnp.int32)[:, None, None],
                                    starts[:, None, None], 0), axis=0)
    slab = (dest * ROW_TILE).T
    xs = _dispatch(slab.reshape(n // DISPATCH_ROWS, 1, DISPATCH_ROWS * TOP_K), xn, n_blocks * r)
    y = _experts(blk_e, blk_rows, xs, w_gate_e[0], w_up_e[0], w_down_e[0])
    out = _combine(slab.reshape(n // COMBINE_ROWS, 1, COMBINE_ROWS * TOP_K), base, gate.T,
                   norm_final_g[None, :], y)
    return out.reshape(bsz, seq, d)
```

```python
import functools

import numpy as np
import jax
import jax.numpy as jnp
from jax import lax
from jax.experimental import pallas as pl
from jax.experimental.pallas import tpu as pltpu

N_META = 16
CHUNK = 128
HEADS = 4
HEAD_DIM = 128
HGRN_W = HEADS * HEAD_DIM
CONV_W = 512
CONV_GROUPS = 4
CONV_K = 3
N_EXPERTS = 64
TOP_K = 8
ROUTED_SCALE = 2.5
EPS = 1e-6

V7X_VMEM_BYTES = 64 * 1024 * 1024
VMEM_LIMIT = V7X_VMEM_BYTES - 8 * 1024 * 1024

MIX_ROWS = 512
CHUNK_UNROLL = 4
ROUTE_ROWS = 512
DISPATCH_ROWS = 512
EXPERT_ROWS = 512
COMBINE_ROWS = 256
DMA_QUEUES = 2
ISSUE_UNROLL = 4

LANES = 128
ROW_TILE = 4

HALF_SPANS = (64, 32, 16, 8, 4, 2, 1)
N_LEVELS = len(HALF_SPANS) + 1


def _decay_sum_matrix():
    a = np.zeros((N_LEVELS, CHUNK, CHUNK), np.float32)
    a[0] = np.tril(np.ones((CHUNK, CHUNK), np.float32))
    for i, m in enumerate(HALF_SPANS):
        for t in range(CHUNK):
            mid = (t // (2 * m)) * 2 * m + m
            if t >= mid:
                a[1 + i, t, mid:t + 1] = 1.0
            else:
                a[1 + i, t, t + 1:mid] = 1.0
    return a.reshape(N_LEVELS * CHUNK, CHUNK)


def _level_matrix():
    lv = np.full((CHUNK, CHUNK), -1, np.int32)
    for t in range(CHUNK):
        lv[t, t] = len(HALF_SPANS)
        for s in range(t):
            top = (t ^ s).bit_length() - 1
            lv[t, s] = HALF_SPANS.index(1 << top)
    return lv


def _rms(x, g):
    return x * lax.rsqrt(jnp.mean(x * x, axis=-1, keepdims=True) + EPS) * g


def _group_rms(x, g, width):
    outs = []
    for j in range(x.shape[-1] // width):
        xs = x[:, j * width:(j + 1) * width]
        outs.append(xs * lax.rsqrt(jnp.mean(xs * xs, axis=-1, keepdims=True) + EPS))
    return jnp.concatenate(outs, axis=-1) * g


def _sigmoid_pair(z):
    t = jnp.exp(-jnp.abs(z))
    inv = 1.0 / (1.0 + t)
    big, small = inv, t * inv
    pos = z >= 0
    return jnp.where(pos, big, small), jnp.where(pos, small, big)


def _pack_rows(x):
    half = x.shape[1] // 2
    bits = lambda v: lax.bitcast_convert_type(v.astype(jnp.bfloat16).astype(jnp.float32), jnp.uint32)
    return (bits(x[:, :half]) >> 16) | (bits(x[:, half:]) & jnp.uint32(0xFFFF0000))


def _unpack_words(w):
    lo = lax.bitcast_convert_type(w << 16, jnp.float32)
    hi = lax.bitcast_convert_type(w & jnp.uint32(0xFFFF0000), jnp.float32)
    return lo, hi


def _store_row_tiles(ref, words):
    t = words.shape[0]
    for c in range(ROW_TILE):
        ref[pl.ds(c, t, stride=ROW_TILE), :] = words[:, c * LANES:(c + 1) * LANES]


def _load_row_tiles(ref, t):
    parts = [_unpack_words(ref[pl.ds(c, t, stride=ROW_TILE), :]) for c in range(ROW_TILE)]
    return [p[0] for p in parts], [p[1] for p in parts]


def _dot(a, b):
    return jnp.dot(a, b, preferred_element_type=jnp.float32)


def _dot_nt(a, b):
    return lax.dot_general(a, b, (((1,), (1,)), ((), ())), preferred_element_type=jnp.float32)


def _dot_tn(a, b):
    return lax.dot_general(a, b, (((0,), (0,)), ((), ())), preferred_element_type=jnp.float32)


def _hgrn_chunk(q, z, iv, lb, amat, level, st_ref, first_valid_row):
    sig, sig_neg = _sigmoid_pair(z)
    lf = jnp.log(lb + (1.0 - lb) * sig)
    k = (1.0 - lb) * sig_neg
    row = lax.broadcasted_iota(jnp.int32, (CHUNK, HGRN_W), 0)
    if first_valid_row:
        valid = row >= first_valid_row
        lf = jnp.where(valid, lf, 0.0)
        k = jnp.where(valid, k, 0.0)

    h1 = lf.astype(jnp.bfloat16)
    h2 = (lf - h1.astype(jnp.float32)).astype(jnp.bfloat16)
    e_all = _dot(amat, jnp.concatenate([h1, h2], axis=0))

    b = e_all[0:CHUNK]
    b_last = b[CHUNK - 1:CHUNK]
    q_in = (q * jnp.exp(b)).astype(jnp.bfloat16)
    k_out = (k * jnp.exp(b_last - b)).astype(jnp.bfloat16)
    st_decay = jnp.exp(b_last)
    v_bf = iv.astype(jnp.bfloat16)

    q_lv = [q.astype(jnp.bfloat16)]
    k_lv = [k.astype(jnp.bfloat16)]
    for i, m in enumerate(HALF_SPANS):
        ex = jnp.exp(e_all[(1 + i) * CHUNK:(2 + i) * CHUNK])
        right = (row & m) != 0
        q_lv.append(jnp.where(right, q * ex, 0.0).astype(jnp.bfloat16))
        k_lv.append(jnp.where(right, 0.0, k * ex).astype(jnp.bfloat16))
    lv_of = [len(HALF_SPANS)] + list(range(len(HALF_SPANS)))

    outs = []
    for h in range(HEADS):
        cols = slice(h * HEAD_DIM, (h + 1) * HEAD_DIM)
        scores = jnp.zeros((CHUNK, CHUNK), jnp.float32)
        for ql, kl, lv in zip(q_lv, k_lv, lv_of):
            scores = jnp.where(level == lv, _dot_nt(ql[:, cols], kl[:, cols]), scores)
        st = st_ref[h]
        o = _dot(scores.astype(jnp.bfloat16), v_bf[:, cols]) + _dot_nt(q_in[:, cols], st.astype(jnp.bfloat16))
        st_ref[h] = st * st_decay[:, cols] + _dot_tn(v_bf[:, cols], k_out[:, cols])
        outs.append(o)
    return jnp.concatenate(outs, axis=-1)


def _mixer_kernel(x_ref, meta_ref, gmix_ref, win_ref, lbt_ref, ghg_ref, cw_ref, gcv_ref, wout_ref,
                  amat_ref, level_ref, h1_ref, proj_ref, o_ref, u_ref, st_ref):
    j = pl.program_id(1)
    rows = x_ref.shape[0]
    n_in = win_ref.shape[1]

    lbt = lbt_ref[...]
    lbe = jnp.exp(lbt - jnp.max(lbt, axis=0, keepdims=True))
    lb = lbe[0:1] / jnp.sum(lbe, axis=0, keepdims=True)

    amat = amat_ref[...]
    level = level_ref[...]
    gmix = gmix_ref[...]

    def project(xv, dst_rows):
        xn = _rms(xv, gmix).astype(jnp.bfloat16)
        for c0 in range(0, n_in, 512):
            proj_ref[dst_rows, c0:c0 + 512] = _dot(xn, win_ref[:, c0:c0 + 512])

    @pl.when(j == 0)
    def _():
        st_ref[...] = jnp.zeros_like(st_ref)
        project(meta_ref[...], pl.ds(0, CHUNK))
        pm = proj_ref[0:CHUNK, :]
        _hgrn_chunk(pm[:, 0:512], pm[:, 512:1024], pm[:, 1024:1536], lb, amat, level, st_ref,
                    CHUNK - N_META)
        u_ref[0:8, :] = (pm[:, 2560:3072] * pm[:, 3072:3584])[CHUNK - 8:CHUNK]

    project(x_ref[...], pl.ds(0, rows))

    def chunk_body(c, carry):
        for u in range(CHUNK_UNROLL):
            r0 = pl.multiple_of((c * CHUNK_UNROLL + u) * CHUNK, CHUNK)
            q = proj_ref[pl.ds(r0, CHUNK), 0:512]
            z = proj_ref[pl.ds(r0, CHUNK), 512:1024]
            iv = proj_ref[pl.ds(r0, CHUNK), 1024:1536]
            o_ref[pl.ds(r0, CHUNK), :] = _hgrn_chunk(q, z, iv, lb, amat, level, st_ref, 0)
        return carry

    lax.fori_loop(0, rows // (CHUNK * CHUNK_UNROLL), chunk_body, 0)

    g_out = proj_ref[:, 1536:2048]
    g_sig, _ = _sigmoid_pair(g_out)
    y_hgrn = _group_rms(o_ref[...], ghg_ref[...], HEAD_DIM) * (g_out * g_sig)

    u = proj_ref[:, 2560:3072] * proj_ref[:, 3072:3584]
    u_ref[8:8 + rows, :] = u
    cw = cw_ref[...]
    y = cw[2:3] * u + cw[1:2] * u_ref[7:7 + rows, :] + cw[0:1] * u_ref[6:6 + rows, :]
    u_ref[0:8, :] = u[rows - 8:rows]
    y_conv = _group_rms(proj_ref[:, 2048:2560] * y, gcv_ref[...], CONV_W // CONV_GROUPS)

    mixed = jnp.concatenate([y_hgrn, y_conv], axis=-1).astype(jnp.bfloat16)
    h1_ref[...] = x_ref[...] + _dot(mixed, wout_ref[...])


def _mixer(x, meta_pad, gmix, w_in, lb_table, ghg, conv_w, gcv, w_out):
    bsz, seq, d = x.shape
    n_in = w_in.shape[1]
    rows = MIX_ROWS
    const = lambda *shape: pl.BlockSpec(shape, lambda b, j: (0,) * len(shape))
    return pl.pallas_call(
        _mixer_kernel,
        out_shape=jax.ShapeDtypeStruct((bsz, seq, d), jnp.float32),
        grid=(bsz, seq // rows),
        in_specs=[
            pl.BlockSpec((None, rows, d), lambda b, j: (b, j, 0)),
            const(CHUNK, d), const(1, d), const(d, n_in), const(*lb_table.shape), const(1, HGRN_W),
            const(CONV_K, CONV_W), const(1, CONV_W), const(d, d),
            const(N_LEVELS * CHUNK, 2 * CHUNK), const(CHUNK, CHUNK),
        ],
        out_specs=pl.BlockSpec((None, rows, d), lambda b, j: (b, j, 0)),
        scratch_shapes=[
            pltpu.VMEM((rows, n_in), jnp.float32),
            pltpu.VMEM((rows, HGRN_W), jnp.float32),
            pltpu.VMEM((rows + 8, CONV_W), jnp.float32),
            pltpu.VMEM((HEADS, HEAD_DIM, HEAD_DIM), jnp.float32),
        ],
        compiler_params=pltpu.CompilerParams(
            dimension_semantics=("arbitrary", "arbitrary"), vmem_limit_bytes=VMEM_LIMIT),
        name="mixer",
    )(x, meta_pad, gmix, w_in, lb_table, ghg, conv_w, gcv, w_out,
      jnp.asarray(np.tile(_decay_sum_matrix(), (1, 2)), jnp.bfloat16), jnp.asarray(_level_matrix()))


def _router_kernel(h1_ref, gffn_ref, wrt_ref, bias_ref, wgs_ref, wus_ref, wds_ref, tri_ref,
                   base_ref, xn_ref, idx_ref, gate_ref, rank_ref, counts_ref, carry_ref):
    i = pl.program_id(0)
    n_tok = h1_ref.shape[0]

    @pl.when(i == 0)
    def _():
        carry_ref[...] = jnp.zeros_like(carry_ref)

    h1 = h1_ref[...]
    xn = _rms(h1, gffn_ref[...])
    _store_row_tiles(xn_ref, _pack_rows(xn))
    xb = xn.astype(jnp.bfloat16)

    g_pre = _dot(xb, wgs_ref[...])
    gate_s, _ = _sigmoid_pair(g_pre)
    hid = (g_pre * gate_s) * _dot(xb, wus_ref[...])
    base_ref[...] = h1 + _dot(hid.astype(jnp.bfloat16), wds_ref[...])

    logits = lax.dot_general(wrt_ref[...], xn, (((1,), (1,)), ((), ())),
                             precision=lax.Precision.HIGHEST, preferred_element_type=jnp.float32)
    scores, _ = _sigmoid_pair(logits)
    sel = scores + bias_ref[...]
    eid = lax.broadcasted_iota(jnp.int32, (N_EXPERTS, n_tok), 0).astype(jnp.float32)
    picks, top_s = [], []
    for _ in range(TOP_K):
        best = jnp.max(sel, axis=0, keepdims=True)
        pick = jnp.min(jnp.where(sel == best, eid, float(N_EXPERTS)), axis=0, keepdims=True)
        hit = eid == pick
        top_s.append(jnp.sum(jnp.where(hit, scores, 0.0), axis=0, keepdims=True))
        sel = jnp.where(hit, -jnp.inf, sel)
        picks.append(pick)
    top_s = jnp.concatenate(top_s, axis=0)
    gate_ref[...] = top_s / jnp.sum(top_s, axis=0, keepdims=True) * ROUTED_SCALE
    idx_ref[...] = jnp.concatenate(picks, axis=0).astype(jnp.int32)

    chosen = jnp.zeros((N_EXPERTS, n_tok), jnp.float32)
    for pick in picks:
        chosen = chosen + jnp.where(eid == pick, 1.0, 0.0)
    incl = _dot(chosen.astype(jnp.bfloat16), tri_ref[...])
    before = carry_ref[...] + incl - 1.0
    ranks = [jnp.sum(jnp.where(eid == pick, before, 0.0), axis=0, keepdims=True) for pick in picks]
    rank_ref[...] = jnp.concatenate(ranks, axis=0).astype(jnp.int32)
    carry_ref[...] = carry_ref[...] + incl[:, n_tok - 1:n_tok]
    counts_ref[...] = jnp.broadcast_to(carry_ref[...], counts_ref.shape).astype(jnp.int32)


def _router(h1, gffn, w_router_t, bias, wgs, wus, wds):
    n, d = h1.shape
    t = ROUTE_ROWS
    ff = wgs.shape[1]
    tri = jnp.asarray(np.triu(np.ones((t, t), np.float32)), jnp.bfloat16)
    const = lambda *shape: pl.BlockSpec(shape, lambda i: (0,) * len(shape))
    tok = lambda width: pl.BlockSpec((t, width), lambda i: (i, 0))
    slot = pl.BlockSpec((TOP_K, t), lambda i: (0, i))
    return pl.pallas_call(
        _router_kernel,
        out_shape=(
            jax.ShapeDtypeStruct((n, d), jnp.float32),
            jax.ShapeDtypeStruct((n * ROW_TILE, LANES), jnp.uint32),
            jax.ShapeDtypeStruct((TOP_K, n), jnp.int32),
            jax.ShapeDtypeStruct((TOP_K, n), jnp.float32),
            jax.ShapeDtypeStruct((TOP_K, n), jnp.int32),
            jax.ShapeDtypeStruct((N_EXPERTS, 128), jnp.int32),
        ),
        grid=(n // t,),
        in_specs=[tok(d), const(1, d), const(N_EXPERTS, d), const(N_EXPERTS, 1),
                  const(d, ff), const(d, ff), const(ff, d), const(t, t)],
        out_specs=(tok(d), pl.BlockSpec((t * ROW_TILE, LANES), lambda i: (i, 0)), slot, slot, slot,
                   const(N_EXPERTS, 128)),
        scratch_shapes=[pltpu.VMEM((N_EXPERTS, 1), jnp.float32)],
        compiler_params=pltpu.CompilerParams(
            dimension_semantics=("arbitrary",), vmem_limit_bytes=VMEM_LIMIT),
        name="router",
    )(h1, gffn, w_router_t, bias, wgs, wus, wds, tri)


def _slab(ref, first_sublane):
    return ref.at[pl.ds(pl.multiple_of(first_sublane, ROW_TILE), ROW_TILE)]


def _dispatch_kernel(dst_ref, xn_ref, xs_hbm, sem):
    n_tok = xn_ref.shape[0] // ROW_TILE

    def issue(g, carry):
        for u in range(ISSUE_UNROLL):
            t = g * ISSUE_UNROLL + u
            for s in range(TOP_K):
                pltpu.make_async_copy(_slab(xn_ref, t * ROW_TILE), _slab(xs_hbm, dst_ref[0, t * TOP_K + s]),
                                      sem).start(priority=s % DMA_QUEUES)
        return carry

    lax.fori_loop(0, n_tok // ISSUE_UNROLL, issue, 0)

    for s in range(TOP_K):
        pltpu.make_async_copy(xn_ref, xs_hbm.at[pl.ds(0, xn_ref.shape[0])], sem).wait()


def _dispatch(dst, xn, n_rows):
    w = xn.shape[1]
    t = DISPATCH_ROWS
    return pl.pallas_call(
        _dispatch_kernel,
        out_shape=jax.ShapeDtypeStruct((n_rows * ROW_TILE, w), xn.dtype),
        grid_spec=pltpu.PrefetchScalarGridSpec(
            num_scalar_prefetch=0,
            grid=(xn.shape[0] // (t * ROW_TILE),),
            in_specs=[pl.BlockSpec((None, 1, t * TOP_K), lambda i: (i, 0, 0), memory_space=pltpu.SMEM),
                      pl.BlockSpec((t * ROW_TILE, w), lambda i: (i, 0))],
            out_specs=pl.BlockSpec(memory_space=pl.ANY),
            scratch_shapes=[pltpu.SemaphoreType.DMA(())],
        ),
        compiler_params=pltpu.CompilerParams(dimension_semantics=("arbitrary",)),
        name="dispatch",
    )(dst, xn)


def _experts_kernel(blk_e_ref, blk_rows_ref, xs_ref, wg_ref, wu_ref, wd_ref, y_ref, wgb_ref, wub_ref, wdb_ref):
    i = pl.program_id(0)
    r = EXPERT_ROWS
    n_valid = blk_rows_ref[i]

    @pl.when((i == 0) | (blk_e_ref[i] != blk_e_ref[jnp.maximum(i - 1, 0)]))
    def _():
        wgb_ref[...] = wg_ref[...].astype(jnp.bfloat16)
        wub_ref[...] = wu_ref[...].astype(jnp.bfloat16)
        wdb_ref[...] = wd_ref[...].astype(jnp.bfloat16)

    @pl.when(n_valid > 0)
    def _():
        keep = lax.broadcasted_iota(jnp.int32, (r, LANES), 0) < n_valid
        lo, hi = _load_row_tiles(xs_ref, r)
        xb = jnp.concatenate([jnp.where(keep, c, 0.0).astype(jnp.bfloat16) for c in lo + hi], axis=-1)
        g = _dot(xb, wgb_ref[...])
        u = _dot(xb, wub_ref[...])
        g_sig, _ = _sigmoid_pair(g)
        hid = ((g * g_sig) * u).astype(jnp.bfloat16)
        _store_row_tiles(y_ref, _pack_rows(_dot(hid, wdb_ref[...])))

    @pl.when(n_valid == 0)
    def _():
        y_ref[...] = jnp.zeros_like(y_ref)


def _experts(blk_e, blk_rows, xs, wg, wu, wd):
    w = xs.shape[1]
    d, ff = wg.shape[1], wg.shape[2]
    r = EXPERT_ROWS
    rows = pl.BlockSpec((r * ROW_TILE, w), lambda i, be, br: (i, 0))
    return pl.pallas_call(
        _experts_kernel,
        out_shape=jax.ShapeDtypeStruct(xs.shape, xs.dtype),
        grid_spec=pltpu.PrefetchScalarGridSpec(
            num_scalar_prefetch=2,
            grid=(xs.shape[0] // (r * ROW_TILE),),
            in_specs=[
                rows,
                pl.BlockSpec((None, d, ff), lambda i, be, br: (be[i], 0, 0)),
                pl.BlockSpec((None, d, ff), lambda i, be, br: (be[i], 0, 0)),
                pl.BlockSpec((None, ff, d), lambda i, be, br: (be[i], 0, 0)),
            ],
            out_specs=rows,
            scratch_shapes=[pltpu.VMEM((d, ff), jnp.bfloat16), pltpu.VMEM((d, ff), jnp.bfloat16),
                            pltpu.VMEM((ff, d), jnp.bfloat16)],
        ),
        compiler_params=pltpu.CompilerParams(
            dimension_semantics=("arbitrary",), vmem_limit_bytes=VMEM_LIMIT),
        name="experts",
    )(blk_e, blk_rows, xs, wg, wu, wd)


def _combine_kernel(src_ref, src_nx_ref, base_ref, gates_ref, gfin_ref, y_hbm, out_ref, buf_ref, sem):
    i = pl.program_id(0)
    n_steps = pl.num_programs(0)
    n_tok = base_ref.shape[0]
    slot_rows = n_tok * ROW_TILE

    def issue(src_r, half):
        def body(g, carry):
            for u in range(ISSUE_UNROLL):
                t = g * ISSUE_UNROLL + u
                for s in range(TOP_K):
                    pltpu.make_async_copy(_slab(y_hbm, src_r[0, t * TOP_K + s]),
                                          _slab(buf_ref.at[half], s * slot_rows + t * ROW_TILE),
                                          sem.at[half]).start(priority=s % DMA_QUEUES)
            return carry
        lax.fori_loop(0, n_tok // ISSUE_UNROLL, body, 0)

    half = i % 2

    @pl.when(i == 0)
    def _():
        issue(src_ref, 0)

    @pl.when(i + 1 < n_steps)
    def _():
        issue(src_nx_ref, 1 - half)

    pltpu.make_async_copy(y_hbm.at[pl.ds(0, TOP_K * slot_rows)], buf_ref.at[half], sem.at[half]).wait()

    gates = gates_ref[...]
    width = ROW_TILE * LANES
    lo_chunks, hi_chunks = [], []
    for c in range(ROW_TILE):
        acc_lo = base_ref[:, c * LANES:(c + 1) * LANES]
        acc_hi = base_ref[:, width + c * LANES:width + (c + 1) * LANES]
        for s in range(TOP_K):
            lo, hi = _unpack_words(buf_ref[half, pl.ds(s * slot_rows + c, n_tok, stride=ROW_TILE), :])
            acc_lo = acc_lo + gates[:, s:s + 1] * lo
            acc_hi = acc_hi + gates[:, s:s + 1] * hi
        lo_chunks.append(acc_lo)
        hi_chunks.append(acc_hi)
    out_ref[...] = _rms(jnp.concatenate(lo_chunks + hi_chunks, axis=-1), gfin_ref[...])


def _combine(src, base, gates, gfin, y):
    n, d = base.shape
    w = y.shape[1]
    t = COMBINE_ROWS
    n_steps = n // t
    cur = pl.BlockSpec((None, 1, t * TOP_K), lambda i: (i, 0, 0), memory_space=pltpu.SMEM)
    nxt = pl.BlockSpec((None, 1, t * TOP_K), lambda i: (jnp.minimum(i + 1, n_steps - 1), 0, 0),
                       memory_space=pltpu.SMEM)
    return pl.pallas_call(
        _combine_kernel,
        out_shape=jax.ShapeDtypeStruct((n, d), jnp.float32),
        grid_spec=pltpu.PrefetchScalarGridSpec(
            num_scalar_prefetch=0,
            grid=(n_steps,),
            in_specs=[cur, nxt,
                      pl.BlockSpec((t, d), lambda i: (i, 0)),
                      pl.BlockSpec((t, TOP_K), lambda i: (i, 0)),
                      pl.BlockSpec((1, d), lambda i: (0, 0)),
                      pl.BlockSpec(memory_space=pl.ANY)],
            out_specs=pl.BlockSpec((t, d), lambda i: (i, 0)),
            scratch_shapes=[pltpu.VMEM((2, TOP_K * t * ROW_TILE, w), y.dtype),
                            pltpu.SemaphoreType.DMA((2,))],
        ),
        compiler_params=pltpu.CompilerParams(
            dimension_semantics=("arbitrary",), vmem_limit_bytes=VMEM_LIMIT),
        name="combine",
    )(src, src, base, gates, gfin, y)


def kernel(x, meta_tokens, norm_mix_g, w_in, lb_table, hgrn_norm_g, conv_w, conv_norm_g, w_out,
           norm_ffn_g, w_router, router_bias, w_gate_e, w_up_e, w_down_e, w_gate_s, w_up_s, w_down_s,
           norm_final_g):
    bsz, seq, d = x.shape
    n = bsz * seq
    bf = jnp.bfloat16
    assert seq % MIX_ROWS == 0 and MIX_ROWS % CHUNK == 0
    assert n % ROUTE_ROWS == 0 and n % DISPATCH_ROWS == 0 and n % COMBINE_ROWS == 0
    assert (n * TOP_K) % EXPERT_ROWS == 0

    meta_pad = jnp.zeros((CHUNK, d), jnp.float32).at[CHUNK - N_META:].set(meta_tokens)
    h1 = _mixer(x, meta_pad, norm_mix_g[0:1], w_in[0].astype(bf), lb_table, hgrn_norm_g[0:1],
                conv_w[0], conv_norm_g[0:1], w_out[0].astype(bf))

    base, xn, idx, gate, rank, counts = _router(
        h1.reshape(n, d), norm_ffn_g[0:1], w_router[0].T, router_bias[0][:, None],
        w_gate_s[0].astype(bf), w_up_s[0].astype(bf), w_down_s[0].astype(bf))

    r = EXPERT_ROWS
    n_blocks = (n * TOP_K) // r + N_EXPERTS
    counts = counts[:, 0]
    padded = (counts + r - 1) // r * r
    ends = jnp.cumsum(padded)
    starts = (ends - padded).astype(jnp.int32)
    blk_row0 = jnp.arange(n_blocks, dtype=jnp.int32) * r
    blk_e = jnp.minimum(jnp.sum((ends[None, :] <= blk_row0[:, None]).astype(jnp.int32), axis=1), N_EXPERTS - 1)
    onehot = (blk_e[:, None] == jnp.arange(N_EXPERTS, dtype=jnp.int32)[None, :]).astype(jnp.int32)
    blk_rows = jnp.clip(onehot @ counts - (blk_row0 - onehot @ starts), 0, r).astype(jnp.int32)

    dest = rank + jnp.sum(jnp.where(idx[None] == jnp.arange(N_EXPERTS, dtype=jnp.int32)[:, None, None],
                                    starts[:, None, None], 0), axis=0)
    slab = (dest * ROW_TILE).T
    xs = _dispatch(slab.reshape(n // DISPATCH_ROWS, 1, DISPATCH_ROWS * TOP_K), xn, n_blocks * r)
    y = _experts(blk_e, blk_rows, xs, w_gate_e[0], w_up_e[0], w_down_e[0])
    out = _combine(slab.reshape(n // COMBINE_ROWS, 1, COMBINE_ROWS * TOP_K), base, gate.T,
                   norm_final_g[None, :], y)
    return out.reshape(bsz, seq, d)
```

```python
import functools

import numpy as np
import jax
import jax.numpy as jnp
from jax import lax
from jax.experimental import pallas as pl
from jax.experimental.pallas import tpu as pltpu

N_META = 16
CHUNK = 128
HEADS = 4
HEAD_DIM = 128
HGRN_W = HEADS * HEAD_DIM
CONV_W = 512
CONV_GROUPS = 4
CONV_K = 3
N_EXPERTS = 64
TOP_K = 8
ROUTED_SCALE = 2.5
EPS = 1e-6

V7X_VMEM_BYTES = 64 * 1024 * 1024
VMEM_LIMIT = V7X_VMEM_BYTES - 8 * 1024 * 1024

MIX_ROWS = 512
CHUNK_UNROLL = 4
ROUTE_ROWS = 512
DISPATCH_ROWS = 512
EXPERT_ROWS = 1024
COMBINE_ROWS = 256
DMA_QUEUES = 2
ISSUE_UNROLL = 8

LANES = 128
ROW_TILE = 4

HALF_SPANS = (64, 32, 16, 8, 4, 2, 1)
N_LEVELS = len(HALF_SPANS) + 1


def _decay_sum_matrix():
    a = np.zeros((N_LEVELS, CHUNK, CHUNK), np.float32)
    a[0] = np.tril(np.ones((CHUNK, CHUNK), np.float32))
    for i, m in enumerate(HALF_SPANS):
        for t in range(CHUNK):
            mid = (t // (2 * m)) * 2 * m + m
            if t >= mid:
                a[1 + i, t, mid:t + 1] = 1.0
            else:
                a[1 + i, t, t + 1:mid] = 1.0
    return a.reshape(N_LEVELS * CHUNK, CHUNK)


def _level_matrix():
    lv = np.full((CHUNK, CHUNK), -1, np.int32)
    for t in range(CHUNK):
        lv[t, t] = len(HALF_SPANS)
        for s in range(t):
            top = (t ^ s).bit_length() - 1
            lv[t, s] = HALF_SPANS.index(1 << top)
    return lv


def _rms(x, g):
    return x * lax.rsqrt(jnp.mean(x * x, axis=-1, keepdims=True) + EPS) * g


def _group_rms(x, g, width):
    outs = []
    for j in range(x.shape[-1] // width):
        xs = x[:, j * width:(j + 1) * width]
        outs.append(xs * lax.rsqrt(jnp.mean(xs * xs, axis=-1, keepdims=True) + EPS))
    return jnp.concatenate(outs, axis=-1) * g


def _sigmoid_pair(z):
    t = jnp.exp(-jnp.abs(z))
    inv = 1.0 / (1.0 + t)
    big, small = inv, t * inv
    pos = z >= 0
    return jnp.where(pos, big, small), jnp.where(pos, small, big)


def _pack_rows(x):
    half = x.shape[1] // 2
    bits = lambda v: lax.bitcast_convert_type(v.astype(jnp.bfloat16).astype(jnp.float32), jnp.uint32)
    return (bits(x[:, :half]) >> 16) | (bits(x[:, half:]) & jnp.uint32(0xFFFF0000))


def _unpack_words(w):
    lo = lax.bitcast_convert_type(w << 16, jnp.float32)
    hi = lax.bitcast_convert_type(w & jnp.uint32(0xFFFF0000), jnp.float32)
    return lo, hi


def _store_row_tiles(ref, words):
    t = words.shape[0]
    for c in range(ROW_TILE):
        ref[pl.ds(c, t, stride=ROW_TILE), :] = words[:, c * LANES:(c + 1) * LANES]


def _load_row_tiles(ref, t):
    parts = [_unpack_words(ref[pl.ds(c, t, stride=ROW_TILE), :]) for c in range(ROW_TILE)]
    return [p[0] for p in parts], [p[1] for p in parts]


def _dot(a, b):
    return jnp.dot(a, b, preferred_element_type=jnp.float32)


def _dot_nt(a, b):
    return lax.dot_general(a, b, (((1,), (1,)), ((), ())), preferred_element_type=jnp.float32)


def _dot_tn(a, b):
    return lax.dot_general(a, b, (((0,), (0,)), ((), ())), preferred_element_type=jnp.float32)


def _hgrn_chunk(q, z, iv, lb, amat, level, st_ref, first_valid_row):
    sig, sig_neg = _sigmoid_pair(z)
    lf = jnp.log(lb + (1.0 - lb) * sig)
    k = (1.0 - lb) * sig_neg
    row = lax.broadcasted_iota(jnp.int32, (CHUNK, HGRN_W), 0)
    if first_valid_row:
        valid = row >= first_valid_row
        lf = jnp.where(valid, lf, 0.0)
        k = jnp.where(valid, k, 0.0)

    h1 = lf.astype(jnp.bfloat16)
    h2 = (lf - h1.astype(jnp.float32)).astype(jnp.bfloat16)
    e_all = _dot(amat, jnp.concatenate([h1, h2], axis=0))

    b = e_all[0:CHUNK]
    b_last = b[CHUNK - 1:CHUNK]
    q_in = (q * jnp.exp(b)).astype(jnp.bfloat16)
    k_out = (k * jnp.exp(b_last - b)).astype(jnp.bfloat16)
    st_decay = jnp.exp(b_last)
    v_bf = iv.astype(jnp.bfloat16)

    q_lv = [q.astype(jnp.bfloat16)]
    k_lv = [k.astype(jnp.bfloat16)]
    for i, m in enumerate(HALF_SPANS):
        ex = jnp.exp(e_all[(1 + i) * CHUNK:(2 + i) * CHUNK])
        right = (row & m) != 0
        q_lv.append(jnp.where(right, q * ex, 0.0).astype(jnp.bfloat16))
        k_lv.append(jnp.where(right, 0.0, k * ex).astype(jnp.bfloat16))
    lv_of = [len(HALF_SPANS)] + list(range(len(HALF_SPANS)))

    outs = []
    for h in range(HEADS):
        cols = slice(h * HEAD_DIM, (h + 1) * HEAD_DIM)
        scores = jnp.zeros((CHUNK, CHUNK), jnp.float32)
        for ql, kl, lv in zip(q_lv, k_lv, lv_of):
            scores = jnp.where(level == lv, _dot_nt(ql[:, cols], kl[:, cols]), scores)
        st = st_ref[h]
        o = _dot(scores.astype(jnp.bfloat16), v_bf[:, cols]) + _dot_nt(q_in[:, cols], st.astype(jnp.bfloat16))
        st_ref[h] = st * st_decay[:, cols] + _dot_tn(v_bf[:, cols], k_out[:, cols])
        outs.append(o)
    return jnp.concatenate(outs, axis=-1)


def _mixer_kernel(x_ref, meta_ref, gmix_ref, win_ref, lbt_ref, ghg_ref, cw_ref, gcv_ref, wout_ref,
                  amat_ref, level_ref, h1_ref, proj_ref, o_ref, u_ref, st_ref):
    j = pl.program_id(1)
    rows = x_ref.shape[0]
    n_in = win_ref.shape[1]

    lbt = lbt_ref[...]
    lbe = jnp.exp(lbt - jnp.max(lbt, axis=0, keepdims=True))
    lb = lbe[0:1] / jnp.sum(lbe, axis=0, keepdims=True)

    amat = amat_ref[...]
    level = level_ref[...]
    gmix = gmix_ref[...]

    def project(xv, dst_rows):
        xn = _rms(xv, gmix).astype(jnp.bfloat16)
        for c0 in range(0, n_in, 512):
            proj_ref[dst_rows, c0:c0 + 512] = _dot(xn, win_ref[:, c0:c0 + 512])

    @pl.when(j == 0)
    def _():
        st_ref[...] = jnp.zeros_like(st_ref)
        project(meta_ref[...], pl.ds(0, CHUNK))
        pm = proj_ref[0:CHUNK, :]
        _hgrn_chunk(pm[:, 0:512], pm[:, 512:1024], pm[:, 1024:1536], lb, amat, level, st_ref,
                    CHUNK - N_META)
        u_ref[0:8, :] = (pm[:, 2560:3072] * pm[:, 3072:3584])[CHUNK - 8:CHUNK]

    project(x_ref[...], pl.ds(0, rows))

    def chunk_body(c, carry):
        for u in range(CHUNK_UNROLL):
            r0 = pl.multiple_of((c * CHUNK_UNROLL + u) * CHUNK, CHUNK)
            q = proj_ref[pl.ds(r0, CHUNK), 0:512]
            z = proj_ref[pl.ds(r0, CHUNK), 512:1024]
            iv = proj_ref[pl.ds(r0, CHUNK), 1024:1536]
            o_ref[pl.ds(r0, CHUNK), :] = _hgrn_chunk(q, z, iv, lb, amat, level, st_ref, 0)
        return carry

    lax.fori_loop(0, rows // (CHUNK * CHUNK_UNROLL), chunk_body, 0)

    g_out = proj_ref[:, 1536:2048]
    g_sig, _ = _sigmoid_pair(g_out)
    y_hgrn = _group_rms(o_ref[...], ghg_ref[...], HEAD_DIM) * (g_out * g_sig)

    u = proj_ref[:, 2560:3072] * proj_ref[:, 3072:3584]
    u_ref[8:8 + rows, :] = u
    cw = cw_ref[...]
    y = cw[2:3] * u + cw[1:2] * u_ref[7:7 + rows, :] + cw[0:1] * u_ref[6:6 + rows, :]
    u_ref[0:8, :] = u[rows - 8:rows]
    y_conv = _group_rms(proj_ref[:, 2048:2560] * y, gcv_ref[...], CONV_W // CONV_GROUPS)

    mixed = jnp.concatenate([y_hgrn, y_conv], axis=-1).astype(jnp.bfloat16)
    h1_ref[...] = x_ref[...] + _dot(mixed, wout_ref[...])


def _mixer(x, meta_pad, gmix, w_in, lb_table, ghg, conv_w, gcv, w_out):
    bsz, seq, d = x.shape
    n_in = w_in.shape[1]
    rows = MIX_ROWS
    const = lambda *shape: pl.BlockSpec(shape, lambda b, j: (0,) * len(shape))
    return pl.pallas_call(
        _mixer_kernel,
        out_shape=jax.ShapeDtypeStruct((bsz, seq, d), jnp.float32),
        grid=(bsz, seq // rows),
        in_specs=[
            pl.BlockSpec((None, rows, d), lambda b, j: (b, j, 0)),
            const(CHUNK, d), const(1, d), const(d, n_in), const(*lb_table.shape), const(1, HGRN_W),
            const(CONV_K, CONV_W), const(1, CONV_W), const(d, d),
            const(N_LEVELS * CHUNK, 2 * CHUNK), const(CHUNK, CHUNK),
        ],
        out_specs=pl.BlockSpec((None, rows, d), lambda b, j: (b, j, 0)),
        scratch_shapes=[
            pltpu.VMEM((rows, n_in), jnp.float32),
            pltpu.VMEM((rows, HGRN_W), jnp.float32),
            pltpu.VMEM((rows + 8, CONV_W), jnp.float32),
            pltpu.VMEM((HEADS, HEAD_DIM, HEAD_DIM), jnp.float32),
        ],
        compiler_params=pltpu.CompilerParams(
            dimension_semantics=("arbitrary", "arbitrary"), vmem_limit_bytes=VMEM_LIMIT),
        name="mixer",
    )(x, meta_pad, gmix, w_in, lb_table, ghg, conv_w, gcv, w_out,
      jnp.asarray(np.tile(_decay_sum_matrix(), (1, 2)), jnp.bfloat16), jnp.asarray(_level_matrix()))


def _router_kernel(h1_ref, gffn_ref, wr_ref, bias_ref, wgs_ref, wus_ref, wds_ref, tri_ref,
                   base_ref, xn_ref, idx_ref, gate_ref, rank_ref, counts_ref, carry_ref):
    i = pl.program_id(0)
    n_tok = h1_ref.shape[0]

    @pl.when(i == 0)
    def _():
        carry_ref[...] = jnp.zeros_like(carry_ref)

    h1 = h1_ref[...]
    xn = _rms(h1, gffn_ref[...])
    _store_row_tiles(xn_ref, _pack_rows(xn))
    xb = xn.astype(jnp.bfloat16)

    g_pre = _dot(xb, wgs_ref[...])
    gate_s, _ = _sigmoid_pair(g_pre)
    hid = (g_pre * gate_s) * _dot(xb, wus_ref[...])
    base_ref[...] = h1 + _dot(hid.astype(jnp.bfloat16), wds_ref[...])

    logits = lax.dot_general(wr_ref[...], xn, (((1,), (1,)), ((), ())),
                             precision=lax.Precision.HIGHEST, preferred_element_type=jnp.float32)
    scores, _ = _sigmoid_pair(logits)
    sel = scores + bias_ref[...]
    eid = lax.broadcasted_iota(jnp.int32, (N_EXPERTS, n_tok), 0).astype(jnp.float32)
    picks, top_s = [], []
    for _ in range(TOP_K):
        best = jnp.max(sel, axis=0, keepdims=True)
        pick = jnp.min(jnp.where(sel == best, eid, float(N_EXPERTS)), axis=0, keepdims=True)
        hit = eid == pick
        top_s.append(jnp.sum(jnp.where(hit, scores, 0.0), axis=0, keepdims=True))
        sel = jnp.where(hit, -jnp.inf, sel)
        picks.append(pick)
    top_s = jnp.concatenate(top_s, axis=0)
    gate_ref[...] = top_s / jnp.sum(top_s, axis=0, keepdims=True) * ROUTED_SCALE
    idx_ref[...] = jnp.concatenate(picks, axis=0).astype(jnp.int32)

    chosen = jnp.zeros((N_EXPERTS, n_tok), jnp.float32)
    for pick in picks:
        chosen = chosen + jnp.where(eid == pick, 1.0, 0.0)
    incl = _dot(chosen.astype(jnp.bfloat16), tri_ref[...])
    before = carry_ref[...] + incl - 1.0
    ranks = [jnp.sum(jnp.where(eid == pick, before, 0.0), axis=0, keepdims=True) for pick in picks]
    rank_ref[...] = jnp.concatenate(ranks, axis=0).astype(jnp.int32)
    carry_ref[...] = carry_ref[...] + incl[:, n_tok - 1:n_tok]
    counts_ref[...] = jnp.broadcast_to(carry_ref[...], counts_ref.shape).astype(jnp.int32)


def _router(h1, gffn, w_router, bias, wgs, wus, wds):
    n, d = h1.shape
    t = ROUTE_ROWS
    ff = wgs.shape[1]
    tri = jnp.asarray(np.triu(np.ones((t, t), np.float32)), jnp.bfloat16)
    const = lambda *shape: pl.BlockSpec(shape, lambda i: (0,) * len(shape))
    tok = lambda width: pl.BlockSpec((t, width), lambda i: (i, 0))
    slot = pl.BlockSpec((TOP_K, t), lambda i: (0, i))
    return pl.pallas_call(
        _router_kernel,
        out_shape=(
            jax.ShapeDtypeStruct((n, d), jnp.float32),
            jax.ShapeDtypeStruct((n * ROW_TILE, LANES), jnp.uint32),
            jax.ShapeDtypeStruct((TOP_K, n), jnp.int32),
            jax.ShapeDtypeStruct((TOP_K, n), jnp.float32),
            jax.ShapeDtypeStruct((TOP_K, n), jnp.int32),
            jax.ShapeDtypeStruct((N_EXPERTS, 128), jnp.int32),
        ),
        grid=(n // t,),
        in_specs=[tok(d), const(1, d), const(N_EXPERTS, d), const(N_EXPERTS, 1),
                  const(d, ff), const(d, ff), const(ff, d), const(t, t)],
        out_specs=(tok(d), pl.BlockSpec((t * ROW_TILE, LANES), lambda i: (i, 0)), slot, slot, slot,
                   const(N_EXPERTS, 128)),
        scratch_shapes=[pltpu.VMEM((N_EXPERTS, 1), jnp.float32)],
        compiler_params=pltpu.CompilerParams(
            dimension_semantics=("arbitrary",), vmem_limit_bytes=VMEM_LIMIT),
        name="router",
    )(h1, gffn, w_router, bias, wgs, wus, wds, tri)


def _slab(ref, first_sublane):
    return ref.at[pl.ds(pl.multiple_of(first_sublane, ROW_TILE), ROW_TILE)]


def _dispatch_kernel(dst_ref, xn_ref, xs_hbm, sem):
    n_tok = xn_ref.shape[0] // ROW_TILE

    def issue(g, carry):
        for u in range(ISSUE_UNROLL):
            t = g * ISSUE_UNROLL + u
            for s in range(TOP_K):
                pltpu.make_async_copy(_slab(xn_ref, t * ROW_TILE), _slab(xs_hbm, dst_ref[0, t * TOP_K + s]),
                                      sem).start(priority=s % DMA_QUEUES)
        return carry

    lax.fori_loop(0, n_tok // ISSUE_UNROLL, issue, 0)

    for s in range(TOP_K):
        pltpu.make_async_copy(xn_ref, xs_hbm.at[pl.ds(0, xn_ref.shape[0])], sem).wait()


def _dispatch(dst, xn, n_rows):
    w = xn.shape[1]
    t = DISPATCH_ROWS
    return pl.pallas_call(
        _dispatch_kernel,
        out_shape=jax.ShapeDtypeStruct((n_rows * ROW_TILE, w), xn.dtype),
        grid_spec=pltpu.PrefetchScalarGridSpec(
            num_scalar_prefetch=0,
            grid=(xn.shape[0] // (t * ROW_TILE),),
            in_specs=[pl.BlockSpec((None, 1, t * TOP_K), lambda i: (i, 0, 0), memory_space=pltpu.SMEM),
                      pl.BlockSpec((t * ROW_TILE, w), lambda i: (i, 0))],
            out_specs=pl.BlockSpec(memory_space=pl.ANY),
            scratch_shapes=[pltpu.SemaphoreType.DMA(())],
        ),
        compiler_params=pltpu.CompilerParams(dimension_semantics=("arbitrary",)),
        name="dispatch",
    )(dst, xn)


def _experts_kernel(blk_e_ref, blk_rows_ref, xs_ref, wg_ref, wu_ref, wd_ref, y_ref, wgb_ref, wub_ref, wdb_ref):
    i = pl.program_id(0)
    r = EXPERT_ROWS
    n_valid = blk_rows_ref[i]

    @pl.when((i == 0) | (blk_e_ref[i] != blk_e_ref[jnp.maximum(i - 1, 0)]))
    def _():
        wgb_ref[...] = wg_ref[...].astype(jnp.bfloat16)
        wub_ref[...] = wu_ref[...].astype(jnp.bfloat16)
        wdb_ref[...] = wd_ref[...].astype(jnp.bfloat16)

    @pl.when(n_valid > 0)
    def _():
        keep = lax.broadcasted_iota(jnp.int32, (r, LANES), 0) < n_valid
        lo, hi = _load_row_tiles(xs_ref, r)
        xb = jnp.concatenate([jnp.where(keep, c, 0.0).astype(jnp.bfloat16) for c in lo + hi], axis=-1)
        g = _dot(xb, wgb_ref[...])
        u = _dot(xb, wub_ref[...])
        g_sig, _ = _sigmoid_pair(g)
        hid = ((g * g_sig) * u).astype(jnp.bfloat16)
        _store_row_tiles(y_ref, _pack_rows(_dot(hid, wdb_ref[...])))

    @pl.when(n_valid == 0)
    def _():
        y_ref[...] = jnp.zeros_like(y_ref)


def _experts(blk_e, blk_rows, xs, wg, wu, wd):
    w = xs.shape[1]
    d, ff = wg.shape[1], wg.shape[2]
    r = EXPERT_ROWS
    rows = pl.BlockSpec((r * ROW_TILE, w), lambda i, be, br: (i, 0))
    return pl.pallas_call(
        _experts_kernel,
        out_shape=jax.ShapeDtypeStruct(xs.shape, xs.dtype),
        grid_spec=pltpu.PrefetchScalarGridSpec(
            num_scalar_prefetch=2,
            grid=(xs.shape[0] // (r * ROW_TILE),),
            in_specs=[
                rows,
                pl.BlockSpec((None, d, ff), lambda i, be, br: (be[i], 0, 0)),
                pl.BlockSpec((None, d, ff), lambda i, be, br: (be[i], 0, 0)),
                pl.BlockSpec((None, ff, d), lambda i, be, br: (be[i], 0, 0)),
            ],
            out_specs=rows,
            scratch_shapes=[pltpu.VMEM((d, ff), jnp.bfloat16), pltpu.VMEM((d, ff), jnp.bfloat16),
                            pltpu.VMEM((ff, d), jnp.bfloat16)],
        ),
        compiler_params=pltpu.CompilerParams(
            dimension_semantics=("arbitrary",), vmem_limit_bytes=VMEM_LIMIT),
        name="experts",
    )(blk_e, blk_rows, xs, wg, wu, wd)


def _combine_kernel(src_ref, src_nx_ref, base_ref, gates_ref, gfin_ref, y_hbm, out_ref, buf_ref, sem):
    i = pl.program_id(0)
    n_steps = pl.num_programs(0)
    n_tok = base_ref.shape[0]
    slot_rows = n_tok * ROW_TILE

    def issue(src_r, half):
        def body(g, carry):
            for u in range(ISSUE_UNROLL):
                t = g * ISSUE_UNROLL + u
                for s in range(TOP_K):
                    pltpu.make_async_copy(_slab(y_hbm, src_r[0, t * TOP_K + s]),
                                          _slab(buf_ref.at[half], s * slot_rows + t * ROW_TILE),
                                          sem.at[half]).start(priority=s % DMA_QUEUES)
            return carry
        lax.fori_loop(0, n_tok // ISSUE_UNROLL, body, 0)

    half = i % 2

    @pl.when(i == 0)
    def _():
        issue(src_ref, 0)

    @pl.when(i + 1 < n_steps)
    def _():
        issue(src_nx_ref, 1 - half)

    pltpu.make_async_copy(y_hbm.at[pl.ds(0, TOP_K * slot_rows)], buf_ref.at[half], sem.at[half]).wait()

    gates = gates_ref[...]
    width = ROW_TILE * LANES
    lo_chunks, hi_chunks = [], []
    for c in range(ROW_TILE):
        acc_lo = base_ref[:, c * LANES:(c + 1) * LANES]
        acc_hi = base_ref[:, width + c * LANES:width + (c + 1) * LANES]
        for s in range(TOP_K):
            lo, hi = _unpack_words(buf_ref[half, pl.ds(s * slot_rows + c, n_tok, stride=ROW_TILE), :])
            acc_lo = acc_lo + gates[:, s:s + 1] * lo
            acc_hi = acc_hi + gates[:, s:s + 1] * hi
        lo_chunks.append(acc_lo)
        hi_chunks.append(acc_hi)
    out_ref[...] = _rms(jnp.concatenate(lo_chunks + hi_chunks, axis=-1), gfin_ref[...])


def _combine(src, base, gates, gfin, y):
    n, d = base.shape
    w = y.shape[1]
    t = COMBINE_ROWS
    n_steps = n // t
    cur = pl.BlockSpec((None, 1, t * TOP_K), lambda i: (i, 0, 0), memory_space=pltpu.SMEM)
    nxt = pl.BlockSpec((None, 1, t * TOP_K), lambda i: (jnp.minimum(i + 1, n_steps - 1), 0, 0),
                       memory_space=pltpu.SMEM)
    return pl.pallas_call(
        _combine_kernel,
        out_shape=jax.ShapeDtypeStruct((n, d), jnp.float32),
        grid_spec=pltpu.PrefetchScalarGridSpec(
            num_scalar_prefetch=0,
            grid=(n_steps,),
            in_specs=[cur, nxt,
                      pl.BlockSpec((t, d), lambda i: (i, 0)),
                      pl.BlockSpec((t, TOP_K), lambda i: (i, 0)),
                      pl.BlockSpec((1, d), lambda i: (0, 0)),
                      pl.BlockSpec(memory_space=pl.ANY)],
            out_specs=pl.BlockSpec((t, d), lambda i: (i, 0)),
            scratch_shapes=[pltpu.VMEM((2, TOP_K * t * ROW_TILE, w), y.dtype),
                            pltpu.SemaphoreType.DMA((2,))],
        ),
        compiler_params=pltpu.CompilerParams(
            dimension_semantics=("arbitrary",), vmem_limit_bytes=VMEM_LIMIT),
        name="combine",
    )(src, src, base, gates, gfin, y)


def kernel(x, meta_tokens, norm_mix_g, w_in, lb_table, hgrn_norm_g, conv_w, conv_norm_g, w_out,
           norm_ffn_g, w_router, router_bias, w_gate_e, w_up_e, w_down_e, w_gate_s, w_up_s, w_down_s,
           norm_final_g):
    bsz, seq, d = x.shape
    n = bsz * seq
    bf = jnp.bfloat16
    assert seq % MIX_ROWS == 0 and MIX_ROWS % CHUNK == 0
    assert n % ROUTE_ROWS == 0 and n % DISPATCH_ROWS == 0 and n % COMBINE_ROWS == 0
    assert (n * TOP_K) % EXPERT_ROWS == 0

    meta_pad = jnp.zeros((CHUNK, d), jnp.float32).at[CHUNK - N_META:].set(meta_tokens)
    h1 = _mixer(x, meta_pad, norm_mix_g[0:1], w_in[0].astype(bf), lb_table, hgrn_norm_g[0:1],
                conv_w[0], conv_norm_g[0:1], w_out[0].astype(bf))

    base, xn, idx, gate, rank, counts = _router(
        h1.reshape(n, d), norm_ffn_g[0:1], w_router[0].T, router_bias[0][:, None],
        w_gate_s[0].astype(bf), w_up_s[0].astype(bf), w_down_s[0].astype(bf))

    r = EXPERT_ROWS
    n_blocks = (n * TOP_K) // r + N_EXPERTS
    counts = counts[:, 0]
    padded = (counts + r - 1) // r * r
    ends = jnp.cumsum(padded)
    starts = (ends - padded).astype(jnp.int32)
    blk_row0 = jnp.arange(n_blocks, dtype=jnp.int32) * r
    blk_e = jnp.minimum(jnp.sum((ends[None, :] <= blk_row0[:, None]).astype(jnp.int32), axis=1), N_EXPERTS - 1)
    onehot = (blk_e[:, None] == jnp.arange(N_EXPERTS, dtype=jnp.int32)[None, :]).astype(jnp.int32)
    blk_rows = jnp.clip(onehot @ counts - (blk_row0 - onehot @ starts), 0, r).astype(jnp.int32)

    dest = rank + jnp.sum(jnp.where(idx[None] == jnp.arange(N_EXPERTS, dtype=jnp.int32)[:, None, None],
                                    starts[:, None, None], 0), axis=0)
    slab = (dest * ROW_TILE).T
    xs = _dispatch(slab.reshape(n // DISPATCH_ROWS, 1, DISPATCH_ROWS * TOP_K), xn, n_blocks * r)
    y = _experts(blk_e, blk_rows, xs, w_gate_e[0], w_up_e[0], w_down_e[0])
    out = _combine(slab.reshape(n // COMBINE_ROWS, 1, COMBINE_ROWS * TOP_K), base, gate.T,
                   norm_final_g[None, :], y)
    return out.reshape(bsz, seq, d)
```

```python
import functools

import numpy as np
import jax
import jax.numpy as jnp
from jax import lax
from jax.experimental import pallas as pl
from jax.experimental.pallas import tpu as pltpu

N_META = 16
CHUNK = 128
HEADS = 4
HEAD_DIM = 128
HGRN_W = HEADS * HEAD_DIM
CONV_W = 512
CONV_GROUPS = 4
CONV_K = 3
N_EXPERTS = 64
TOP_K = 8
ROUTED_SCALE = 2.5
EPS = 1e-6

V7X_VMEM_BYTES = 64 * 1024 * 1024
VMEM_LIMIT = V7X_VMEM_BYTES - 8 * 1024 * 1024

MIX_ROWS = 512
CHUNK_UNROLL = 4
ROUTE_ROWS = 512
DISPATCH_ROWS = 512
EXPERT_ROWS = 1024
COMBINE_ROWS = 256
DMA_QUEUES = 2
ISSUE_UNROLL = 8
TOKEN_GROUPS = 2
SIDE_ROWS = 128

LANES = 128
ROW_TILE = 4

HALF_SPANS = (64, 32, 16, 8, 4, 2, 1)
N_LEVELS = len(HALF_SPANS) + 1


def _decay_sum_matrix():
    a = np.zeros((N_LEVELS, CHUNK, CHUNK), np.float32)
    a[0] = np.tril(np.ones((CHUNK, CHUNK), np.float32))
    for i, m in enumerate(HALF_SPANS):
        for t in range(CHUNK):
            mid = (t // (2 * m)) * 2 * m + m
            if t >= mid:
                a[1 + i, t, mid:t + 1] = 1.0
            else:
                a[1 + i, t, t + 1:mid] = 1.0
    return a.reshape(N_LEVELS * CHUNK, CHUNK)


def _level_matrix():
    lv = np.full((CHUNK, CHUNK), -1, np.int32)
    for t in range(CHUNK):
        lv[t, t] = len(HALF_SPANS)
        for s in range(t):
            top = (t ^ s).bit_length() - 1
            lv[t, s] = HALF_SPANS.index(1 << top)
    return lv


def _rms(x, g):
    return x * lax.rsqrt(jnp.mean(x * x, axis=-1, keepdims=True) + EPS) * g


def _group_rms(x, g, width):
    outs = []
    for j in range(x.shape[-1] // width):
        xs = x[:, j * width:(j + 1) * width]
        outs.append(xs * lax.rsqrt(jnp.mean(xs * xs, axis=-1, keepdims=True) + EPS))
    return jnp.concatenate(outs, axis=-1) * g


def _sigmoid_pair(z):
    t = jnp.exp(-jnp.abs(z))
    inv = 1.0 / (1.0 + t)
    big, small = inv, t * inv
    pos = z >= 0
    return jnp.where(pos, big, small), jnp.where(pos, small, big)


def _pack_rows(x):
    half = x.shape[1] // 2
    bits = lambda v: lax.bitcast_convert_type(v.astype(jnp.bfloat16).astype(jnp.float32), jnp.uint32)
    return (bits(x[:, :half]) >> 16) | (bits(x[:, half:]) & jnp.uint32(0xFFFF0000))


def _unpack_words(w):
    lo = lax.bitcast_convert_type(w << 16, jnp.float32)
    hi = lax.bitcast_convert_type(w & jnp.uint32(0xFFFF0000), jnp.float32)
    return lo, hi


def _store_row_tiles(ref, words):
    t = words.shape[0]
    for c in range(ROW_TILE):
        ref[pl.ds(c, t, stride=ROW_TILE), :] = words[:, c * LANES:(c + 1) * LANES]


def _load_row_tiles(ref, t):
    parts = [_unpack_words(ref[pl.ds(c, t, stride=ROW_TILE), :]) for c in range(ROW_TILE)]
    return [p[0] for p in parts], [p[1] for p in parts]


def _dot(a, b):
    return jnp.dot(a, b, preferred_element_type=jnp.float32)


def _dot_nt(a, b):
    return lax.dot_general(a, b, (((1,), (1,)), ((), ())), preferred_element_type=jnp.float32)


def _dot_tn(a, b):
    return lax.dot_general(a, b, (((0,), (0,)), ((), ())), preferred_element_type=jnp.float32)


def _hgrn_chunk(q, z, iv, lb, amat, level, st_ref, first_valid_row):
    sig, sig_neg = _sigmoid_pair(z)
    lf = jnp.log(lb + (1.0 - lb) * sig)
    k = (1.0 - lb) * sig_neg
    row = lax.broadcasted_iota(jnp.int32, (CHUNK, HGRN_W), 0)
    if first_valid_row:
        valid = row >= first_valid_row
        lf = jnp.where(valid, lf, 0.0)
        k = jnp.where(valid, k, 0.0)

    h1 = lf.astype(jnp.bfloat16)
    h2 = (lf - h1.astype(jnp.float32)).astype(jnp.bfloat16)
    e_all = _dot(amat, jnp.concatenate([h1, h2], axis=0))

    b = e_all[0:CHUNK]
    b_last = b[CHUNK - 1:CHUNK]
    q_in = (q * jnp.exp(b)).astype(jnp.bfloat16)
    k_out = (k * jnp.exp(b_last - b)).astype(jnp.bfloat16)
    st_decay = jnp.exp(b_last)
    v_bf = iv.astype(jnp.bfloat16)

    q_lv = [q.astype(jnp.bfloat16)]
    k_lv = [k.astype(jnp.bfloat16)]
    for i, m in enumerate(HALF_SPANS):
        ex = jnp.exp(e_all[(1 + i) * CHUNK:(2 + i) * CHUNK])
        right = (row & m) != 0
        q_lv.append(jnp.where(right, q * ex, 0.0).astype(jnp.bfloat16))
        k_lv.append(jnp.where(right, 0.0, k * ex).astype(jnp.bfloat16))
    lv_of = [len(HALF_SPANS)] + list(range(len(HALF_SPANS)))

    outs = []
    for h in range(HEADS):
        cols = slice(h * HEAD_DIM, (h + 1) * HEAD_DIM)
        scores = jnp.zeros((CHUNK, CHUNK), jnp.float32)
        for ql, kl, lv in zip(q_lv, k_lv, lv_of):
            scores = jnp.where(level == lv, _dot_nt(ql[:, cols], kl[:, cols]), scores)
        st = st_ref[h]
        o = _dot(scores.astype(jnp.bfloat16), v_bf[:, cols]) + _dot_nt(q_in[:, cols], st.astype(jnp.bfloat16))
        st_ref[h] = st * st_decay[:, cols] + _dot_tn(v_bf[:, cols], k_out[:, cols])
        outs.append(o)
    return jnp.concatenate(outs, axis=-1)


def _mixer_kernel(x_ref, meta_ref, gmix_ref, win_ref, lbt_ref, ghg_ref, cw_ref, gcv_ref, wout_ref,
                  amat_ref, level_ref, h1_ref, proj_ref, o_ref, u_ref, st_ref):
    j = pl.program_id(1)
    rows = x_ref.shape[0]
    n_in = win_ref.shape[1]

    lbt = lbt_ref[...]
    lbe = jnp.exp(lbt - jnp.max(lbt, axis=0, keepdims=True))
    lb = lbe[0:1] / jnp.sum(lbe, axis=0, keepdims=True)

    amat = amat_ref[...]
    level = level_ref[...]
    gmix = gmix_ref[...]

    def project(xv, dst_rows):
        xn = _rms(xv, gmix).astype(jnp.bfloat16)
        for c0 in range(0, n_in, 512):
            proj_ref[dst_rows, c0:c0 + 512] = _dot(xn, win_ref[:, c0:c0 + 512])

    @pl.when(j == 0)
    def _():
        st_ref[...] = jnp.zeros_like(st_ref)
        project(meta_ref[...], pl.ds(0, CHUNK))
        pm = proj_ref[0:CHUNK, :]
        _hgrn_chunk(pm[:, 0:512], pm[:, 512:1024], pm[:, 1024:1536], lb, amat, level, st_ref,
                    CHUNK - N_META)
        u_ref[0:8, :] = (pm[:, 2560:3072] * pm[:, 3072:3584])[CHUNK - 8:CHUNK]

    project(x_ref[...], pl.ds(0, rows))

    def chunk_body(c, carry):
        for u in range(CHUNK_UNROLL):
            r0 = pl.multiple_of((c * CHUNK_UNROLL + u) * CHUNK, CHUNK)
            q = proj_ref[pl.ds(r0, CHUNK), 0:512]
            z = proj_ref[pl.ds(r0, CHUNK), 512:1024]
            iv = proj_ref[pl.ds(r0, CHUNK), 1024:1536]
            o_ref[pl.ds(r0, CHUNK), :] = _hgrn_chunk(q, z, iv, lb, amat, level, st_ref, 0)
        return carry

    lax.fori_loop(0, rows // (CHUNK * CHUNK_UNROLL), chunk_body, 0)

    g_out = proj_ref[:, 1536:2048]
    g_sig, _ = _sigmoid_pair(g_out)
    y_hgrn = _group_rms(o_ref[...], ghg_ref[...], HEAD_DIM) * (g_out * g_sig)

    u = proj_ref[:, 2560:3072] * proj_ref[:, 3072:3584]
    u_ref[8:8 + rows, :] = u
    cw = cw_ref[...]
    y = cw[2:3] * u + cw[1:2] * u_ref[7:7 + rows, :] + cw[0:1] * u_ref[6:6 + rows, :]
    u_ref[0:8, :] = u[rows - 8:rows]
    y_conv = _group_rms(proj_ref[:, 2048:2560] * y, gcv_ref[...], CONV_W // CONV_GROUPS)

    mixed = jnp.concatenate([y_hgrn, y_conv], axis=-1).astype(jnp.bfloat16)
    h1_ref[...] = x_ref[...] + _dot(mixed, wout_ref[...])


def _mixer(x, meta_pad, gmix, w_in, lb_table, ghg, conv_w, gcv, w_out):
    bsz, seq, d = x.shape
    n_in = w_in.shape[1]
    rows = MIX_ROWS
    const = lambda *shape: pl.BlockSpec(shape, lambda b, j: (0,) * len(shape))
    return pl.pallas_call(
        _mixer_kernel,
        out_shape=jax.ShapeDtypeStruct((bsz, seq, d), jnp.float32),
        grid=(bsz, seq // rows),
        in_specs=[
            pl.BlockSpec((None, rows, d), lambda b, j: (b, j, 0)),
            const(CHUNK, d), const(1, d), const(d, n_in), const(*lb_table.shape), const(1, HGRN_W),
            const(CONV_K, CONV_W), const(1, CONV_W), const(d, d),
            const(N_LEVELS * CHUNK, 2 * CHUNK), const(CHUNK, CHUNK),
        ],
        out_specs=pl.BlockSpec((None, rows, d), lambda b, j: (b, j, 0)),
        scratch_shapes=[
            pltpu.VMEM((rows, n_in), jnp.float32),
            pltpu.VMEM((rows, HGRN_W), jnp.float32),
            pltpu.VMEM((rows + 8, CONV_W), jnp.float32),
            pltpu.VMEM((HEADS, HEAD_DIM, HEAD_DIM), jnp.float32),
        ],
        compiler_params=pltpu.CompilerParams(
            dimension_semantics=("arbitrary", "arbitrary"), vmem_limit_bytes=VMEM_LIMIT),
        name="mixer",
    )(x, meta_pad, gmix, w_in, lb_table, ghg, conv_w, gcv, w_out,
      jnp.asarray(np.tile(_decay_sum_matrix(), (1, 2)), jnp.bfloat16), jnp.asarray(_level_matrix()))


def _router_kernel(h1_ref, gffn_ref, wr_ref, bias_ref, wgs_ref, wus_ref, wds_ref, tri_ref,
                   base_ref, xn_ref, idx_ref, gate_ref, rank_ref, counts_ref, carry_ref):
    i = pl.program_id(0)
    n_tok = h1_ref.shape[0]

    @pl.when(i % (pl.num_programs(0) // TOKEN_GROUPS) == 0)
    def _():
        carry_ref[...] = jnp.zeros_like(carry_ref)

    h1 = h1_ref[...]
    xn = _rms(h1, gffn_ref[...])
    _store_row_tiles(xn_ref, _pack_rows(xn))
    xb = xn.astype(jnp.bfloat16)

    g_pre = _dot(xb, wgs_ref[...])
    gate_s, _ = _sigmoid_pair(g_pre)
    hid = (g_pre * gate_s) * _dot(xb, wus_ref[...])
    base_ref[...] = h1 + _dot(hid.astype(jnp.bfloat16), wds_ref[...])

    logits = lax.dot_general(wr_ref[...], xn, (((1,), (1,)), ((), ())),
                             precision=lax.Precision.HIGHEST, preferred_element_type=jnp.float32)
    scores, _ = _sigmoid_pair(logits)
    sel = scores + bias_ref[...]
    eid = lax.broadcasted_iota(jnp.int32, (N_EXPERTS, n_tok), 0).astype(jnp.float32)
    picks, top_s = [], []
    for _ in range(TOP_K):
        best = jnp.max(sel, axis=0, keepdims=True)
        pick = jnp.min(jnp.where(sel == best, eid, float(N_EXPERTS)), axis=0, keepdims=True)
        hit = eid == pick
        top_s.append(jnp.sum(jnp.where(hit, scores, 0.0), axis=0, keepdims=True))
        sel = jnp.where(hit, -jnp.inf, sel)
        picks.append(pick)
    top_s = jnp.concatenate(top_s, axis=0)
    gate_ref[...] = top_s / jnp.sum(top_s, axis=0, keepdims=True) * ROUTED_SCALE
    idx_ref[...] = jnp.concatenate(picks, axis=0).astype(jnp.int32)

    chosen = jnp.zeros((N_EXPERTS, n_tok), jnp.float32)
    for pick in picks:
        chosen = chosen + jnp.where(eid == pick, 1.0, 0.0)
    incl = _dot(chosen.astype(jnp.bfloat16), tri_ref[...])
    before = carry_ref[...] + incl - 1.0
    ranks = [jnp.sum(jnp.where(eid == pick, before, 0.0), axis=0, keepdims=True) for pick in picks]
    rank_ref[...] = jnp.concatenate(ranks, axis=0).astype(jnp.int32)
    carry_ref[...] = carry_ref[...] + incl[:, n_tok - 1:n_tok]
    counts_ref[...] = jnp.broadcast_to(carry_ref[...], counts_ref.shape).astype(jnp.int32)


def _router(h1, gffn, w_router, bias, wgs, wus, wds):
    n, d = h1.shape
    t = ROUTE_ROWS
    ff = wgs.shape[1]
    tri = jnp.asarray(np.triu(np.ones((t, t), np.float32)), jnp.bfloat16)
    const = lambda *shape: pl.BlockSpec(shape, lambda i: (0,) * len(shape))
    tok = lambda width: pl.BlockSpec((t, width), lambda i: (i, 0))
    slot = pl.BlockSpec((TOP_K, t), lambda i: (0, i))
    return pl.pallas_call(
        _router_kernel,
        out_shape=(
            jax.ShapeDtypeStruct((n, d), jnp.float32),
            jax.ShapeDtypeStruct((n * ROW_TILE, LANES), jnp.uint32),
            jax.ShapeDtypeStruct((TOP_K, n), jnp.int32),
            jax.ShapeDtypeStruct((TOP_K, n), jnp.float32),
            jax.ShapeDtypeStruct((TOP_K, n), jnp.int32),
            jax.ShapeDtypeStruct((TOKEN_GROUPS, N_EXPERTS, 128), jnp.int32),
        ),
        grid=(n // t,),
        in_specs=[tok(d), const(1, d), const(N_EXPERTS, d), const(N_EXPERTS, 1),
                  const(d, ff), const(d, ff), const(ff, d), const(t, t)],
        out_specs=(tok(d), pl.BlockSpec((t * ROW_TILE, LANES), lambda i: (i, 0)), slot, slot, slot,
                   pl.BlockSpec((None, N_EXPERTS, 128), lambda i: (i // (n // t // TOKEN_GROUPS), 0, 0))),
        scratch_shapes=[pltpu.VMEM((N_EXPERTS, 1), jnp.float32)],
        compiler_params=pltpu.CompilerParams(
            dimension_semantics=("arbitrary",), vmem_limit_bytes=VMEM_LIMIT),
        name="router",
    )(h1, gffn, w_router, bias, wgs, wus, wds, tri)


def _slab(ref, first_sublane):
    if not isinstance(first_sublane, int):
        first_sublane = pl.multiple_of(first_sublane, ROW_TILE)
    return ref.at[pl.ds(first_sublane, ROW_TILE)]


def _start_rows_out(tok, rows_ref, off_ref, dst_hbm, sem):
    for s in range(TOP_K):
        pltpu.make_async_copy(_slab(rows_ref, tok * ROW_TILE), _slab(dst_hbm, off_ref[0, tok * TOP_K + s]),
                              sem).start(priority=s % DMA_QUEUES)


def _wait_rows_out(rows_ref, dst_hbm, sem):
    for s in range(TOP_K):
        pltpu.make_async_copy(rows_ref, dst_hbm.at[pl.ds(0, rows_ref.shape[0])], sem).wait()


def _start_rows_in(tok, n_tok, src_hbm, off_ref, buf_ref, sem):
    for s in range(TOP_K):
        pltpu.make_async_copy(_slab(src_hbm, off_ref[0, tok * TOP_K + s]),
                              _slab(buf_ref, (s * n_tok + tok) * ROW_TILE), sem).start(priority=s % DMA_QUEUES)


def _wait_rows_in(src_hbm, buf_ref, sem):
    pltpu.make_async_copy(src_hbm.at[pl.ds(0, buf_ref.shape[0])], buf_ref, sem).wait()


def _weighted_sum(base_ref, gates_ref, gfin_ref, buf_ref, out_ref):
    n_tok = base_ref.shape[0]
    gates = gates_ref[...]
    width = ROW_TILE * LANES
    lo_chunks, hi_chunks = [], []
    for c in range(ROW_TILE):
        acc_lo = base_ref[:, c * LANES:(c + 1) * LANES]
        acc_hi = base_ref[:, width + c * LANES:width + (c + 1) * LANES]
        for s in range(TOP_K):
            lo, hi = _unpack_words(buf_ref[pl.ds(s * n_tok * ROW_TILE + c, n_tok, stride=ROW_TILE), :])
            acc_lo = acc_lo + gates[:, s:s + 1] * lo
            acc_hi = acc_hi + gates[:, s:s + 1] * hi
        lo_chunks.append(acc_lo)
        hi_chunks.append(acc_hi)
    out_ref[...] = _rms(jnp.concatenate(lo_chunks + hi_chunks, axis=-1), gfin_ref[...])


def _offsets_spec(n_tok, index_map):
    return pl.BlockSpec((None, 1, n_tok * TOP_K), index_map, memory_space=pltpu.SMEM)


def _dispatch_kernel(off_ref, xn_ref, xs_hbm, sem):
    n_tok = xn_ref.shape[0] // ROW_TILE

    def issue(g, carry):
        for u in range(ISSUE_UNROLL):
            _start_rows_out(g * ISSUE_UNROLL + u, xn_ref, off_ref, xs_hbm, sem)
        return carry

    lax.fori_loop(0, n_tok // ISSUE_UNROLL, issue, 0)
    _wait_rows_out(xn_ref, xs_hbm, sem)


def _dispatch(off, xn, n_rows, first_block):
    t = DISPATCH_ROWS
    return pl.pallas_call(
        _dispatch_kernel,
        out_shape=jax.ShapeDtypeStruct((n_rows * ROW_TILE, LANES), xn.dtype),
        grid=(off.shape[0],),
        in_specs=[_offsets_spec(t, lambda i: (i, 0, 0)),
                  pl.BlockSpec((t * ROW_TILE, LANES), lambda i: (first_block + i, 0))],
        out_specs=pl.BlockSpec(memory_space=pl.ANY),
        scratch_shapes=[pltpu.SemaphoreType.DMA(())],
        compiler_params=pltpu.CompilerParams(dimension_semantics=("arbitrary",)),
        name="dispatch",
    )(off, xn)


def _swiglu_block(n_valid, xs_ref, wgb_ref, wub_ref, wdb_ref, y_ref):
    r = EXPERT_ROWS
    keep = lax.broadcasted_iota(jnp.int32, (r, LANES), 0) < n_valid
    lo, hi = _load_row_tiles(xs_ref, r)
    xb = jnp.concatenate([jnp.where(keep, c, 0.0).astype(jnp.bfloat16) for c in lo + hi], axis=-1)
    g = _dot(xb, wgb_ref[...])
    u = _dot(xb, wub_ref[...])
    g_sig, _ = _sigmoid_pair(g)
    hid = ((g * g_sig) * u).astype(jnp.bfloat16)
    _store_row_tiles(y_ref, _pack_rows(_dot(hid, wdb_ref[...])))


def _refresh_weights(i, blk_e_ref, wg_ref, wu_ref, wd_ref, wgb_ref, wub_ref, wdb_ref):
    @pl.when((i == 0) | (blk_e_ref[i] != blk_e_ref[jnp.maximum(i - 1, 0)]))
    def _():
        wgb_ref[...] = wg_ref[...].astype(jnp.bfloat16)
        wub_ref[...] = wu_ref[...].astype(jnp.bfloat16)
        wdb_ref[...] = wd_ref[...].astype(jnp.bfloat16)


def _experts_out_kernel(side_steps, blk_e_ref, blk_rows_ref, xs_ref, wg_ref, wu_ref, wd_ref, off_ref, rows_ref,
                        y_ref, dst_hbm, wgb_ref, wub_ref, wdb_ref, sem):
    i = pl.program_id(0)
    n_valid = blk_rows_ref[i]
    side = i < side_steps
    _refresh_weights(i, blk_e_ref, wg_ref, wu_ref, wd_ref, wgb_ref, wub_ref, wdb_ref)

    def send():
        for tok in range(SIDE_ROWS):
            _start_rows_out(tok, rows_ref, off_ref, dst_hbm, sem)

    @pl.when((n_valid > 0) & side)
    def _():
        send()
        _swiglu_block(n_valid, xs_ref, wgb_ref, wub_ref, wdb_ref, y_ref)
        _wait_rows_out(rows_ref, dst_hbm, sem)

    @pl.when((n_valid > 0) & jnp.logical_not(side))
    def _():
        _swiglu_block(n_valid, xs_ref, wgb_ref, wub_ref, wdb_ref, y_ref)

    @pl.when((n_valid == 0) & side)
    def _():
        send()
        y_ref[...] = jnp.zeros_like(y_ref)
        _wait_rows_out(rows_ref, dst_hbm, sem)

    @pl.when((n_valid == 0) & jnp.logical_not(side))
    def _():
        y_ref[...] = jnp.zeros_like(y_ref)


def _experts_in_kernel(side_steps, blk_e_ref, blk_rows_ref, xs_ref, wg_ref, wu_ref, wd_ref, off_ref, off_nx_ref,
                       base_ref, gates_ref, gfin_ref, src_hbm, y_ref, out_ref,
                       wgb_ref, wub_ref, wdb_ref, buf_ref, sem):
    i = pl.program_id(0)
    n_valid = blk_rows_ref[i]
    _refresh_weights(i, blk_e_ref, wg_ref, wu_ref, wd_ref, wgb_ref, wub_ref, wdb_ref)

    def fetch(off_r):
        for tok in range(SIDE_ROWS):
            _start_rows_in(tok, SIDE_ROWS, src_hbm, off_r, buf_ref, sem)

    @pl.when(i == 0)
    def _():
        fetch(off_ref)

    @pl.when(i < side_steps)
    def _():
        _wait_rows_in(src_hbm, buf_ref, sem)
        _weighted_sum(base_ref, gates_ref, gfin_ref, buf_ref, out_ref)

    more = i + 1 < side_steps

    @pl.when((n_valid > 0) & more)
    def _():
        fetch(off_nx_ref)
        _swiglu_block(n_valid, xs_ref, wgb_ref, wub_ref, wdb_ref, y_ref)

    @pl.when((n_valid > 0) & jnp.logical_not(more))
    def _():
        _swiglu_block(n_valid, xs_ref, wgb_ref, wub_ref, wdb_ref, y_ref)

    @pl.when((n_valid == 0) & more)
    def _():
        fetch(off_nx_ref)
        y_ref[...] = jnp.zeros_like(y_ref)

    @pl.when((n_valid == 0) & jnp.logical_not(more))
    def _():
        y_ref[...] = jnp.zeros_like(y_ref)


def _expert_specs(d, ff):
    r = EXPERT_ROWS
    rows = pl.BlockSpec((r * ROW_TILE, LANES), lambda i, be, br: (i, 0))
    weights = [pl.BlockSpec((None, d, ff), lambda i, be, br: (be[i], 0, 0)),
               pl.BlockSpec((None, d, ff), lambda i, be, br: (be[i], 0, 0)),
               pl.BlockSpec((None, ff, d), lambda i, be, br: (be[i], 0, 0))]
    scratch = [pltpu.VMEM((d, ff), jnp.bfloat16), pltpu.VMEM((d, ff), jnp.bfloat16),
               pltpu.VMEM((ff, d), jnp.bfloat16)]
    return rows, weights, scratch


def _experts_out(blk_e, blk_rows, xs, wg, wu, wd, off, xn, first_block, n_rows_next):
    side_steps = off.shape[0]
    d, ff = wg.shape[1], wg.shape[2]
    rows, weights, scratch = _expert_specs(d, ff)
    side = lambda i, be, br: jnp.minimum(i, side_steps - 1)
    return pl.pallas_call(
        functools.partial(_experts_out_kernel, side_steps),
        out_shape=(jax.ShapeDtypeStruct(xs.shape, xs.dtype),
                   jax.ShapeDtypeStruct((n_rows_next * ROW_TILE, LANES), xs.dtype)),
        grid_spec=pltpu.PrefetchScalarGridSpec(
            num_scalar_prefetch=2,
            grid=(xs.shape[0] // (EXPERT_ROWS * ROW_TILE),),
            in_specs=[rows] + weights + [
                _offsets_spec(SIDE_ROWS, lambda i, be, br: (side(i, be, br), 0, 0)),
                pl.BlockSpec((SIDE_ROWS * ROW_TILE, LANES), lambda i, be, br: (first_block + side(i, be, br), 0))],
            out_specs=(rows, pl.BlockSpec(memory_space=pl.ANY)),
            scratch_shapes=scratch + [pltpu.SemaphoreType.DMA(())],
        ),
        compiler_params=pltpu.CompilerParams(
            dimension_semantics=("arbitrary",), vmem_limit_bytes=VMEM_LIMIT),
        name="experts_send",
    )(blk_e, blk_rows, xs, wg, wu, wd, off, xn)


def _experts_in(blk_e, blk_rows, xs, wg, wu, wd, off, base, gates, gfin, y_prev):
    side_steps = off.shape[0]
    d, ff = wg.shape[1], wg.shape[2]
    n = base.shape[0]
    rows, weights, scratch = _expert_specs(d, ff)
    side = lambda i, be, br: jnp.minimum(i, side_steps - 1)
    nxt = lambda i, be, br: jnp.minimum(i + 1, side_steps - 1)
    return pl.pallas_call(
        functools.partial(_experts_in_kernel, side_steps),
        out_shape=(jax.ShapeDtypeStruct(xs.shape, xs.dtype), jax.ShapeDtypeStruct((n, d), jnp.float32)),
        grid_spec=pltpu.PrefetchScalarGridSpec(
            num_scalar_prefetch=2,
            grid=(xs.shape[0] // (EXPERT_ROWS * ROW_TILE),),
            in_specs=[rows] + weights + [
                _offsets_spec(SIDE_ROWS, lambda i, be, br: (side(i, be, br), 0, 0)),
                _offsets_spec(SIDE_ROWS, lambda i, be, br: (nxt(i, be, br), 0, 0)),
                pl.BlockSpec((SIDE_ROWS, d), lambda i, be, br: (side(i, be, br), 0)),
                pl.BlockSpec((SIDE_ROWS, TOP_K), lambda i, be, br: (side(i, be, br), 0)),
                pl.BlockSpec((1, d), lambda i, be, br: (0, 0)),
                pl.BlockSpec(memory_space=pl.ANY)],
            out_specs=(rows, pl.BlockSpec((SIDE_ROWS, d), lambda i, be, br: (side(i, be, br), 0))),
            scratch_shapes=scratch + [pltpu.VMEM((TOP_K * SIDE_ROWS * ROW_TILE, LANES), xs.dtype),
                                      pltpu.SemaphoreType.DMA(())],
        ),
        compiler_params=pltpu.CompilerParams(
            dimension_semantics=("arbitrary",), vmem_limit_bytes=VMEM_LIMIT),
        name="experts_fetch",
    )(blk_e, blk_rows, xs, wg, wu, wd, off, off, base, gates, gfin, y_prev)


def _combine_kernel(off_ref, off_nx_ref, base_ref, gates_ref, gfin_ref, y_hbm, out_in_ref, out_ref, buf_ref, sem):
    del out_in_ref
    i = pl.program_id(0)
    n_steps = pl.num_programs(0)
    n_tok = base_ref.shape[0]

    def issue(off_r, half):
        def body(g, carry):
            for u in range(ISSUE_UNROLL):
                _start_rows_in(g * ISSUE_UNROLL + u, n_tok, y_hbm, off_r, buf_ref.at[half], sem.at[half])
            return carry
        lax.fori_loop(0, n_tok // ISSUE_UNROLL, body, 0)

    half = i % 2

    @pl.when(i == 0)
    def _():
        issue(off_ref, 0)

    @pl.when(i + 1 < n_steps)
    def _():
        issue(off_nx_ref, 1 - half)

    _wait_rows_in(y_hbm, buf_ref.at[half], sem.at[half])
    _weighted_sum(base_ref, gates_ref, gfin_ref, buf_ref.at[half], out_ref)


def _combine(off, base, gates, gfin, y, out_prev, first_block):
    n, d = base.shape
    t = COMBINE_ROWS
    n_steps = off.shape[0]
    tok = lambda width: pl.BlockSpec((t, width), lambda i: (first_block + i, 0))
    return pl.pallas_call(
        _combine_kernel,
        out_shape=jax.ShapeDtypeStruct((n, d), jnp.float32),
        grid=(n_steps,),
        in_specs=[_offsets_spec(t, lambda i: (i, 0, 0)),
                  _offsets_spec(t, lambda i: (jnp.minimum(i + 1, n_steps - 1), 0, 0)),
                  tok(d), tok(TOP_K), pl.BlockSpec((1, d), lambda i: (0, 0)),
                  pl.BlockSpec(memory_space=pl.ANY), pl.BlockSpec(memory_space=pl.ANY)],
        out_specs=tok(d),
        scratch_shapes=[pltpu.VMEM((2, TOP_K * t * ROW_TILE, LANES), y.dtype),
                        pltpu.SemaphoreType.DMA((2,))],
        input_output_aliases={6: 0},
        compiler_params=pltpu.CompilerParams(
            dimension_semantics=("arbitrary",), vmem_limit_bytes=VMEM_LIMIT),
        name="combine",
    )(off, off, base, gates, gfin, y, out_prev)


def _group_layout(counts, idx, rank):
    r = EXPERT_ROWS
    n_g = idx.shape[1]
    n_blocks = (n_g * TOP_K) // r + N_EXPERTS
    padded = (counts + r - 1) // r * r
    ends = jnp.cumsum(padded)
    starts = (ends - padded).astype(jnp.int32)
    blk_row0 = jnp.arange(n_blocks, dtype=jnp.int32) * r
    blk_e = jnp.minimum(jnp.sum((ends[None, :] <= blk_row0[:, None]).astype(jnp.int32), axis=1), N_EXPERTS - 1)
    onehot = (blk_e[:, None] == jnp.arange(N_EXPERTS, dtype=jnp.int32)[None, :]).astype(jnp.int32)
    blk_rows = jnp.clip(onehot @ counts - (blk_row0 - onehot @ starts), 0, r).astype(jnp.int32)
    dest = rank + jnp.sum(jnp.where(idx[None] == jnp.arange(N_EXPERTS, dtype=jnp.int32)[:, None, None],
                                    starts[:, None, None], 0), axis=0)
    return blk_e, blk_rows, (dest * ROW_TILE).T.reshape(-1), n_blocks * r


def kernel(x, meta_tokens, norm_mix_g, w_in, lb_table, hgrn_norm_g, conv_w, conv_norm_g, w_out,
           norm_ffn_g, w_router, router_bias, w_gate_e, w_up_e, w_down_e, w_gate_s, w_up_s, w_down_s,
           norm_final_g):
    bsz, seq, d = x.shape
    n = bsz * seq
    n_g = n // TOKEN_GROUPS
    bf = jnp.bfloat16
    assert TOKEN_GROUPS == 2
    assert seq % MIX_ROWS == 0 and MIX_ROWS % CHUNK == 0
    assert n_g % ROUTE_ROWS == 0 and n_g % DISPATCH_ROWS == 0 and n_g % COMBINE_ROWS == 0
    assert (n_g * TOP_K) % EXPERT_ROWS == 0 and n_g % SIDE_ROWS == 0
    assert n_g // SIDE_ROWS <= (n_g * TOP_K) // EXPERT_ROWS

    meta_pad = jnp.zeros((CHUNK, d), jnp.float32).at[CHUNK - N_META:].set(meta_tokens)
    h1 = _mixer(x, meta_pad, norm_mix_g[0:1], w_in[0].astype(bf), lb_table, hgrn_norm_g[0:1],
                conv_w[0], conv_norm_g[0:1], w_out[0].astype(bf))

    base, xn, idx, gate, rank, counts = _router(
        h1.reshape(n, d), norm_ffn_g[0:1], w_router[0].T, router_bias[0][:, None],
        w_gate_s[0].astype(bf), w_up_s[0].astype(bf), w_down_s[0].astype(bf))

    lay = [_group_layout(counts[g, :, 0], idx[:, g * n_g:(g + 1) * n_g], rank[:, g * n_g:(g + 1) * n_g])
           for g in range(TOKEN_GROUPS)]
    (be0, br0, off0, rows0), (be1, br1, off1, rows1) = lay
    steps = lambda off, t: off.reshape(n_g // t, 1, t * TOP_K)
    gates = gate.T
    gfin = norm_final_g[None, :]
    wg, wu, wd = w_gate_e[0], w_up_e[0], w_down_e[0]

    xs0 = _dispatch(steps(off0, DISPATCH_ROWS), xn, rows0, 0)
    y0, xs1 = _experts_out(be0, br0, xs0, wg, wu, wd, steps(off1, SIDE_ROWS), xn, n_g // SIDE_ROWS, rows1)
    y1, out = _experts_in(be1, br1, xs1, wg, wu, wd, steps(off0, SIDE_ROWS), base, gates, gfin, y0)
    out = _combine(steps(off1, COMBINE_ROWS), base, gates, gfin, y1, out, n_g // COMBINE_ROWS)
    return out.reshape(bsz, seq, d)
```

```python
import functools

import numpy as np
import jax
import jax.numpy as jnp
from jax import lax
from jax.experimental import pallas as pl
from jax.experimental.pallas import tpu as pltpu

N_META = 16
CHUNK = 128
HEADS = 4
HEAD_DIM = 128
HGRN_W = HEADS * HEAD_DIM
CONV_W = 512
CONV_GROUPS = 4
CONV_K = 3
N_EXPERTS = 64
TOP_K = 8
ROUTED_SCALE = 2.5
EPS = 1e-6

V7X_VMEM_BYTES = 64 * 1024 * 1024
VMEM_LIMIT = V7X_VMEM_BYTES - 8 * 1024 * 1024

MIX_ROWS = 512
CHUNK_UNROLL = 4
ROUTE_ROWS = 512
DISPATCH_ROWS = 512
EXPERT_ROWS = 1024
COMBINE_ROWS = 256
DMA_QUEUES = 2
ISSUE_UNROLL = 8
TOKEN_GROUPS = 2
SIDE_ROWS = 128

LANES = 128
ROW_TILE = 4

HALF_SPANS = (64, 32, 16, 8, 4, 2, 1)
N_LEVELS = len(HALF_SPANS) + 1


def _decay_sum_matrix():
    a = np.zeros((N_LEVELS, CHUNK, CHUNK), np.float32)
    a[0] = np.tril(np.ones((CHUNK, CHUNK), np.float32))
    for i, m in enumerate(HALF_SPANS):
        for t in range(CHUNK):
            mid = (t // (2 * m)) * 2 * m + m
            if t >= mid:
                a[1 + i, t, mid:t + 1] = 1.0
            else:
                a[1 + i, t, t + 1:mid] = 1.0
    return a.reshape(N_LEVELS * CHUNK, CHUNK)


def _level_matrix():
    lv = np.full((CHUNK, CHUNK), -1, np.int32)
    for t in range(CHUNK):
        lv[t, t] = len(HALF_SPANS)
        for s in range(t):
            top = (t ^ s).bit_length() - 1
            lv[t, s] = HALF_SPANS.index(1 << top)
    return lv


def _rms(x, g):
    return x * lax.rsqrt(jnp.mean(x * x, axis=-1, keepdims=True) + EPS) * g


def _group_rms(x, g, width):
    outs = []
    for j in range(x.shape[-1] // width):
        xs = x[:, j * width:(j + 1) * width]
        outs.append(xs * lax.rsqrt(jnp.mean(xs * xs, axis=-1, keepdims=True) + EPS))
    return jnp.concatenate(outs, axis=-1) * g


def _sigmoid_pair(z):
    t = jnp.exp(-jnp.abs(z))
    inv = 1.0 / (1.0 + t)
    big, small = inv, t * inv
    pos = z >= 0
    return jnp.where(pos, big, small), jnp.where(pos, small, big)


def _pack_rows(x):
    half = x.shape[1] // 2
    bits = lambda v: lax.bitcast_convert_type(v.astype(jnp.bfloat16).astype(jnp.float32), jnp.uint32)
    return (bits(x[:, :half]) >> 16) | (bits(x[:, half:]) & jnp.uint32(0xFFFF0000))


def _unpack_words(w):
    lo = lax.bitcast_convert_type(w << 16, jnp.float32)
    hi = lax.bitcast_convert_type(w & jnp.uint32(0xFFFF0000), jnp.float32)
    return lo, hi


def _store_row_tiles(ref, words):
    t = words.shape[0]
    for c in range(ROW_TILE):
        ref[pl.ds(c, t, stride=ROW_TILE), :] = words[:, c * LANES:(c + 1) * LANES]


def _load_row_tiles(ref, t):
    parts = [_unpack_words(ref[pl.ds(c, t, stride=ROW_TILE), :]) for c in range(ROW_TILE)]
    return [p[0] for p in parts], [p[1] for p in parts]


def _dot(a, b):
    return jnp.dot(a, b, preferred_element_type=jnp.float32)


def _dot_nt(a, b):
    return lax.dot_general(a, b, (((1,), (1,)), ((), ())), preferred_element_type=jnp.float32)


def _dot_tn(a, b):
    return lax.dot_general(a, b, (((0,), (0,)), ((), ())), preferred_element_type=jnp.float32)


def _hgrn_chunk(q, z, iv, lb, amat, level, st_ref, first_valid_row):
    sig, sig_neg = _sigmoid_pair(z)
    lf = jnp.log(lb + (1.0 - lb) * sig)
    k = (1.0 - lb) * sig_neg
    row = lax.broadcasted_iota(jnp.int32, (CHUNK, HGRN_W), 0)
    if first_valid_row:
        valid = row >= first_valid_row
        lf = jnp.where(valid, lf, 0.0)
        k = jnp.where(valid, k, 0.0)

    h1 = lf.astype(jnp.bfloat16)
    h2 = (lf - h1.astype(jnp.float32)).astype(jnp.bfloat16)
    e_all = _dot(amat, jnp.concatenate([h1, h2], axis=0))

    b = e_all[0:CHUNK]
    b_last = b[CHUNK - 1:CHUNK]
    q_in = (q * jnp.exp(b)).astype(jnp.bfloat16)
    k_out = (k * jnp.exp(b_last - b)).astype(jnp.bfloat16)
    st_decay = jnp.exp(b_last)
    v_bf = iv.astype(jnp.bfloat16)

    q_lv = [q.astype(jnp.bfloat16)]
    k_lv = [k.astype(jnp.bfloat16)]
    for i, m in enumerate(HALF_SPANS):
        ex = jnp.exp(e_all[(1 + i) * CHUNK:(2 + i) * CHUNK])
        right = (row & m) != 0
        q_lv.append(jnp.where(right, q * ex, 0.0).astype(jnp.bfloat16))
        k_lv.append(jnp.where(right, 0.0, k * ex).astype(jnp.bfloat16))
    lv_of = [len(HALF_SPANS)] + list(range(len(HALF_SPANS)))

    outs = []
    for h in range(HEADS):
        cols = slice(h * HEAD_DIM, (h + 1) * HEAD_DIM)
        scores = jnp.zeros((CHUNK, CHUNK), jnp.float32)
        for ql, kl, lv in zip(q_lv, k_lv, lv_of):
            scores = jnp.where(level == lv, _dot_nt(ql[:, cols], kl[:, cols]), scores)
        st = st_ref[h]
        o = _dot(scores.astype(jnp.bfloat16), v_bf[:, cols]) + _dot_nt(q_in[:, cols], st.astype(jnp.bfloat16))
        st_ref[h] = st * st_decay[:, cols] + _dot_tn(v_bf[:, cols], k_out[:, cols])
        outs.append(o)
    return jnp.concatenate(outs, axis=-1)


def _mixer_kernel(x_ref, meta_ref, gmix_ref, win_ref, lbt_ref, ghg_ref, cw_ref, gcv_ref, wout_ref,
                  amat_ref, level_ref, h1_ref, proj_ref, o_ref, u_ref, st_ref):
    j = pl.program_id(1)
    rows = x_ref.shape[0]
    n_in = win_ref.shape[1]

    lbt = lbt_ref[...]
    lbe = jnp.exp(lbt - jnp.max(lbt, axis=0, keepdims=True))
    lb = lbe[0:1] / jnp.sum(lbe, axis=0, keepdims=True)

    amat = amat_ref[...]
    level = level_ref[...]
    gmix = gmix_ref[...]

    def project(xv, dst_rows):
        xn = _rms(xv, gmix).astype(jnp.bfloat16)
        for c0 in range(0, n_in, 512):
            proj_ref[dst_rows, c0:c0 + 512] = _dot(xn, win_ref[:, c0:c0 + 512])

    @pl.when(j == 0)
    def _():
        st_ref[...] = jnp.zeros_like(st_ref)
        project(meta_ref[...], pl.ds(0, CHUNK))
        pm = proj_ref[0:CHUNK, :]
        _hgrn_chunk(pm[:, 0:512], pm[:, 512:1024], pm[:, 1024:1536], lb, amat, level, st_ref,
                    CHUNK - N_META)
        u_ref[0:8, :] = (pm[:, 2560:3072] * pm[:, 3072:3584])[CHUNK - 8:CHUNK]

    project(x_ref[...], pl.ds(0, rows))

    def chunk_body(c, carry):
        for u in range(CHUNK_UNROLL):
            r0 = pl.multiple_of((c * CHUNK_UNROLL + u) * CHUNK, CHUNK)
            q = proj_ref[pl.ds(r0, CHUNK), 0:512]
            z = proj_ref[pl.ds(r0, CHUNK), 512:1024]
            iv = proj_ref[pl.ds(r0, CHUNK), 1024:1536]
            o_ref[pl.ds(r0, CHUNK), :] = _hgrn_chunk(q, z, iv, lb, amat, level, st_ref, 0)
        return carry

    lax.fori_loop(0, rows // (CHUNK * CHUNK_UNROLL), chunk_body, 0)

    g_out = proj_ref[:, 1536:2048]
    g_sig, _ = _sigmoid_pair(g_out)
    y_hgrn = _group_rms(o_ref[...], ghg_ref[...], HEAD_DIM) * (g_out * g_sig)

    u = proj_ref[:, 2560:3072] * proj_ref[:, 3072:3584]
    u_ref[8:8 + rows, :] = u
    cw = cw_ref[...]
    y = cw[2:3] * u + cw[1:2] * u_ref[7:7 + rows, :] + cw[0:1] * u_ref[6:6 + rows, :]
    u_ref[0:8, :] = u[rows - 8:rows]
    y_conv = _group_rms(proj_ref[:, 2048:2560] * y, gcv_ref[...], CONV_W // CONV_GROUPS)

    mixed = jnp.concatenate([y_hgrn, y_conv], axis=-1).astype(jnp.bfloat16)
    h1_ref[...] = x_ref[...] + _dot(mixed, wout_ref[...])


def _mixer(x, meta_pad, gmix, w_in, lb_table, ghg, conv_w, gcv, w_out):
    bsz, seq, d = x.shape
    n_in = w_in.shape[1]
    rows = MIX_ROWS
    const = lambda *shape: pl.BlockSpec(shape, lambda b, j: (0,) * len(shape))
    return pl.pallas_call(
        _mixer_kernel,
        out_shape=jax.ShapeDtypeStruct((bsz, seq, d), jnp.float32),
        grid=(bsz, seq // rows),
        in_specs=[
            pl.BlockSpec((None, rows, d), lambda b, j: (b, j, 0)),
            const(CHUNK, d), const(1, d), const(d, n_in), const(*lb_table.shape), const(1, HGRN_W),
            const(CONV_K, CONV_W), const(1, CONV_W), const(d, d),
            const(N_LEVELS * CHUNK, 2 * CHUNK), const(CHUNK, CHUNK),
        ],
        out_specs=pl.BlockSpec((None, rows, d), lambda b, j: (b, j, 0)),
        scratch_shapes=[
            pltpu.VMEM((rows, n_in), jnp.float32),
            pltpu.VMEM((rows, HGRN_W), jnp.float32),
            pltpu.VMEM((rows + 8, CONV_W), jnp.float32),
            pltpu.VMEM((HEADS, HEAD_DIM, HEAD_DIM), jnp.float32),
        ],
        compiler_params=pltpu.CompilerParams(
            dimension_semantics=("arbitrary", "arbitrary"), vmem_limit_bytes=VMEM_LIMIT),
        name="mixer",
    )(x, meta_pad, gmix, w_in, lb_table, ghg, conv_w, gcv, w_out,
      jnp.asarray(np.tile(_decay_sum_matrix(), (1, 2)), jnp.bfloat16), jnp.asarray(_level_matrix()))


def _router_kernel(h1_ref, gffn_ref, wr_ref, bias_ref, wgs_ref, wus_ref, wds_ref, tri_ref,
                   base_ref, xn_ref, idx_ref, gate_ref, rank_ref, counts_ref, carry_ref):
    i = pl.program_id(0)
    n_tok = h1_ref.shape[0]

    @pl.when(i % (pl.num_programs(0) // TOKEN_GROUPS) == 0)
    def _():
        carry_ref[...] = jnp.zeros_like(carry_ref)

    h1 = h1_ref[...]
    xn = _rms(h1, gffn_ref[...])
    _store_row_tiles(xn_ref, _pack_rows(xn))
    xb = xn.astype(jnp.bfloat16)

    g_pre = _dot(xb, wgs_ref[...])
    gate_s, _ = _sigmoid_pair(g_pre)
    hid = (g_pre * gate_s) * _dot(xb, wus_ref[...])
    base_ref[...] = h1 + _dot(hid.astype(jnp.bfloat16), wds_ref[...])

    logits = lax.dot_general(wr_ref[...], xn, (((1,), (1,)), ((), ())),
                             precision=lax.Precision.HIGHEST, preferred_element_type=jnp.float32)
    scores, _ = _sigmoid_pair(logits)
    sel = scores + bias_ref[...]
    eid = lax.broadcasted_iota(jnp.int32, (N_EXPERTS, n_tok), 0).astype(jnp.float32)
    picks, top_s = [], []
    for _ in range(TOP_K):
        best = jnp.max(sel, axis=0, keepdims=True)
        pick = jnp.min(jnp.where(sel == best, eid, float(N_EXPERTS)), axis=0, keepdims=True)
        hit = eid == pick
        top_s.append(jnp.sum(jnp.where(hit, scores, 0.0), axis=0, keepdims=True))
        sel = jnp.where(hit, -jnp.inf, sel)
        picks.append(pick)
    top_s = jnp.concatenate(top_s, axis=0)
    gate_ref[...] = top_s / jnp.sum(top_s, axis=0, keepdims=True) * ROUTED_SCALE
    idx_ref[...] = jnp.concatenate(picks, axis=0).astype(jnp.int32)

    chosen = jnp.zeros((N_EXPERTS, n_tok), jnp.float32)
    for pick in picks:
        chosen = chosen + jnp.where(eid == pick, 1.0, 0.0)
    incl = _dot(chosen.astype(jnp.bfloat16), tri_ref[...])
    before = carry_ref[...] + incl - 1.0
    ranks = [jnp.sum(jnp.where(eid == pick, before, 0.0), axis=0, keepdims=True) for pick in picks]
    rank_ref[...] = jnp.concatenate(ranks, axis=0).astype(jnp.int32)
    carry_ref[...] = carry_ref[...] + incl[:, n_tok - 1:n_tok]
    counts_ref[...] = jnp.broadcast_to(carry_ref[...], counts_ref.shape).astype(jnp.int32)


def _router(h1, gffn, w_router, bias, wgs, wus, wds):
    n, d = h1.shape
    t = ROUTE_ROWS
    ff = wgs.shape[1]
    tri = jnp.asarray(np.triu(np.ones((t, t), np.float32)), jnp.bfloat16)
    const = lambda *shape: pl.BlockSpec(shape, lambda i: (0,) * len(shape))
    tok = lambda width: pl.BlockSpec((t, width), lambda i: (i, 0))
    slot = pl.BlockSpec((TOP_K, t), lambda i: (0, i))
    return pl.pallas_call(
        _router_kernel,
        out_shape=(
            jax.ShapeDtypeStruct((n, d), jnp.float32),
            jax.ShapeDtypeStruct((n * ROW_TILE, LANES), jnp.uint32),
            jax.ShapeDtypeStruct((TOP_K, n), jnp.int32),
            jax.ShapeDtypeStruct((TOP_K, n), jnp.float32),
            jax.ShapeDtypeStruct((TOP_K, n), jnp.int32),
            jax.ShapeDtypeStruct((TOKEN_GROUPS, N_EXPERTS, 128), jnp.int32),
        ),
        grid=(n // t,),
        in_specs=[tok(d), const(1, d), const(N_EXPERTS, d), const(N_EXPERTS, 1),
                  const(d, ff), const(d, ff), const(ff, d), const(t, t)],
        out_specs=(tok(d), pl.BlockSpec((t * ROW_TILE, LANES), lambda i: (i, 0)), slot, slot, slot,
                   pl.BlockSpec((None, N_EXPERTS, 128), lambda i: (i // (n // t // TOKEN_GROUPS), 0, 0))),
        scratch_shapes=[pltpu.VMEM((N_EXPERTS, 1), jnp.float32)],
        compiler_params=pltpu.CompilerParams(
            dimension_semantics=("arbitrary",), vmem_limit_bytes=VMEM_LIMIT),
        name="router",
    )(h1, gffn, w_router, bias, wgs, wus, wds, tri)


def _slab(ref, first_sublane):
    if not isinstance(first_sublane, int):
        first_sublane = pl.multiple_of(first_sublane, ROW_TILE)
    return ref.at[pl.ds(first_sublane, ROW_TILE)]


def _start_rows_out(tok, rows_ref, off_ref, dst_hbm, sem):
    for s in range(TOP_K):
        pltpu.make_async_copy(_slab(rows_ref, tok * ROW_TILE), _slab(dst_hbm, off_ref[0, tok * TOP_K + s]),
                              sem).start(priority=s % DMA_QUEUES)


def _wait_rows_out(rows_ref, dst_hbm, sem):
    for s in range(TOP_K):
        pltpu.make_async_copy(rows_ref, dst_hbm.at[pl.ds(0, rows_ref.shape[0])], sem).wait()


def _start_rows_in(tok, n_tok, src_hbm, off_ref, buf_ref, sem):
    for s in range(TOP_K):
        pltpu.make_async_copy(_slab(src_hbm, off_ref[0, tok * TOP_K + s]),
                              _slab(buf_ref, (s * n_tok + tok) * ROW_TILE), sem).start(priority=s % DMA_QUEUES)


def _wait_rows_in(src_hbm, buf_ref, sem):
    pltpu.make_async_copy(src_hbm.at[pl.ds(0, buf_ref.shape[0])], buf_ref, sem).wait()


def _weighted_sum(base_ref, gates_ref, gfin_ref, buf_ref, out_ref):
    n_tok = base_ref.shape[0]
    gates = gates_ref[...]
    width = ROW_TILE * LANES
    lo_chunks, hi_chunks = [], []
    for c in range(ROW_TILE):
        acc_lo = base_ref[:, c * LANES:(c + 1) * LANES]
        acc_hi = base_ref[:, width + c * LANES:width + (c + 1) * LANES]
        for s in range(TOP_K):
            lo, hi = _unpack_words(buf_ref[pl.ds(s * n_tok * ROW_TILE + c, n_tok, stride=ROW_TILE), :])
            acc_lo = acc_lo + gates[:, s:s + 1] * lo
            acc_hi = acc_hi + gates[:, s:s + 1] * hi
        lo_chunks.append(acc_lo)
        hi_chunks.append(acc_hi)
    out_ref[...] = _rms(jnp.concatenate(lo_chunks + hi_chunks, axis=-1), gfin_ref[...])


def _offsets_spec(n_tok, index_map):
    return pl.BlockSpec((None, 1, n_tok * TOP_K), index_map, memory_space=pltpu.SMEM)


def _dispatch_kernel(off_ref, xn_ref, xs_hbm, sem):
    n_tok = xn_ref.shape[0] // ROW_TILE

    def issue(g, carry):
        for u in range(ISSUE_UNROLL):
            _start_rows_out(g * ISSUE_UNROLL + u, xn_ref, off_ref, xs_hbm, sem)
        return carry

    lax.fori_loop(0, n_tok // ISSUE_UNROLL, issue, 0)
    _wait_rows_out(xn_ref, xs_hbm, sem)


def _dispatch(off, xn, n_rows, first_block):
    t = DISPATCH_ROWS
    return pl.pallas_call(
        _dispatch_kernel,
        out_shape=jax.ShapeDtypeStruct((n_rows * ROW_TILE, LANES), xn.dtype),
        grid=(off.shape[0],),
        in_specs=[_offsets_spec(t, lambda i: (i, 0, 0)),
                  pl.BlockSpec((t * ROW_TILE, LANES), lambda i: (first_block + i, 0))],
        out_specs=pl.BlockSpec(memory_space=pl.ANY),
        scratch_shapes=[pltpu.SemaphoreType.DMA(())],
        compiler_params=pltpu.CompilerParams(dimension_semantics=("arbitrary",)),
        name="dispatch",
    )(off, xn)


def _swiglu_block(n_valid, xs_ref, wgb_ref, wub_ref, wdb_ref, y_ref):
    r = EXPERT_ROWS
    keep = lax.broadcasted_iota(jnp.int32, (r, LANES), 0) < n_valid
    lo, hi = _load_row_tiles(xs_ref, r)
    xb = jnp.concatenate([jnp.where(keep, c, 0.0).astype(jnp.bfloat16) for c in lo + hi], axis=-1)
    g = _dot(xb, wgb_ref[...])
    u = _dot(xb, wub_ref[...])
    g_sig, _ = _sigmoid_pair(g)
    hid = ((g * g_sig) * u).astype(jnp.bfloat16)
    _store_row_tiles(y_ref, _pack_rows(_dot(hid, wdb_ref[...])))


def _refresh_weights(i, blk_e_ref, wg_ref, wu_ref, wd_ref, wgb_ref, wub_ref, wdb_ref):
    @pl.when((i == 0) | (blk_e_ref[i] != blk_e_ref[jnp.maximum(i - 1, 0)]))
    def _():
        wgb_ref[...] = wg_ref[...].astype(jnp.bfloat16)
        wub_ref[...] = wu_ref[...].astype(jnp.bfloat16)
        wdb_ref[...] = wd_ref[...].astype(jnp.bfloat16)


def _experts_out_kernel(side_steps, blk_e_ref, blk_rows_ref, xs_ref, wg_ref, wu_ref, wd_ref, off_ref, rows_ref,
                        y_ref, dst_hbm, wgb_ref, wub_ref, wdb_ref, stage_ref, sem):
    i = pl.program_id(0)
    n_valid = blk_rows_ref[i]
    side = i < side_steps
    slot = i % 2
    _refresh_weights(i, blk_e_ref, wg_ref, wu_ref, wd_ref, wgb_ref, wub_ref, wdb_ref)

    def send():
        stage_ref[slot] = rows_ref[...]
        for tok in range(SIDE_ROWS):
            _start_rows_out(tok, stage_ref.at[slot], off_ref, dst_hbm, sem.at[slot])

    @pl.when((n_valid > 0) & side)
    def _():
        send()
        _swiglu_block(n_valid, xs_ref, wgb_ref, wub_ref, wdb_ref, y_ref)

    @pl.when((n_valid > 0) & jnp.logical_not(side))
    def _():
        _swiglu_block(n_valid, xs_ref, wgb_ref, wub_ref, wdb_ref, y_ref)

    @pl.when((n_valid == 0) & side)
    def _():
        send()
        y_ref[...] = jnp.zeros_like(y_ref)

    @pl.when((n_valid == 0) & jnp.logical_not(side))
    def _():
        y_ref[...] = jnp.zeros_like(y_ref)

    @pl.when((i >= 1) & (i <= side_steps))
    def _():
        _wait_rows_out(stage_ref.at[1 - slot], dst_hbm, sem.at[1 - slot])


def _experts_in_kernel(side_steps, blk_e_ref, blk_rows_ref, xs_ref, wg_ref, wu_ref, wd_ref,
                       off_ref, off_nx_ref, off_nx2_ref, base_ref, gates_ref, gfin_ref, src_hbm, y_ref, out_ref,
                       wgb_ref, wub_ref, wdb_ref, buf_ref, sem):
    i = pl.program_id(0)
    n_valid = blk_rows_ref[i]
    slot = i % 2
    _refresh_weights(i, blk_e_ref, wg_ref, wu_ref, wd_ref, wgb_ref, wub_ref, wdb_ref)

    def fetch(off_r, into):
        for tok in range(SIDE_ROWS):
            _start_rows_in(tok, SIDE_ROWS, src_hbm, off_r, buf_ref.at[into], sem.at[into])

    @pl.when(i == 0)
    def _():
        fetch(off_ref, 0)
        fetch(off_nx_ref, 1)

    @pl.when(i < side_steps)
    def _():
        _wait_rows_in(src_hbm, buf_ref.at[slot], sem.at[slot])
        _weighted_sum(base_ref, gates_ref, gfin_ref, buf_ref.at[slot], out_ref)

    more = i + 2 < side_steps

    @pl.when((n_valid > 0) & more)
    def _():
        fetch(off_nx2_ref, slot)
        _swiglu_block(n_valid, xs_ref, wgb_ref, wub_ref, wdb_ref, y_ref)

    @pl.when((n_valid > 0) & jnp.logical_not(more))
    def _():
        _swiglu_block(n_valid, xs_ref, wgb_ref, wub_ref, wdb_ref, y_ref)

    @pl.when((n_valid == 0) & more)
    def _():
        fetch(off_nx2_ref, slot)
        y_ref[...] = jnp.zeros_like(y_ref)

    @pl.when((n_valid == 0) & jnp.logical_not(more))
    def _():
        y_ref[...] = jnp.zeros_like(y_ref)


def _expert_specs(d, ff):
    r = EXPERT_ROWS
    rows = pl.BlockSpec((r * ROW_TILE, LANES), lambda i, be, br: (i, 0))
    weights = [pl.BlockSpec((None, d, ff), lambda i, be, br: (be[i], 0, 0)),
               pl.BlockSpec((None, d, ff), lambda i, be, br: (be[i], 0, 0)),
               pl.BlockSpec((None, ff, d), lambda i, be, br: (be[i], 0, 0))]
    scratch = [pltpu.VMEM((d, ff), jnp.bfloat16), pltpu.VMEM((d, ff), jnp.bfloat16),
               pltpu.VMEM((ff, d), jnp.bfloat16)]
    return rows, weights, scratch


def _experts_out(blk_e, blk_rows, xs, wg, wu, wd, off, xn, first_block, n_rows_next):
    side_steps = off.shape[0]
    d, ff = wg.shape[1], wg.shape[2]
    rows, weights, scratch = _expert_specs(d, ff)
    side = lambda i, be, br: jnp.minimum(i, side_steps - 1)
    return pl.pallas_call(
        functools.partial(_experts_out_kernel, side_steps),
        out_shape=(jax.ShapeDtypeStruct(xs.shape, xs.dtype),
                   jax.ShapeDtypeStruct((n_rows_next * ROW_TILE, LANES), xs.dtype)),
        grid_spec=pltpu.PrefetchScalarGridSpec(
            num_scalar_prefetch=2,
            grid=(xs.shape[0] // (EXPERT_ROWS * ROW_TILE),),
            in_specs=[rows] + weights + [
                _offsets_spec(SIDE_ROWS, lambda i, be, br: (side(i, be, br), 0, 0)),
                pl.BlockSpec((SIDE_ROWS * ROW_TILE, LANES), lambda i, be, br: (first_block + side(i, be, br), 0))],
            out_specs=(rows, pl.BlockSpec(memory_space=pl.ANY)),
            scratch_shapes=scratch + [pltpu.VMEM((2, SIDE_ROWS * ROW_TILE, LANES), xs.dtype),
                                      pltpu.SemaphoreType.DMA((2,))],
        ),
        compiler_params=pltpu.CompilerParams(
            dimension_semantics=("arbitrary",), vmem_limit_bytes=VMEM_LIMIT),
        name="experts_send",
    )(blk_e, blk_rows, xs, wg, wu, wd, off, xn)


def _experts_in(blk_e, blk_rows, xs, wg, wu, wd, off, base, gates, gfin, y_prev):
    side_steps = off.shape[0]
    d, ff = wg.shape[1], wg.shape[2]
    n = base.shape[0]
    rows, weights, scratch = _expert_specs(d, ff)
    side = lambda i, be, br: jnp.minimum(i, side_steps - 1)
    nxt = lambda i, be, br: jnp.minimum(i + 1, side_steps - 1)
    nxt2 = lambda i, be, br: jnp.minimum(i + 2, side_steps - 1)
    return pl.pallas_call(
        functools.partial(_experts_in_kernel, side_steps),
        out_shape=(jax.ShapeDtypeStruct(xs.shape, xs.dtype), jax.ShapeDtypeStruct((n, d), jnp.float32)),
        grid_spec=pltpu.PrefetchScalarGridSpec(
            num_scalar_prefetch=2,
            grid=(xs.shape[0] // (EXPERT_ROWS * ROW_TILE),),
            in_specs=[rows] + weights + [
                _offsets_spec(SIDE_ROWS, lambda i, be, br: (side(i, be, br), 0, 0)),
                _offsets_spec(SIDE_ROWS, lambda i, be, br: (nxt(i, be, br), 0, 0)),
                _offsets_spec(SIDE_ROWS, lambda i, be, br: (nxt2(i, be, br), 0, 0)),
                pl.BlockSpec((SIDE_ROWS, d), lambda i, be, br: (side(i, be, br), 0)),
                pl.BlockSpec((SIDE_ROWS, TOP_K), lambda i, be, br: (side(i, be, br), 0)),
                pl.BlockSpec((1, d), lambda i, be, br: (0, 0)),
                pl.BlockSpec(memory_space=pl.ANY)],
            out_specs=(rows, pl.BlockSpec((SIDE_ROWS, d), lambda i, be, br: (side(i, be, br), 0))),
            scratch_shapes=scratch + [pltpu.VMEM((2, TOP_K * SIDE_ROWS * ROW_TILE, LANES), xs.dtype),
                                      pltpu.SemaphoreType.DMA((2,))],
        ),
        compiler_params=pltpu.CompilerParams(
            dimension_semantics=("arbitrary",), vmem_limit_bytes=VMEM_LIMIT),
        name="experts_fetch",
    )(blk_e, blk_rows, xs, wg, wu, wd, off, off, off, base, gates, gfin, y_prev)


def _combine_kernel(off_ref, off_nx_ref, base_ref, gates_ref, gfin_ref, y_hbm, out_in_ref, out_ref, buf_ref, sem):
    del out_in_ref
    i = pl.program_id(0)
    n_steps = pl.num_programs(0)
    n_tok = base_ref.shape[0]

    def issue(off_r, half):
        def body(g, carry):
            for u in range(ISSUE_UNROLL):
                _start_rows_in(g * ISSUE_UNROLL + u, n_tok, y_hbm, off_r, buf_ref.at[half], sem.at[half])
            return carry
        lax.fori_loop(0, n_tok // ISSUE_UNROLL, body, 0)

    half = i % 2

    @pl.when(i == 0)
    def _():
        issue(off_ref, 0)

    @pl.when(i + 1 < n_steps)
    def _():
        issue(off_nx_ref, 1 - half)

    _wait_rows_in(y_hbm, buf_ref.at[half], sem.at[half])
    _weighted_sum(base_ref, gates_ref, gfin_ref, buf_ref.at[half], out_ref)


def _combine(off, base, gates, gfin, y, out_prev, first_block):
    n, d = base.shape
    t = COMBINE_ROWS
    n_steps = off.shape[0]
    tok = lambda width: pl.BlockSpec((t, width), lambda i: (first_block + i, 0))
    return pl.pallas_call(
        _combine_kernel,
        out_shape=jax.ShapeDtypeStruct((n, d), jnp.float32),
        grid=(n_steps,),
        in_specs=[_offsets_spec(t, lambda i: (i, 0, 0)),
                  _offsets_spec(t, lambda i: (jnp.minimum(i + 1, n_steps - 1), 0, 0)),
                  tok(d), tok(TOP_K), pl.BlockSpec((1, d), lambda i: (0, 0)),
                  pl.BlockSpec(memory_space=pl.ANY), pl.BlockSpec(memory_space=pl.ANY)],
        out_specs=tok(d),
        scratch_shapes=[pltpu.VMEM((2, TOP_K * t * ROW_TILE, LANES), y.dtype),
                        pltpu.SemaphoreType.DMA((2,))],
        input_output_aliases={6: 0},
        compiler_params=pltpu.CompilerParams(
            dimension_semantics=("arbitrary",), vmem_limit_bytes=VMEM_LIMIT),
        name="combine",
    )(off, off, base, gates, gfin, y, out_prev)


def _group_layout(counts, idx, rank):
    r = EXPERT_ROWS
    n_g = idx.shape[1]
    n_blocks = (n_g * TOP_K) // r + N_EXPERTS
    padded = (counts + r - 1) // r * r
    ends = jnp.cumsum(padded)
    starts = (ends - padded).astype(jnp.int32)
    blk_row0 = jnp.arange(n_blocks, dtype=jnp.int32) * r
    blk_e = jnp.minimum(jnp.sum((ends[None, :] <= blk_row0[:, None]).astype(jnp.int32), axis=1), N_EXPERTS - 1)
    onehot = (blk_e[:, None] == jnp.arange(N_EXPERTS, dtype=jnp.int32)[None, :]).astype(jnp.int32)
    blk_rows = jnp.clip(onehot @ counts - (blk_row0 - onehot @ starts), 0, r).astype(jnp.int32)
    dest = rank + jnp.sum(jnp.where(idx[None] == jnp.arange(N_EXPERTS, dtype=jnp.int32)[:, None, None],
                                    starts[:, None, None], 0), axis=0)
    return blk_e, blk_rows, (dest * ROW_TILE).T.reshape(-1), n_blocks * r


def kernel(x, meta_tokens, norm_mix_g, w_in, lb_table, hgrn_norm_g, conv_w, conv_norm_g, w_out,
           norm_ffn_g, w_router, router_bias, w_gate_e, w_up_e, w_down_e, w_gate_s, w_up_s, w_down_s,
           norm_final_g):
    bsz, seq, d = x.shape
    n = bsz * seq
    n_g = n // TOKEN_GROUPS
    bf = jnp.bfloat16
    assert TOKEN_GROUPS == 2
    assert seq % MIX_ROWS == 0 and MIX_ROWS % CHUNK == 0
    assert n_g % ROUTE_ROWS == 0 and n_g % DISPATCH_ROWS == 0 and n_g % COMBINE_ROWS == 0
    assert (n_g * TOP_K) % EXPERT_ROWS == 0 and n_g % SIDE_ROWS == 0
    assert 2 <= n_g // SIDE_ROWS < (n_g * TOP_K) // EXPERT_ROWS + N_EXPERTS

    meta_pad = jnp.zeros((CHUNK, d), jnp.float32).at[CHUNK - N_META:].set(meta_tokens)
    h1 = _mixer(x, meta_pad, norm_mix_g[0:1], w_in[0].astype(bf), lb_table, hgrn_norm_g[0:1],
                conv_w[0], conv_norm_g[0:1], w_out[0].astype(bf))

    base, xn, idx, gate, rank, counts = _router(
        h1.reshape(n, d), norm_ffn_g[0:1], w_router[0].T, router_bias[0][:, None],
        w_gate_s[0].astype(bf), w_up_s[0].astype(bf), w_down_s[0].astype(bf))

    lay = [_group_layout(counts[g, :, 0], idx[:, g * n_g:(g + 1) * n_g], rank[:, g * n_g:(g + 1) * n_g])
           for g in range(TOKEN_GROUPS)]
    (be0, br0, off0, rows0), (be1, br1, off1, rows1) = lay
    steps = lambda off, t: off.reshape(n_g // t, 1, t * TOP_K)
    gates = gate.T
    gfin = norm_final_g[None, :]
    wg, wu, wd = w_gate_e[0], w_up_e[0], w_down_e[0]

    xs0 = _dispatch(steps(off0, DISPATCH_ROWS), xn, rows0, 0)
    y0, xs1 = _experts_out(be0, br0, xs0, wg, wu, wd, steps(off1, SIDE_ROWS), xn, n_g // SIDE_ROWS, rows1)
    y1, out = _experts_in(be1, br1, xs1, wg, wu, wd, steps(off0, SIDE_ROWS), base, gates, gfin, y0)
    out = _combine(steps(off1, COMBINE_ROWS), base, gates, gfin, y1, out, n_g // COMBINE_ROWS)
    return out.reshape(bsz, seq, d)
```

```python
import functools

import numpy as np
import jax
import jax.numpy as jnp
from jax import lax
from jax.experimental import pallas as pl
from jax.experimental.pallas import tpu as pltpu

N_META = 16
CHUNK = 128
HEADS = 4
HEAD_DIM = 128
HGRN_W = HEADS * HEAD_DIM
CONV_W = 512
CONV_GROUPS = 4
CONV_K = 3
N_EXPERTS = 64
TOP_K = 8
ROUTED_SCALE = 2.5
EPS = 1e-6

V7X_VMEM_BYTES = 64 * 1024 * 1024
VMEM_LIMIT = V7X_VMEM_BYTES - 8 * 1024 * 1024

MIX_ROWS = 512
CHUNK_UNROLL = 4
ROUTE_ROWS = 512
DISPATCH_ROWS = 512
EXPERT_ROWS = 1024
COMBINE_ROWS = 256
DMA_QUEUES = 2
ISSUE_UNROLL = 8
TOKEN_GROUPS = 2
SIDE_ROWS = 128
PACE_EVERY = 4

LANES = 128
ROW_TILE = 4

HALF_SPANS = (64, 32, 16, 8, 4, 2, 1)
N_LEVELS = len(HALF_SPANS) + 1


def _decay_sum_matrix():
    a = np.zeros((N_LEVELS, CHUNK, CHUNK), np.float32)
    a[0] = np.tril(np.ones((CHUNK, CHUNK), np.float32))
    for i, m in enumerate(HALF_SPANS):
        for t in range(CHUNK):
            mid = (t // (2 * m)) * 2 * m + m
            if t >= mid:
                a[1 + i, t, mid:t + 1] = 1.0
            else:
                a[1 + i, t, t + 1:mid] = 1.0
    return a.reshape(N_LEVELS * CHUNK, CHUNK)


def _level_matrix():
    lv = np.full((CHUNK, CHUNK), -1, np.int32)
    for t in range(CHUNK):
        lv[t, t] = len(HALF_SPANS)
        for s in range(t):
            top = (t ^ s).bit_length() - 1
            lv[t, s] = HALF_SPANS.index(1 << top)
    return lv


def _rms(x, g):
    return x * lax.rsqrt(jnp.mean(x * x, axis=-1, keepdims=True) + EPS) * g


def _group_rms(x, g, width):
    outs = []
    for j in range(x.shape[-1] // width):
        xs = x[:, j * width:(j + 1) * width]
        outs.append(xs * lax.rsqrt(jnp.mean(xs * xs, axis=-1, keepdims=True) + EPS))
    return jnp.concatenate(outs, axis=-1) * g


def _sigmoid_pair(z):
    t = jnp.exp(-jnp.abs(z))
    inv = 1.0 / (1.0 + t)
    big, small = inv, t * inv
    pos = z >= 0
    return jnp.where(pos, big, small), jnp.where(pos, small, big)


def _pack_rows(x):
    half = x.shape[1] // 2
    bits = lambda v: lax.bitcast_convert_type(v.astype(jnp.bfloat16).astype(jnp.float32), jnp.uint32)
    return (bits(x[:, :half]) >> 16) | (bits(x[:, half:]) & jnp.uint32(0xFFFF0000))


def _unpack_words(w):
    lo = lax.bitcast_convert_type(w << 16, jnp.float32)
    hi = lax.bitcast_convert_type(w & jnp.uint32(0xFFFF0000), jnp.float32)
    return lo, hi


def _store_row_tiles(ref, words):
    t = words.shape[0]
    for c in range(ROW_TILE):
        ref[pl.ds(c, t, stride=ROW_TILE), :] = words[:, c * LANES:(c + 1) * LANES]


def _load_row_tiles(ref, t):
    parts = [_unpack_words(ref[pl.ds(c, t, stride=ROW_TILE), :]) for c in range(ROW_TILE)]
    return [p[0] for p in parts], [p[1] for p in parts]


def _dot(a, b):
    return jnp.dot(a, b, preferred_element_type=jnp.float32)


def _dot_nt(a, b):
    return lax.dot_general(a, b, (((1,), (1,)), ((), ())), preferred_element_type=jnp.float32)


def _dot_tn(a, b):
    return lax.dot_general(a, b, (((0,), (0,)), ((), ())), preferred_element_type=jnp.float32)


def _hgrn_chunk(q, z, iv, lb, amat, level, st_ref, first_valid_row):
    sig, sig_neg = _sigmoid_pair(z)
    lf = jnp.log(lb + (1.0 - lb) * sig)
    k = (1.0 - lb) * sig_neg
    row = lax.broadcasted_iota(jnp.int32, (CHUNK, HGRN_W), 0)
    if first_valid_row:
        valid = row >= first_valid_row
        lf = jnp.where(valid, lf, 0.0)
        k = jnp.where(valid, k, 0.0)

    h1 = lf.astype(jnp.bfloat16)
    h2 = (lf - h1.astype(jnp.float32)).astype(jnp.bfloat16)
    e_all = _dot(amat, jnp.concatenate([h1, h2], axis=0))

    b = e_all[0:CHUNK]
    b_last = b[CHUNK - 1:CHUNK]
    q_in = (q * jnp.exp(b)).astype(jnp.bfloat16)
    k_out = (k * jnp.exp(b_last - b)).astype(jnp.bfloat16)
    st_decay = jnp.exp(b_last)
    v_bf = iv.astype(jnp.bfloat16)

    q_lv = [q.astype(jnp.bfloat16)]
    k_lv = [k.astype(jnp.bfloat16)]
    for i, m in enumerate(HALF_SPANS):
        ex = jnp.exp(e_all[(1 + i) * CHUNK:(2 + i) * CHUNK])
        right = (row & m) != 0
        q_lv.append(jnp.where(right, q * ex, 0.0).astype(jnp.bfloat16))
        k_lv.append(jnp.where(right, 0.0, k * ex).astype(jnp.bfloat16))
    lv_of = [len(HALF_SPANS)] + list(range(len(HALF_SPANS)))

    outs = []
    for h in range(HEADS):
        cols = slice(h * HEAD_DIM, (h + 1) * HEAD_DIM)
        scores = jnp.zeros((CHUNK, CHUNK), jnp.float32)
        for ql, kl, lv in zip(q_lv, k_lv, lv_of):
            scores = jnp.where(level == lv, _dot_nt(ql[:, cols], kl[:, cols]), scores)
        st = st_ref[h]
        o = _dot(scores.astype(jnp.bfloat16), v_bf[:, cols]) + _dot_nt(q_in[:, cols], st.astype(jnp.bfloat16))
        st_ref[h] = st * st_decay[:, cols] + _dot_tn(v_bf[:, cols], k_out[:, cols])
        outs.append(o)
    return jnp.concatenate(outs, axis=-1)


def _mixer_kernel(x_ref, meta_ref, gmix_ref, win_ref, lbt_ref, ghg_ref, cw_ref, gcv_ref, wout_ref,
                  amat_ref, level_ref, h1_ref, proj_ref, o_ref, u_ref, st_ref):
    j = pl.program_id(1)
    rows = x_ref.shape[0]
    n_in = win_ref.shape[1]

    lbt = lbt_ref[...]
    lbe = jnp.exp(lbt - jnp.max(lbt, axis=0, keepdims=True))
    lb = lbe[0:1] / jnp.sum(lbe, axis=0, keepdims=True)

    amat = amat_ref[...]
    level = level_ref[...]
    gmix = gmix_ref[...]

    def project(xv, dst_rows):
        xn = _rms(xv, gmix).astype(jnp.bfloat16)
        for c0 in range(0, n_in, 512):
            proj_ref[dst_rows, c0:c0 + 512] = _dot(xn, win_ref[:, c0:c0 + 512])

    @pl.when(j == 0)
    def _():
        st_ref[...] = jnp.zeros_like(st_ref)
        project(meta_ref[...], pl.ds(0, CHUNK))
        pm = proj_ref[0:CHUNK, :]
        _hgrn_chunk(pm[:, 0:512], pm[:, 512:1024], pm[:, 1024:1536], lb, amat, level, st_ref,
                    CHUNK - N_META)
        u_ref[0:8, :] = (pm[:, 2560:3072] * pm[:, 3072:3584])[CHUNK - 8:CHUNK]

    project(x_ref[...], pl.ds(0, rows))

    def chunk_body(c, carry):
        for u in range(CHUNK_UNROLL):
            r0 = pl.multiple_of((c * CHUNK_UNROLL + u) * CHUNK, CHUNK)
            q = proj_ref[pl.ds(r0, CHUNK), 0:512]
            z = proj_ref[pl.ds(r0, CHUNK), 512:1024]
            iv = proj_ref[pl.ds(r0, CHUNK), 1024:1536]
            o_ref[pl.ds(r0, CHUNK), :] = _hgrn_chunk(q, z, iv, lb, amat, level, st_ref, 0)
        return carry

    lax.fori_loop(0, rows // (CHUNK * CHUNK_UNROLL), chunk_body, 0)

    g_out = proj_ref[:, 1536:2048]
    g_sig, _ = _sigmoid_pair(g_out)
    y_hgrn = _group_rms(o_ref[...], ghg_ref[...], HEAD_DIM) * (g_out * g_sig)

    u = proj_ref[:, 2560:3072] * proj_ref[:, 3072:3584]
    u_ref[8:8 + rows, :] = u
    cw = cw_ref[...]
    y = cw[2:3] * u + cw[1:2] * u_ref[7:7 + rows, :] + cw[0:1] * u_ref[6:6 + rows, :]
    u_ref[0:8, :] = u[rows - 8:rows]
    y_conv = _group_rms(proj_ref[:, 2048:2560] * y, gcv_ref[...], CONV_W // CONV_GROUPS)

    mixed = jnp.concatenate([y_hgrn, y_conv], axis=-1).astype(jnp.bfloat16)
    h1_ref[...] = x_ref[...] + _dot(mixed, wout_ref[...])


def _mixer(x, meta_pad, gmix, w_in, lb_table, ghg, conv_w, gcv, w_out):
    bsz, seq, d = x.shape
    n_in = w_in.shape[1]
    rows = MIX_ROWS
    const = lambda *shape: pl.BlockSpec(shape, lambda b, j: (0,) * len(shape))
    return pl.pallas_call(
        _mixer_kernel,
        out_shape=jax.ShapeDtypeStruct((bsz, seq, d), jnp.float32),
        grid=(bsz, seq // rows),
        in_specs=[
            pl.BlockSpec((None, rows, d), lambda b, j: (b, j, 0)),
            const(CHUNK, d), const(1, d), const(d, n_in), const(*lb_table.shape), const(1, HGRN_W),
            const(CONV_K, CONV_W), const(1, CONV_W), const(d, d),
            const(N_LEVELS * CHUNK, 2 * CHUNK), const(CHUNK, CHUNK),
        ],
        out_specs=pl.BlockSpec((None, rows, d), lambda b, j: (b, j, 0)),
        scratch_shapes=[
            pltpu.VMEM((rows, n_in), jnp.float32),
            pltpu.VMEM((rows, HGRN_W), jnp.float32),
            pltpu.VMEM((rows + 8, CONV_W), jnp.float32),
            pltpu.VMEM((HEADS, HEAD_DIM, HEAD_DIM), jnp.float32),
        ],
        compiler_params=pltpu.CompilerParams(
            dimension_semantics=("arbitrary", "arbitrary"), vmem_limit_bytes=VMEM_LIMIT),
        name="mixer",
    )(x, meta_pad, gmix, w_in, lb_table, ghg, conv_w, gcv, w_out,
      jnp.asarray(np.tile(_decay_sum_matrix(), (1, 2)), jnp.bfloat16), jnp.asarray(_level_matrix()))


def _router_kernel(h1_ref, gffn_ref, wr_ref, bias_ref, wgs_ref, wus_ref, wds_ref, tri_ref,
                   base_ref, xn_ref, idx_ref, gate_ref, rank_ref, counts_ref, carry_ref):
    i = pl.program_id(0)
    n_tok = h1_ref.shape[0]

    @pl.when(i % (pl.num_programs(0) // TOKEN_GROUPS) == 0)
    def _():
        carry_ref[...] = jnp.zeros_like(carry_ref)

    h1 = h1_ref[...]
    xn = _rms(h1, gffn_ref[...])
    _store_row_tiles(xn_ref, _pack_rows(xn))
    xb = xn.astype(jnp.bfloat16)

    g_pre = _dot(xb, wgs_ref[...])
    gate_s, _ = _sigmoid_pair(g_pre)
    hid = (g_pre * gate_s) * _dot(xb, wus_ref[...])
    base_ref[...] = h1 + _dot(hid.astype(jnp.bfloat16), wds_ref[...])

    logits = lax.dot_general(wr_ref[...], xn, (((1,), (1,)), ((), ())),
                             precision=lax.Precision.HIGHEST, preferred_element_type=jnp.float32)
    scores, _ = _sigmoid_pair(logits)
    sel = scores + bias_ref[...]
    eid = lax.broadcasted_iota(jnp.int32, (N_EXPERTS, n_tok), 0).astype(jnp.float32)
    picks, top_s = [], []
    for _ in range(TOP_K):
        best = jnp.max(sel, axis=0, keepdims=True)
        pick = jnp.min(jnp.where(sel == best, eid, float(N_EXPERTS)), axis=0, keepdims=True)
        hit = eid == pick
        top_s.append(jnp.sum(jnp.where(hit, scores, 0.0), axis=0, keepdims=True))
        sel = jnp.where(hit, -jnp.inf, sel)
        picks.append(pick)
    top_s = jnp.concatenate(top_s, axis=0)
    gate_ref[...] = top_s / jnp.sum(top_s, axis=0, keepdims=True) * ROUTED_SCALE
    idx_ref[...] = jnp.concatenate(picks, axis=0).astype(jnp.int32)

    chosen = jnp.zeros((N_EXPERTS, n_tok), jnp.float32)
    for pick in picks:
        chosen = chosen + jnp.where(eid == pick, 1.0, 0.0)
    incl = _dot(chosen.astype(jnp.bfloat16), tri_ref[...])
    before = carry_ref[...] + incl - 1.0
    ranks = [jnp.sum(jnp.where(eid == pick, before, 0.0), axis=0, keepdims=True) for pick in picks]
    rank_ref[...] = jnp.concatenate(ranks, axis=0).astype(jnp.int32)
    carry_ref[...] = carry_ref[...] + incl[:, n_tok - 1:n_tok]
    counts_ref[...] = jnp.broadcast_to(carry_ref[...], counts_ref.shape).astype(jnp.int32)


def _router(h1, gffn, w_router, bias, wgs, wus, wds):
    n, d = h1.shape
    t = ROUTE_ROWS
    ff = wgs.shape[1]
    tri = jnp.asarray(np.triu(np.ones((t, t), np.float32)), jnp.bfloat16)
    const = lambda *shape: pl.BlockSpec(shape, lambda i: (0,) * len(shape))
    tok = lambda width: pl.BlockSpec((t, width), lambda i: (i, 0))
    slot = pl.BlockSpec((TOP_K, t), lambda i: (0, i))
    return pl.pallas_call(
        _router_kernel,
        out_shape=(
            jax.ShapeDtypeStruct((n, d), jnp.float32),
            jax.ShapeDtypeStruct((n * ROW_TILE, LANES), jnp.uint32),
            jax.ShapeDtypeStruct((TOP_K, n), jnp.int32),
            jax.ShapeDtypeStruct((TOP_K, n), jnp.float32),
            jax.ShapeDtypeStruct((TOP_K, n), jnp.int32),
            jax.ShapeDtypeStruct((TOKEN_GROUPS, N_EXPERTS, 128), jnp.int32),
        ),
        grid=(n // t,),
        in_specs=[tok(d), const(1, d), const(N_EXPERTS, d), const(N_EXPERTS, 1),
                  const(d, ff), const(d, ff), const(ff, d), const(t, t)],
        out_specs=(tok(d), pl.BlockSpec((t * ROW_TILE, LANES), lambda i: (i, 0)), slot, slot, slot,
                   pl.BlockSpec((None, N_EXPERTS, 128), lambda i: (i // (n // t // TOKEN_GROUPS), 0, 0))),
        scratch_shapes=[pltpu.VMEM((N_EXPERTS, 1), jnp.float32)],
        compiler_params=pltpu.CompilerParams(
            dimension_semantics=("arbitrary",), vmem_limit_bytes=VMEM_LIMIT),
        name="router",
    )(h1, gffn, w_router, bias, wgs, wus, wds, tri)


def _slab(ref, first_sublane):
    if not isinstance(first_sublane, int):
        first_sublane = pl.multiple_of(first_sublane, ROW_TILE)
    return ref.at[pl.ds(first_sublane, ROW_TILE)]


def _paced_offset(off_ref, k, pace):
    if pace is None or k % PACE_EVERY:
        return off_ref[0, k], pace
    off = off_ref[0, k + pace]
    return off, lax.shift_right_logical(off, 31)


def _start_rows_out(tok, rows_ref, off_ref, dst_hbm, sem, pace=None):
    for s in range(TOP_K):
        off, pace = _paced_offset(off_ref, tok * TOP_K + s, pace)
        pltpu.make_async_copy(_slab(rows_ref, tok * ROW_TILE), _slab(dst_hbm, off), sem).start(
            priority=s % DMA_QUEUES)
    return pace


def _wait_rows_out(rows_ref, dst_hbm, sem):
    for s in range(TOP_K):
        pltpu.make_async_copy(rows_ref, dst_hbm.at[pl.ds(0, rows_ref.shape[0])], sem).wait()


def _start_rows_in(tok, n_tok, src_hbm, off_ref, buf_ref, sem, pace=None):
    for s in range(TOP_K):
        off, pace = _paced_offset(off_ref, tok * TOP_K + s, pace)
        pltpu.make_async_copy(_slab(src_hbm, off), _slab(buf_ref, (s * n_tok + tok) * ROW_TILE), sem).start(
            priority=s % DMA_QUEUES)
    return pace


def _wait_rows_in(src_hbm, buf_ref, sem):
    pltpu.make_async_copy(src_hbm.at[pl.ds(0, buf_ref.shape[0])], buf_ref, sem).wait()


def _weighted_sum(base_ref, gates_ref, gfin_ref, buf_ref, out_ref):
    n_tok = base_ref.shape[0]
    gates = gates_ref[...]
    width = ROW_TILE * LANES
    lo_chunks, hi_chunks = [], []
    for c in range(ROW_TILE):
        acc_lo = base_ref[:, c * LANES:(c + 1) * LANES]
        acc_hi = base_ref[:, width + c * LANES:width + (c + 1) * LANES]
        for s in range(TOP_K):
            lo, hi = _unpack_words(buf_ref[pl.ds(s * n_tok * ROW_TILE + c, n_tok, stride=ROW_TILE), :])
            acc_lo = acc_lo + gates[:, s:s + 1] * lo
            acc_hi = acc_hi + gates[:, s:s + 1] * hi
        lo_chunks.append(acc_lo)
        hi_chunks.append(acc_hi)
    out_ref[...] = _rms(jnp.concatenate(lo_chunks + hi_chunks, axis=-1), gfin_ref[...])


def _offsets_spec(n_tok, index_map):
    return pl.BlockSpec((None, 1, n_tok * TOP_K), index_map, memory_space=pltpu.SMEM)


def _dispatch_kernel(off_ref, xn_ref, xs_hbm, sem):
    n_tok = xn_ref.shape[0] // ROW_TILE

    def issue(g, carry):
        for u in range(ISSUE_UNROLL):
            _start_rows_out(g * ISSUE_UNROLL + u, xn_ref, off_ref, xs_hbm, sem)
        return carry

    lax.fori_loop(0, n_tok // ISSUE_UNROLL, issue, 0)
    _wait_rows_out(xn_ref, xs_hbm, sem)


def _dispatch(off, xn, n_rows, first_block):
    t = DISPATCH_ROWS
    return pl.pallas_call(
        _dispatch_kernel,
        out_shape=jax.ShapeDtypeStruct((n_rows * ROW_TILE, LANES), xn.dtype),
        grid=(off.shape[0],),
        in_specs=[_offsets_spec(t, lambda i: (i, 0, 0)),
                  pl.BlockSpec((t * ROW_TILE, LANES), lambda i: (first_block + i, 0))],
        out_specs=pl.BlockSpec(memory_space=pl.ANY),
        scratch_shapes=[pltpu.SemaphoreType.DMA(())],
        compiler_params=pltpu.CompilerParams(dimension_semantics=("arbitrary",)),
        name="dispatch",
    )(off, xn)


def _swiglu_block(n_valid, xs_ref, wgb_ref, wub_ref, wdb_ref, y_ref):
    r = EXPERT_ROWS
    keep = lax.broadcasted_iota(jnp.int32, (r, LANES), 0) < n_valid
    lo, hi = _load_row_tiles(xs_ref, r)
    xb = jnp.concatenate([jnp.where(keep, c, 0.0).astype(jnp.bfloat16) for c in lo + hi], axis=-1)
    g = _dot(xb, wgb_ref[...])
    u = _dot(xb, wub_ref[...])
    g_sig, _ = _sigmoid_pair(g)
    hid = ((g * g_sig) * u).astype(jnp.bfloat16)
    _store_row_tiles(y_ref, _pack_rows(_dot(hid, wdb_ref[...])))


def _refresh_weights(i, blk_e_ref, wg_ref, wu_ref, wd_ref, wgb_ref, wub_ref, wdb_ref):
    @pl.when((i == 0) | (blk_e_ref[i] != blk_e_ref[jnp.maximum(i - 1, 0)]))
    def _():
        wgb_ref[...] = wg_ref[...].astype(jnp.bfloat16)
        wub_ref[...] = wu_ref[...].astype(jnp.bfloat16)
        wdb_ref[...] = wd_ref[...].astype(jnp.bfloat16)


def _experts_out_kernel(side_steps, blk_e_ref, blk_rows_ref, xs_ref, wg_ref, wu_ref, wd_ref, off_ref, rows_ref,
                        y_ref, dst_hbm, wgb_ref, wub_ref, wdb_ref, stage_ref, sem):
    i = pl.program_id(0)
    n_valid = blk_rows_ref[i]
    side = i < side_steps
    slot = i % 2
    _refresh_weights(i, blk_e_ref, wg_ref, wu_ref, wd_ref, wgb_ref, wub_ref, wdb_ref)

    def send():
        stage_ref[slot] = rows_ref[...]
        pace = jnp.int32(0)
        for tok in range(SIDE_ROWS):
            pace = _start_rows_out(tok, stage_ref.at[slot], off_ref, dst_hbm, sem.at[slot], pace)

    @pl.when((n_valid > 0) & side)
    def _():
        send()
        _swiglu_block(n_valid, xs_ref, wgb_ref, wub_ref, wdb_ref, y_ref)

    @pl.when((n_valid > 0) & jnp.logical_not(side))
    def _():
        _swiglu_block(n_valid, xs_ref, wgb_ref, wub_ref, wdb_ref, y_ref)

    @pl.when((n_valid == 0) & side)
    def _():
        send()
        y_ref[...] = jnp.zeros_like(y_ref)

    @pl.when((n_valid == 0) & jnp.logical_not(side))
    def _():
        y_ref[...] = jnp.zeros_like(y_ref)

    @pl.when((i >= 1) & (i <= side_steps))
    def _():
        _wait_rows_out(stage_ref.at[1 - slot], dst_hbm, sem.at[1 - slot])


def _experts_in_kernel(side_steps, blk_e_ref, blk_rows_ref, xs_ref, wg_ref, wu_ref, wd_ref,
                       off_ref, off_nx_ref, off_nx2_ref, base_ref, gates_ref, gfin_ref, src_hbm, y_ref, out_ref,
                       wgb_ref, wub_ref, wdb_ref, buf_ref, sem):
    i = pl.program_id(0)
    n_valid = blk_rows_ref[i]
    slot = i % 2
    _refresh_weights(i, blk_e_ref, wg_ref, wu_ref, wd_ref, wgb_ref, wub_ref, wdb_ref)

    def fetch(off_r, into):
        pace = jnp.int32(0)
        for tok in range(SIDE_ROWS):
            pace = _start_rows_in(tok, SIDE_ROWS, src_hbm, off_r, buf_ref.at[into], sem.at[into], pace)

    @pl.when(i == 0)
    def _():
        fetch(off_ref, 0)
        fetch(off_nx_ref, 1)

    @pl.when(i < side_steps)
    def _():
        _wait_rows_in(src_hbm, buf_ref.at[slot], sem.at[slot])
        _weighted_sum(base_ref, gates_ref, gfin_ref, buf_ref.at[slot], out_ref)

    more = i + 2 < side_steps

    @pl.when((n_valid > 0) & more)
    def _():
        fetch(off_nx2_ref, slot)
        _swiglu_block(n_valid, xs_ref, wgb_ref, wub_ref, wdb_ref, y_ref)

    @pl.when((n_valid > 0) & jnp.logical_not(more))
    def _():
        _swiglu_block(n_valid, xs_ref, wgb_ref, wub_ref, wdb_ref, y_ref)

    @pl.when((n_valid == 0) & more)
    def _():
        fetch(off_nx2_ref, slot)
        y_ref[...] = jnp.zeros_like(y_ref)

    @pl.when((n_valid == 0) & jnp.logical_not(more))
    def _():
        y_ref[...] = jnp.zeros_like(y_ref)


def _expert_specs(d, ff):
    r = EXPERT_ROWS
    rows = pl.BlockSpec((r * ROW_TILE, LANES), lambda i, be, br: (i, 0))
    weights = [pl.BlockSpec((None, d, ff), lambda i, be, br: (be[i], 0, 0)),
               pl.BlockSpec((None, d, ff), lambda i, be, br: (be[i], 0, 0)),
               pl.BlockSpec((None, ff, d), lambda i, be, br: (be[i], 0, 0))]
    scratch = [pltpu.VMEM((d, ff), jnp.bfloat16), pltpu.VMEM((d, ff), jnp.bfloat16),
               pltpu.VMEM((ff, d), jnp.bfloat16)]
    return rows, weights, scratch


def _experts_out(blk_e, blk_rows, xs, wg, wu, wd, off, xn, first_block, n_rows_next):
    side_steps = off.shape[0]
    d, ff = wg.shape[1], wg.shape[2]
    rows, weights, scratch = _expert_specs(d, ff)
    side = lambda i, be, br: jnp.minimum(i, side_steps - 1)
    return pl.pallas_call(
        functools.partial(_experts_out_kernel, side_steps),
        out_shape=(jax.ShapeDtypeStruct(xs.shape, xs.dtype),
                   jax.ShapeDtypeStruct((n_rows_next * ROW_TILE, LANES), xs.dtype)),
        grid_spec=pltpu.PrefetchScalarGridSpec(
            num_scalar_prefetch=2,
            grid=(xs.shape[0] // (EXPERT_ROWS * ROW_TILE),),
            in_specs=[rows] + weights + [
                _offsets_spec(SIDE_ROWS, lambda i, be, br: (side(i, be, br), 0, 0)),
                pl.BlockSpec((SIDE_ROWS * ROW_TILE, LANES), lambda i, be, br: (first_block + side(i, be, br), 0))],
            out_specs=(rows, pl.BlockSpec(memory_space=pl.ANY)),
            scratch_shapes=scratch + [pltpu.VMEM((2, SIDE_ROWS * ROW_TILE, LANES), xs.dtype),
                                      pltpu.SemaphoreType.DMA((2,))],
        ),
        compiler_params=pltpu.CompilerParams(
            dimension_semantics=("arbitrary",), vmem_limit_bytes=VMEM_LIMIT),
        name="experts_send",
    )(blk_e, blk_rows, xs, wg, wu, wd, off, xn)


def _experts_in(blk_e, blk_rows, xs, wg, wu, wd, off, base, gates, gfin, y_prev):
    side_steps = off.shape[0]
    d, ff = wg.shape[1], wg.shape[2]
    n = base.shape[0]
    rows, weights, scratch = _expert_specs(d, ff)
    side = lambda i, be, br: jnp.minimum(i, side_steps - 1)
    nxt = lambda i, be, br: jnp.minimum(i + 1, side_steps - 1)
    nxt2 = lambda i, be, br: jnp.minimum(i + 2, side_steps - 1)
    return pl.pallas_call(
        functools.partial(_experts_in_kernel, side_steps),
        out_shape=(jax.ShapeDtypeStruct(xs.shape, xs.dtype), jax.ShapeDtypeStruct((n, d), jnp.float32)),
        grid_spec=pltpu.PrefetchScalarGridSpec(
            num_scalar_prefetch=2,
            grid=(xs.shape[0] // (EXPERT_ROWS * ROW_TILE),),
            in_specs=[rows] + weights + [
                _offsets_spec(SIDE_ROWS, lambda i, be, br: (side(i, be, br), 0, 0)),
                _offsets_spec(SIDE_ROWS, lambda i, be, br: (nxt(i, be, br), 0, 0)),
                _offsets_spec(SIDE_ROWS, lambda i, be, br: (nxt2(i, be, br), 0, 0)),
                pl.BlockSpec((SIDE_ROWS, d), lambda i, be, br: (side(i, be, br), 0)),
                pl.BlockSpec((SIDE_ROWS, TOP_K), lambda i, be, br: (side(i, be, br), 0)),
                pl.BlockSpec((1, d), lambda i, be, br: (0, 0)),
                pl.BlockSpec(memory_space=pl.ANY)],
            out_specs=(rows, pl.BlockSpec((SIDE_ROWS, d), lambda i, be, br: (side(i, be, br), 0))),
            scratch_shapes=scratch + [pltpu.VMEM((2, TOP_K * SIDE_ROWS * ROW_TILE, LANES), xs.dtype),
                                      pltpu.SemaphoreType.DMA((2,))],
        ),
        compiler_params=pltpu.CompilerParams(
            dimension_semantics=("arbitrary",), vmem_limit_bytes=VMEM_LIMIT),
        name="experts_fetch",
    )(blk_e, blk_rows, xs, wg, wu, wd, off, off, off, base, gates, gfin, y_prev)


def _combine_kernel(off_ref, off_nx_ref, base_ref, gates_ref, gfin_ref, y_hbm, out_in_ref, out_ref, buf_ref, sem):
    del out_in_ref
    i = pl.program_id(0)
    n_steps = pl.num_programs(0)
    n_tok = base_ref.shape[0]

    def issue(off_r, half):
        def body(g, carry):
            for u in range(ISSUE_UNROLL):
                _start_rows_in(g * ISSUE_UNROLL + u, n_tok, y_hbm, off_r, buf_ref.at[half], sem.at[half])
            return carry
        lax.fori_loop(0, n_tok // ISSUE_UNROLL, body, 0)

    half = i % 2

    @pl.when(i == 0)
    def _():
        issue(off_ref, 0)

    @pl.when(i + 1 < n_steps)
    def _():
        issue(off_nx_ref, 1 - half)

    _wait_rows_in(y_hbm, buf_ref.at[half], sem.at[half])
    _weighted_sum(base_ref, gates_ref, gfin_ref, buf_ref.at[half], out_ref)


def _combine(off, base, gates, gfin, y, out_prev, first_block):
    n, d = base.shape
    t = COMBINE_ROWS
    n_steps = off.shape[0]
    tok = lambda width: pl.BlockSpec((t, width), lambda i: (first_block + i, 0))
    return pl.pallas_call(
        _combine_kernel,
        out_shape=jax.ShapeDtypeStruct((n, d), jnp.float32),
        grid=(n_steps,),
        in_specs=[_offsets_spec(t, lambda i: (i, 0, 0)),
                  _offsets_spec(t, lambda i: (jnp.minimum(i + 1, n_steps - 1), 0, 0)),
                  tok(d), tok(TOP_K), pl.BlockSpec((1, d), lambda i: (0, 0)),
                  pl.BlockSpec(memory_space=pl.ANY), pl.BlockSpec(memory_space=pl.ANY)],
        out_specs=tok(d),
        scratch_shapes=[pltpu.VMEM((2, TOP_K * t * ROW_TILE, LANES), y.dtype),
                        pltpu.SemaphoreType.DMA((2,))],
        input_output_aliases={6: 0},
        compiler_params=pltpu.CompilerParams(
            dimension_semantics=("arbitrary",), vmem_limit_bytes=VMEM_LIMIT),
        name="combine",
    )(off, off, base, gates, gfin, y, out_prev)


def _group_layout(counts, idx, rank):
    r = EXPERT_ROWS
    n_g = idx.shape[1]
    n_blocks = (n_g * TOP_K) // r + N_EXPERTS
    padded = (counts + r - 1) // r * r
    ends = jnp.cumsum(padded)
    starts = (ends - padded).astype(jnp.int32)
    blk_row0 = jnp.arange(n_blocks, dtype=jnp.int32) * r
    blk_e = jnp.minimum(jnp.sum((ends[None, :] <= blk_row0[:, None]).astype(jnp.int32), axis=1), N_EXPERTS - 1)
    onehot = (blk_e[:, None] == jnp.arange(N_EXPERTS, dtype=jnp.int32)[None, :]).astype(jnp.int32)
    blk_rows = jnp.clip(onehot @ counts - (blk_row0 - onehot @ starts), 0, r).astype(jnp.int32)
    dest = rank + jnp.sum(jnp.where(idx[None] == jnp.arange(N_EXPERTS, dtype=jnp.int32)[:, None, None],
                                    starts[:, None, None], 0), axis=0)
    return blk_e, blk_rows, (dest * ROW_TILE).T.reshape(-1), n_blocks * r


def kernel(x, meta_tokens, norm_mix_g, w_in, lb_table, hgrn_norm_g, conv_w, conv_norm_g, w_out,
           norm_ffn_g, w_router, router_bias, w_gate_e, w_up_e, w_down_e, w_gate_s, w_up_s, w_down_s,
           norm_final_g):
    bsz, seq, d = x.shape
    n = bsz * seq
    n_g = n // TOKEN_GROUPS
    bf = jnp.bfloat16
    assert TOKEN_GROUPS == 2
    assert seq % MIX_ROWS == 0 and MIX_ROWS % CHUNK == 0
    assert n_g % ROUTE_ROWS == 0 and n_g % DISPATCH_ROWS == 0 and n_g % COMBINE_ROWS == 0
    assert (n_g * TOP_K) % EXPERT_ROWS == 0 and n_g % SIDE_ROWS == 0
    assert 2 <= n_g // SIDE_ROWS < (n_g * TOP_K) // EXPERT_ROWS + N_EXPERTS

    meta_pad = jnp.zeros((CHUNK, d), jnp.float32).at[CHUNK - N_META:].set(meta_tokens)
    h1 = _mixer(x, meta_pad, norm_mix_g[0:1], w_in[0].astype(bf), lb_table, hgrn_norm_g[0:1],
                conv_w[0], conv_norm_g[0:1], w_out[0].astype(bf))

    base, xn, idx, gate, rank, counts = _router(
        h1.reshape(n, d), norm_ffn_g[0:1], w_router[0].T, router_bias[0][:, None],
        w_gate_s[0].astype(bf), w_up_s[0].astype(bf), w_down_s[0].astype(bf))

    lay = [_group_layout(counts[g, :, 0], idx[:, g * n_g:(g + 1) * n_g], rank[:, g * n_g:(g + 1) * n_g])
           for g in range(TOKEN_GROUPS)]
    (be0, br0, off0, rows0), (be1, br1, off1, rows1) = lay
    steps = lambda off, t: off.reshape(n_g // t, 1, t * TOP_K)
    gates = gate.T
    gfin = norm_final_g[None, :]
    wg, wu, wd = w_gate_e[0], w_up_e[0], w_down_e[0]

    xs0 = _dispatch(steps(off0, DISPATCH_ROWS), xn, rows0, 0)
    y0, xs1 = _experts_out(be0, br0, xs0, wg, wu, wd, steps(off1, SIDE_ROWS), xn, n_g // SIDE_ROWS, rows1)
    y1, out = _experts_in(be1, br1, xs1, wg, wu, wd, steps(off0, SIDE_ROWS), base, gates, gfin, y0)
    out = _combine(steps(off1, COMBINE_ROWS), base, gates, gfin, y1, out, n_g // COMBINE_ROWS)
    return out.reshape(bsz, seq, d)
```

```python
import functools

import numpy as np
import jax
import jax.numpy as jnp
from jax import lax
from jax.experimental import pallas as pl
from jax.experimental.pallas import tpu as pltpu

N_META = 16
CHUNK = 128
HEADS = 4
HEAD_DIM = 128
HGRN_W = HEADS * HEAD_DIM
CONV_W = 512
CONV_GROUPS = 4
CONV_K = 3
N_EXPERTS = 64
TOP_K = 8
ROUTED_SCALE = 2.5
EPS = 1e-6

V7X_VMEM_BYTES = 64 * 1024 * 1024
VMEM_LIMIT = V7X_VMEM_BYTES - 8 * 1024 * 1024

MIX_ROWS = 512
CHUNK_UNROLL = 4
ROUTE_ROWS = 512
DISPATCH_ROWS = 512
EXPERT_ROWS = 1024
COMBINE_ROWS = 256
DMA_QUEUES = 2
ISSUE_UNROLL = 8
TOKEN_GROUPS = 2
SIDE_ROWS = 128

LANES = 128
ROW_TILE = 4

HALF_SPANS = (64, 32, 16, 8, 4, 2, 1)
N_LEVELS = len(HALF_SPANS) + 1


def _decay_sum_matrix():
    a = np.zeros((N_LEVELS, CHUNK, CHUNK), np.float32)
    a[0] = np.tril(np.ones((CHUNK, CHUNK), np.float32))
    for i, m in enumerate(HALF_SPANS):
        for t in range(CHUNK):
            mid = (t // (2 * m)) * 2 * m + m
            if t >= mid:
                a[1 + i, t, mid:t + 1] = 1.0
            else:
                a[1 + i, t, t + 1:mid] = 1.0
    return a.reshape(N_LEVELS * CHUNK, CHUNK)


def _level_matrix():
    lv = np.full((CHUNK, CHUNK), -1, np.int32)
    for t in range(CHUNK):
        lv[t, t] = len(HALF_SPANS)
        for s in range(t):
            top = (t ^ s).bit_length() - 1
            lv[t, s] = HALF_SPANS.index(1 << top)
    return lv


def _rms(x, g):
    return x * lax.rsqrt(jnp.mean(x * x, axis=-1, keepdims=True) + EPS) * g


def _group_rms(x, g, width):
    outs = []
    for j in range(x.shape[-1] // width):
        xs = x[:, j * width:(j + 1) * width]
        outs.append(xs * lax.rsqrt(jnp.mean(xs * xs, axis=-1, keepdims=True) + EPS))
    return jnp.concatenate(outs, axis=-1) * g


def _sigmoid_pair(z):
    t = jnp.exp(-jnp.abs(z))
    inv = 1.0 / (1.0 + t)
    big, small = inv, t * inv
    pos = z >= 0
    return jnp.where(pos, big, small), jnp.where(pos, small, big)


def _pack_rows(x):
    half = x.shape[1] // 2
    bits = lambda v: lax.bitcast_convert_type(v.astype(jnp.bfloat16).astype(jnp.float32), jnp.uint32)
    return (bits(x[:, :half]) >> 16) | (bits(x[:, half:]) & jnp.uint32(0xFFFF0000))


def _unpack_words(w):
    lo = lax.bitcast_convert_type(w << 16, jnp.float32)
    hi = lax.bitcast_convert_type(w & jnp.uint32(0xFFFF0000), jnp.float32)
    return lo, hi


def _store_row_tiles(ref, words):
    t = words.shape[0]
    for c in range(ROW_TILE):
        ref[pl.ds(c, t, stride=ROW_TILE), :] = words[:, c * LANES:(c + 1) * LANES]


def _load_row_tiles(ref, t):
    parts = [_unpack_words(ref[pl.ds(c, t, stride=ROW_TILE), :]) for c in range(ROW_TILE)]
    return [p[0] for p in parts], [p[1] for p in parts]


def _dot(a, b):
    return jnp.dot(a, b, preferred_element_type=jnp.float32)


def _dot_nt(a, b):
    return lax.dot_general(a, b, (((1,), (1,)), ((), ())), preferred_element_type=jnp.float32)


def _dot_tn(a, b):
    return lax.dot_general(a, b, (((0,), (0,)), ((), ())), preferred_element_type=jnp.float32)


def _hgrn_chunk(q, z, iv, lb, amat, level, st_ref, first_valid_row):
    sig, sig_neg = _sigmoid_pair(z)
    lf = jnp.log(lb + (1.0 - lb) * sig)
    k = (1.0 - lb) * sig_neg
    row = lax.broadcasted_iota(jnp.int32, (CHUNK, HGRN_W), 0)
    if first_valid_row:
        valid = row >= first_valid_row
        lf = jnp.where(valid, lf, 0.0)
        k = jnp.where(valid, k, 0.0)

    h1 = lf.astype(jnp.bfloat16)
    h2 = (lf - h1.astype(jnp.float32)).astype(jnp.bfloat16)
    e_all = _dot(amat, jnp.concatenate([h1, h2], axis=0))

    b = e_all[0:CHUNK]
    b_last = b[CHUNK - 1:CHUNK]
    q_in = (q * jnp.exp(b)).astype(jnp.bfloat16)
    k_out = (k * jnp.exp(b_last - b)).astype(jnp.bfloat16)
    st_decay = jnp.exp(b_last)
    v_bf = iv.astype(jnp.bfloat16)

    q_lv = [q.astype(jnp.bfloat16)]
    k_lv = [k.astype(jnp.bfloat16)]
    for i, m in enumerate(HALF_SPANS):
        ex = jnp.exp(e_all[(1 + i) * CHUNK:(2 + i) * CHUNK])
        right = (row & m) != 0
        q_lv.append(jnp.where(right, q * ex, 0.0).astype(jnp.bfloat16))
        k_lv.append(jnp.where(right, 0.0, k * ex).astype(jnp.bfloat16))
    lv_of = [len(HALF_SPANS)] + list(range(len(HALF_SPANS)))

    outs = []
    for h in range(HEADS):
        cols = slice(h * HEAD_DIM, (h + 1) * HEAD_DIM)
        scores = jnp.zeros((CHUNK, CHUNK), jnp.float32)
        for ql, kl, lv in zip(q_lv, k_lv, lv_of):
            scores = jnp.where(level == lv, _dot_nt(ql[:, cols], kl[:, cols]), scores)
        st = st_ref[h]
        o = _dot(scores.astype(jnp.bfloat16), v_bf[:, cols]) + _dot_nt(q_in[:, cols], st.astype(jnp.bfloat16))
        st_ref[h] = st * st_decay[:, cols] + _dot_tn(v_bf[:, cols], k_out[:, cols])
        outs.append(o)
    return jnp.concatenate(outs, axis=-1)


def _mixer_kernel(with_send, x_ref, meta_ref, gmix_ref, win_ref, lbt_ref, ghg_ref, cw_ref, gcv_ref, wout_ref,
                  amat_ref, level_ref, *rest):
    if with_send:
        off_ref, rows_ref, h1_ref, dst_hbm, proj_ref, o_ref, u_ref, st_ref, stage_ref, sem = rest
    else:
        h1_ref, proj_ref, o_ref, u_ref, st_ref = rest
    j = pl.program_id(1)
    rows = x_ref.shape[0]
    n_in = win_ref.shape[1]

    lbt = lbt_ref[...]
    lbe = jnp.exp(lbt - jnp.max(lbt, axis=0, keepdims=True))
    lb = lbe[0:1] / jnp.sum(lbe, axis=0, keepdims=True)

    amat = amat_ref[...]
    level = level_ref[...]
    gmix = gmix_ref[...]

    def project(xv, dst_rows):
        xn = _rms(xv, gmix).astype(jnp.bfloat16)
        for c0 in range(0, n_in, 512):
            proj_ref[dst_rows, c0:c0 + 512] = _dot(xn, win_ref[:, c0:c0 + 512])

    @pl.when(j == 0)
    def _():
        st_ref[...] = jnp.zeros_like(st_ref)
        project(meta_ref[...], pl.ds(0, CHUNK))
        pm = proj_ref[0:CHUNK, :]
        _hgrn_chunk(pm[:, 0:512], pm[:, 512:1024], pm[:, 1024:1536], lb, amat, level, st_ref,
                    CHUNK - N_META)
        u_ref[0:8, :] = (pm[:, 2560:3072] * pm[:, 3072:3584])[CHUNK - 8:CHUNK]

    if with_send:
        step = pl.program_id(0) * pl.num_programs(1) + j
        slot = step % 2
        stage_ref[slot] = rows_ref[...]
        for tok in range(rows_ref.shape[0] // ROW_TILE):
            _start_rows_out(tok, stage_ref.at[slot], off_ref, dst_hbm, sem.at[slot])

    project(x_ref[...], pl.ds(0, rows))

    def chunk_body(c, carry):
        for u in range(CHUNK_UNROLL):
            r0 = pl.multiple_of((c * CHUNK_UNROLL + u) * CHUNK, CHUNK)
            q = proj_ref[pl.ds(r0, CHUNK), 0:512]
            z = proj_ref[pl.ds(r0, CHUNK), 512:1024]
            iv = proj_ref[pl.ds(r0, CHUNK), 1024:1536]
            o_ref[pl.ds(r0, CHUNK), :] = _hgrn_chunk(q, z, iv, lb, amat, level, st_ref, 0)
        return carry

    lax.fori_loop(0, rows // (CHUNK * CHUNK_UNROLL), chunk_body, 0)

    g_out = proj_ref[:, 1536:2048]
    g_sig, _ = _sigmoid_pair(g_out)
    y_hgrn = _group_rms(o_ref[...], ghg_ref[...], HEAD_DIM) * (g_out * g_sig)

    u = proj_ref[:, 2560:3072] * proj_ref[:, 3072:3584]
    u_ref[8:8 + rows, :] = u
    cw = cw_ref[...]
    y = cw[2:3] * u + cw[1:2] * u_ref[7:7 + rows, :] + cw[0:1] * u_ref[6:6 + rows, :]
    u_ref[0:8, :] = u[rows - 8:rows]
    y_conv = _group_rms(proj_ref[:, 2048:2560] * y, gcv_ref[...], CONV_W // CONV_GROUPS)

    mixed = jnp.concatenate([y_hgrn, y_conv], axis=-1).astype(jnp.bfloat16)
    h1_ref[...] = x_ref[...] + _dot(mixed, wout_ref[...])

    if with_send:
        @pl.when(step >= 1)
        def _():
            _wait_rows_out(stage_ref.at[1 - slot], dst_hbm, sem.at[1 - slot])

        @pl.when(step == pl.num_programs(0) * pl.num_programs(1) - 1)
        def _():
            _wait_rows_out(stage_ref.at[slot], dst_hbm, sem.at[slot])


def _mixer(x, batch0, n_batch, meta_pad, gmix, w_in, lb_table, ghg, conv_w, gcv, w_out, send=None):
    _, seq, d = x.shape
    n_in = w_in.shape[1]
    rows = MIX_ROWS
    n_j = seq // rows
    const = lambda *shape: pl.BlockSpec(shape, lambda b, j: (0,) * len(shape))
    in_specs = [
        pl.BlockSpec((None, rows, d), lambda b, j: (batch0 + b, j, 0)),
        const(CHUNK, d), const(1, d), const(d, n_in), const(*lb_table.shape), const(1, HGRN_W),
        const(CONV_K, CONV_W), const(1, CONV_W), const(d, d),
        const(N_LEVELS * CHUNK, 2 * CHUNK), const(CHUNK, CHUNK),
    ]
    args = [x, meta_pad, gmix, w_in, lb_table, ghg, conv_w, gcv, w_out,
            jnp.asarray(np.tile(_decay_sum_matrix(), (1, 2)), jnp.bfloat16), jnp.asarray(_level_matrix())]
    out_shape = jax.ShapeDtypeStruct((n_batch, seq, d), jnp.float32)
    out_specs = pl.BlockSpec((None, rows, d), lambda b, j: (b, j, 0))
    scratch = [
        pltpu.VMEM((rows, n_in), jnp.float32),
        pltpu.VMEM((rows, HGRN_W), jnp.float32),
        pltpu.VMEM((rows + 8, CONV_W), jnp.float32),
        pltpu.VMEM((HEADS, HEAD_DIM, HEAD_DIM), jnp.float32),
    ]
    if send is not None:
        off, xn, n_rows = send
        assert off.shape[0] == n_batch * n_j and off.shape[2] == rows * TOP_K
        in_specs += [_offsets_spec(rows, lambda b, j: (b * n_j + j, 0, 0)),
                     pl.BlockSpec((rows * ROW_TILE, LANES), lambda b, j: (b * n_j + j, 0))]
        args += [off, xn]
        out_shape = (out_shape, jax.ShapeDtypeStruct((n_rows * ROW_TILE, LANES), xn.dtype))
        out_specs = (out_specs, pl.BlockSpec(memory_space=pl.ANY))
        scratch += [pltpu.VMEM((2, rows * ROW_TILE, LANES), xn.dtype), pltpu.SemaphoreType.DMA((2,))]
    return pl.pallas_call(
        functools.partial(_mixer_kernel, send is not None),
        out_shape=out_shape,
        grid=(n_batch, n_j),
        in_specs=in_specs,
        out_specs=out_specs,
        scratch_shapes=scratch,
        compiler_params=pltpu.CompilerParams(
            dimension_semantics=("arbitrary", "arbitrary"), vmem_limit_bytes=VMEM_LIMIT),
        name="mixer_send" if send is not None else "mixer",
    )(*args)


def _router_kernel(h1_ref, gffn_ref, wr_ref, bias_ref, wgs_ref, wus_ref, wds_ref, tri_ref,
                   base_ref, xn_ref, idx_ref, gate_ref, rank_ref, counts_ref, carry_ref):
    i = pl.program_id(0)
    n_tok = h1_ref.shape[0]

    @pl.when(i == 0)
    def _():
        carry_ref[...] = jnp.zeros_like(carry_ref)

    h1 = h1_ref[...]
    xn = _rms(h1, gffn_ref[...])
    _store_row_tiles(xn_ref, _pack_rows(xn))
    xb = xn.astype(jnp.bfloat16)

    g_pre = _dot(xb, wgs_ref[...])
    gate_s, _ = _sigmoid_pair(g_pre)
    hid = (g_pre * gate_s) * _dot(xb, wus_ref[...])
    base_ref[...] = h1 + _dot(hid.astype(jnp.bfloat16), wds_ref[...])

    logits = lax.dot_general(wr_ref[...], xn, (((1,), (1,)), ((), ())),
                             precision=lax.Precision.HIGHEST, preferred_element_type=jnp.float32)
    scores, _ = _sigmoid_pair(logits)
    sel = scores + bias_ref[...]
    eid = lax.broadcasted_iota(jnp.int32, (N_EXPERTS, n_tok), 0).astype(jnp.float32)
    picks, top_s = [], []
    for _ in range(TOP_K):
        best = jnp.max(sel, axis=0, keepdims=True)
        pick = jnp.min(jnp.where(sel == best, eid, float(N_EXPERTS)), axis=0, keepdims=True)
        hit = eid == pick
        top_s.append(jnp.sum(jnp.where(hit, scores, 0.0), axis=0, keepdims=True))
        sel = jnp.where(hit, -jnp.inf, sel)
        picks.append(pick)
    top_s = jnp.concatenate(top_s, axis=0)
    gate_ref[...] = top_s / jnp.sum(top_s, axis=0, keepdims=True) * ROUTED_SCALE
    idx_ref[...] = jnp.concatenate(picks, axis=0).astype(jnp.int32)

    chosen = jnp.zeros((N_EXPERTS, n_tok), jnp.float32)
    for pick in picks:
        chosen = chosen + jnp.where(eid == pick, 1.0, 0.0)
    incl = _dot(chosen.astype(jnp.bfloat16), tri_ref[...])
    before = carry_ref[...] + incl - 1.0
    ranks = [jnp.sum(jnp.where(eid == pick, before, 0.0), axis=0, keepdims=True) for pick in picks]
    rank_ref[...] = jnp.concatenate(ranks, axis=0).astype(jnp.int32)
    carry_ref[...] = carry_ref[...] + incl[:, n_tok - 1:n_tok]
    counts_ref[...] = jnp.broadcast_to(carry_ref[...], counts_ref.shape).astype(jnp.int32)


def _router(h1, gffn, w_router, bias, wgs, wus, wds):
    n, d = h1.shape
    t = ROUTE_ROWS
    ff = wgs.shape[1]
    tri = jnp.asarray(np.triu(np.ones((t, t), np.float32)), jnp.bfloat16)
    const = lambda *shape: pl.BlockSpec(shape, lambda i: (0,) * len(shape))
    tok = lambda width: pl.BlockSpec((t, width), lambda i: (i, 0))
    slot = pl.BlockSpec((TOP_K, t), lambda i: (0, i))
    return pl.pallas_call(
        _router_kernel,
        out_shape=(
            jax.ShapeDtypeStruct((n, d), jnp.float32),
            jax.ShapeDtypeStruct((n * ROW_TILE, LANES), jnp.uint32),
            jax.ShapeDtypeStruct((TOP_K, n), jnp.int32),
            jax.ShapeDtypeStruct((TOP_K, n), jnp.float32),
            jax.ShapeDtypeStruct((TOP_K, n), jnp.int32),
            jax.ShapeDtypeStruct((N_EXPERTS, 128), jnp.int32),
        ),
        grid=(n // t,),
        in_specs=[tok(d), const(1, d), const(N_EXPERTS, d), const(N_EXPERTS, 1),
                  const(d, ff), const(d, ff), const(ff, d), const(t, t)],
        out_specs=(tok(d), pl.BlockSpec((t * ROW_TILE, LANES), lambda i: (i, 0)), slot, slot, slot,
                   const(N_EXPERTS, 128)),
        scratch_shapes=[pltpu.VMEM((N_EXPERTS, 1), jnp.float32)],
        compiler_params=pltpu.CompilerParams(
            dimension_semantics=("arbitrary",), vmem_limit_bytes=VMEM_LIMIT),
        name="router",
    )(h1, gffn, w_router, bias, wgs, wus, wds, tri)


def _slab(ref, first_sublane):
    if not isinstance(first_sublane, int):
        first_sublane = pl.multiple_of(first_sublane, ROW_TILE)
    return ref.at[pl.ds(first_sublane, ROW_TILE)]


def _start_rows_out(tok, rows_ref, off_ref, dst_hbm, sem):
    for s in range(TOP_K):
        pltpu.make_async_copy(_slab(rows_ref, tok * ROW_TILE), _slab(dst_hbm, off_ref[0, tok * TOP_K + s]),
                              sem).start(priority=s % DMA_QUEUES)


def _wait_rows_out(rows_ref, dst_hbm, sem):
    for s in range(TOP_K):
        pltpu.make_async_copy(rows_ref, dst_hbm.at[pl.ds(0, rows_ref.shape[0])], sem).wait()


def _start_rows_in(tok, n_tok, src_hbm, off_ref, buf_ref, sem):
    for s in range(TOP_K):
        pltpu.make_async_copy(_slab(src_hbm, off_ref[0, tok * TOP_K + s]),
                              _slab(buf_ref, (s * n_tok + tok) * ROW_TILE), sem).start(priority=s % DMA_QUEUES)


def _wait_rows_in(src_hbm, buf_ref, sem):
    pltpu.make_async_copy(src_hbm.at[pl.ds(0, buf_ref.shape[0])], buf_ref, sem).wait()


def _weighted_sum(base_ref, gates_ref, gfin_ref, buf_ref, out_ref):
    n_tok = base_ref.shape[0]
    gates = gates_ref[...]
    width = ROW_TILE * LANES
    lo_chunks, hi_chunks = [], []
    for c in range(ROW_TILE):
        acc_lo = base_ref[:, c * LANES:(c + 1) * LANES]
        acc_hi = base_ref[:, width + c * LANES:width + (c + 1) * LANES]
        for s in range(TOP_K):
            lo, hi = _unpack_words(buf_ref[pl.ds(s * n_tok * ROW_TILE + c, n_tok, stride=ROW_TILE), :])
            acc_lo = acc_lo + gates[:, s:s + 1] * lo
            acc_hi = acc_hi + gates[:, s:s + 1] * hi
        lo_chunks.append(acc_lo)
        hi_chunks.append(acc_hi)
    out_ref[...] = _rms(jnp.concatenate(lo_chunks + hi_chunks, axis=-1), gfin_ref[...])


def _offsets_spec(n_tok, index_map):
    return pl.BlockSpec((None, 1, n_tok * TOP_K), index_map, memory_space=pltpu.SMEM)


def _swiglu_block(n_valid, xs_ref, wgb_ref, wub_ref, wdb_ref, y_ref):
    r = EXPERT_ROWS
    keep = lax.broadcasted_iota(jnp.int32, (r, LANES), 0) < n_valid
    lo, hi = _load_row_tiles(xs_ref, r)
    xb = jnp.concatenate([jnp.where(keep, c, 0.0).astype(jnp.bfloat16) for c in lo + hi], axis=-1)
    g = _dot(xb, wgb_ref[...])
    u = _dot(xb, wub_ref[...])
    g_sig, _ = _sigmoid_pair(g)
    hid = ((g * g_sig) * u).astype(jnp.bfloat16)
    _store_row_tiles(y_ref, _pack_rows(_dot(hid, wdb_ref[...])))


def _refresh_weights(i, blk_e_ref, wg_ref, wu_ref, wd_ref, wgb_ref, wub_ref, wdb_ref):
    @pl.when((i == 0) | (blk_e_ref[i] != blk_e_ref[jnp.maximum(i - 1, 0)]))
    def _():
        wgb_ref[...] = wg_ref[...].astype(jnp.bfloat16)
        wub_ref[...] = wu_ref[...].astype(jnp.bfloat16)
        wdb_ref[...] = wd_ref[...].astype(jnp.bfloat16)


def _experts_out_kernel(side_steps, blk_e_ref, blk_rows_ref, xs_ref, wg_ref, wu_ref, wd_ref, off_ref, rows_ref,
                        y_ref, dst_hbm, wgb_ref, wub_ref, wdb_ref, stage_ref, sem):
    i = pl.program_id(0)
    n_valid = blk_rows_ref[i]
    side = i < side_steps
    slot = i % 2
    _refresh_weights(i, blk_e_ref, wg_ref, wu_ref, wd_ref, wgb_ref, wub_ref, wdb_ref)

    def send():
        stage_ref[slot] = rows_ref[...]
        for tok in range(SIDE_ROWS):
            _start_rows_out(tok, stage_ref.at[slot], off_ref, dst_hbm, sem.at[slot])

    @pl.when((n_valid > 0) & side)
    def _():
        send()
        _swiglu_block(n_valid, xs_ref, wgb_ref, wub_ref, wdb_ref, y_ref)

    @pl.when((n_valid > 0) & jnp.logical_not(side))
    def _():
        _swiglu_block(n_valid, xs_ref, wgb_ref, wub_ref, wdb_ref, y_ref)

    @pl.when((n_valid == 0) & side)
    def _():
        send()
        y_ref[...] = jnp.zeros_like(y_ref)

    @pl.when((n_valid == 0) & jnp.logical_not(side))
    def _():
        y_ref[...] = jnp.zeros_like(y_ref)

    @pl.when((i >= 1) & (i <= side_steps))
    def _():
        _wait_rows_out(stage_ref.at[1 - slot], dst_hbm, sem.at[1 - slot])


def _experts_in_kernel(side_steps, blk_e_ref, blk_rows_ref, xs_ref, wg_ref, wu_ref, wd_ref,
                       off_ref, off_nx_ref, off_nx2_ref, base_ref, gates_ref, gfin_ref, src_hbm, y_ref, out_ref,
                       wgb_ref, wub_ref, wdb_ref, buf_ref, sem):
    i = pl.program_id(0)
    n_valid = blk_rows_ref[i]
    slot = i % 2
    _refresh_weights(i, blk_e_ref, wg_ref, wu_ref, wd_ref, wgb_ref, wub_ref, wdb_ref)

    def fetch(off_r, into):
        for tok in range(SIDE_ROWS):
            _start_rows_in(tok, SIDE_ROWS, src_hbm, off_r, buf_ref.at[into], sem.at[into])

    @pl.when(i == 0)
    def _():
        fetch(off_ref, 0)
        fetch(off_nx_ref, 1)

    @pl.when(i < side_steps)
    def _():
        _wait_rows_in(src_hbm, buf_ref.at[slot], sem.at[slot])
        _weighted_sum(base_ref, gates_ref, gfin_ref, buf_ref.at[slot], out_ref)

    more = i + 2 < side_steps

    @pl.when((n_valid > 0) & more)
    def _():
        fetch(off_nx2_ref, slot)
        _swiglu_block(n_valid, xs_ref, wgb_ref, wub_ref, wdb_ref, y_ref)

    @pl.when((n_valid > 0) & jnp.logical_not(more))
    def _():
        _swiglu_block(n_valid, xs_ref, wgb_ref, wub_ref, wdb_ref, y_ref)

    @pl.when((n_valid == 0) & more)
    def _():
        fetch(off_nx2_ref, slot)
        y_ref[...] = jnp.zeros_like(y_ref)

    @pl.when((n_valid == 0) & jnp.logical_not(more))
    def _():
        y_ref[...] = jnp.zeros_like(y_ref)


def _expert_specs(d, ff):
    r = EXPERT_ROWS
    rows = pl.BlockSpec((r * ROW_TILE, LANES), lambda i, be, br: (i, 0))
    weights = [pl.BlockSpec((None, d, ff), lambda i, be, br: (be[i], 0, 0)),
               pl.BlockSpec((None, d, ff), lambda i, be, br: (be[i], 0, 0)),
               pl.BlockSpec((None, ff, d), lambda i, be, br: (be[i], 0, 0))]
    scratch = [pltpu.VMEM((d, ff), jnp.bfloat16), pltpu.VMEM((d, ff), jnp.bfloat16),
               pltpu.VMEM((ff, d), jnp.bfloat16)]
    return rows, weights, scratch


def _experts_out(blk_e, blk_rows, xs, wg, wu, wd, off, xn, first_block, n_rows_next):
    side_steps = off.shape[0]
    d, ff = wg.shape[1], wg.shape[2]
    rows, weights, scratch = _expert_specs(d, ff)
    side = lambda i, be, br: jnp.minimum(i, side_steps - 1)
    return pl.pallas_call(
        functools.partial(_experts_out_kernel, side_steps),
        out_shape=(jax.ShapeDtypeStruct(xs.shape, xs.dtype),
                   jax.ShapeDtypeStruct((n_rows_next * ROW_TILE, LANES), xs.dtype)),
        grid_spec=pltpu.PrefetchScalarGridSpec(
            num_scalar_prefetch=2,
            grid=(xs.shape[0] // (EXPERT_ROWS * ROW_TILE),),
            in_specs=[rows] + weights + [
                _offsets_spec(SIDE_ROWS, lambda i, be, br: (side(i, be, br), 0, 0)),
                pl.BlockSpec((SIDE_ROWS * ROW_TILE, LANES), lambda i, be, br: (first_block + side(i, be, br), 0))],
            out_specs=(rows, pl.BlockSpec(memory_space=pl.ANY)),
            scratch_shapes=scratch + [pltpu.VMEM((2, SIDE_ROWS * ROW_TILE, LANES), xs.dtype),
                                      pltpu.SemaphoreType.DMA((2,))],
        ),
        compiler_params=pltpu.CompilerParams(
            dimension_semantics=("arbitrary",), vmem_limit_bytes=VMEM_LIMIT),
        name="experts_send",
    )(blk_e, blk_rows, xs, wg, wu, wd, off, xn)


def _experts_in(blk_e, blk_rows, xs, wg, wu, wd, off, base, gates, gfin, y_prev, n):
    side_steps = off.shape[0]
    d, ff = wg.shape[1], wg.shape[2]
    rows, weights, scratch = _expert_specs(d, ff)
    side = lambda i, be, br: jnp.minimum(i, side_steps - 1)
    nxt = lambda i, be, br: jnp.minimum(i + 1, side_steps - 1)
    nxt2 = lambda i, be, br: jnp.minimum(i + 2, side_steps - 1)
    return pl.pallas_call(
        functools.partial(_experts_in_kernel, side_steps),
        out_shape=(jax.ShapeDtypeStruct(xs.shape, xs.dtype), jax.ShapeDtypeStruct((n, d), jnp.float32)),
        grid_spec=pltpu.PrefetchScalarGridSpec(
            num_scalar_prefetch=2,
            grid=(xs.shape[0] // (EXPERT_ROWS * ROW_TILE),),
            in_specs=[rows] + weights + [
                _offsets_spec(SIDE_ROWS, lambda i, be, br: (side(i, be, br), 0, 0)),
                _offsets_spec(SIDE_ROWS, lambda i, be, br: (nxt(i, be, br), 0, 0)),
                _offsets_spec(SIDE_ROWS, lambda i, be, br: (nxt2(i, be, br), 0, 0)),
                pl.BlockSpec((SIDE_ROWS, d), lambda i, be, br: (side(i, be, br), 0)),
                pl.BlockSpec((SIDE_ROWS, TOP_K), lambda i, be, br: (side(i, be, br), 0)),
                pl.BlockSpec((1, d), lambda i, be, br: (0, 0)),
                pl.BlockSpec(memory_space=pl.ANY)],
            out_specs=(rows, pl.BlockSpec((SIDE_ROWS, d), lambda i, be, br: (side(i, be, br), 0))),
            scratch_shapes=scratch + [pltpu.VMEM((2, TOP_K * SIDE_ROWS * ROW_TILE, LANES), xs.dtype),
                                      pltpu.SemaphoreType.DMA((2,))],
        ),
        compiler_params=pltpu.CompilerParams(
            dimension_semantics=("arbitrary",), vmem_limit_bytes=VMEM_LIMIT),
        name="experts_fetch",
    )(blk_e, blk_rows, xs, wg, wu, wd, off, off, off, base, gates, gfin, y_prev)


def _combine_kernel(off_ref, off_nx_ref, base_ref, gates_ref, gfin_ref, y_hbm, out_in_ref, out_ref, buf_ref, sem):
    del out_in_ref
    i = pl.program_id(0)
    n_steps = pl.num_programs(0)
    n_tok = base_ref.shape[0]

    def issue(off_r, half):
        def body(g, carry):
            for u in range(ISSUE_UNROLL):
                _start_rows_in(g * ISSUE_UNROLL + u, n_tok, y_hbm, off_r, buf_ref.at[half], sem.at[half])
            return carry
        lax.fori_loop(0, n_tok // ISSUE_UNROLL, body, 0)

    half = i % 2

    @pl.when(i == 0)
    def _():
        issue(off_ref, 0)

    @pl.when(i + 1 < n_steps)
    def _():
        issue(off_nx_ref, 1 - half)

    _wait_rows_in(y_hbm, buf_ref.at[half], sem.at[half])
    _weighted_sum(base_ref, gates_ref, gfin_ref, buf_ref.at[half], out_ref)


def _combine(off, base, gates, gfin, y, out_prev, first_block):
    d = base.shape[1]
    t = COMBINE_ROWS
    n_steps = off.shape[0]
    tok = lambda width, first=0: pl.BlockSpec((t, width), lambda i: (first + i, 0))
    return pl.pallas_call(
        _combine_kernel,
        out_shape=jax.ShapeDtypeStruct(out_prev.shape, jnp.float32),
        grid=(n_steps,),
        in_specs=[_offsets_spec(t, lambda i: (i, 0, 0)),
                  _offsets_spec(t, lambda i: (jnp.minimum(i + 1, n_steps - 1), 0, 0)),
                  tok(d), tok(TOP_K), pl.BlockSpec((1, d), lambda i: (0, 0)),
                  pl.BlockSpec(memory_space=pl.ANY), pl.BlockSpec(memory_space=pl.ANY)],
        out_specs=tok(d, first_block),
        scratch_shapes=[pltpu.VMEM((2, TOP_K * t * ROW_TILE, LANES), y.dtype),
                        pltpu.SemaphoreType.DMA((2,))],
        input_output_aliases={6: 0},
        compiler_params=pltpu.CompilerParams(
            dimension_semantics=("arbitrary",), vmem_limit_bytes=VMEM_LIMIT),
        name="combine",
    )(off, off, base, gates, gfin, y, out_prev)


def _group_layout(counts, idx, rank):
    r = EXPERT_ROWS
    n_g = idx.shape[1]
    n_blocks = (n_g * TOP_K) // r + N_EXPERTS
    padded = (counts + r - 1) // r * r
    ends = jnp.cumsum(padded)
    starts = (ends - padded).astype(jnp.int32)
    blk_row0 = jnp.arange(n_blocks, dtype=jnp.int32) * r
    blk_e = jnp.minimum(jnp.sum((ends[None, :] <= blk_row0[:, None]).astype(jnp.int32), axis=1), N_EXPERTS - 1)
    onehot = (blk_e[:, None] == jnp.arange(N_EXPERTS, dtype=jnp.int32)[None, :]).astype(jnp.int32)
    blk_rows = jnp.clip(onehot @ counts - (blk_row0 - onehot @ starts), 0, r).astype(jnp.int32)
    dest = rank + jnp.sum(jnp.where(idx[None] == jnp.arange(N_EXPERTS, dtype=jnp.int32)[:, None, None],
                                    starts[:, None, None], 0), axis=0)
    return blk_e, blk_rows, (dest * ROW_TILE).T.reshape(-1), n_blocks * r


def kernel(x, meta_tokens, norm_mix_g, w_in, lb_table, hgrn_norm_g, conv_w, conv_norm_g, w_out,
           norm_ffn_g, w_router, router_bias, w_gate_e, w_up_e, w_down_e, w_gate_s, w_up_s, w_down_s,
           norm_final_g):
    bsz, seq, d = x.shape
    n = bsz * seq
    assert TOKEN_GROUPS == 2 and bsz % TOKEN_GROUPS == 0
    b_g = bsz // TOKEN_GROUPS
    n_g = n // TOKEN_GROUPS
    bf = jnp.bfloat16
    assert seq % MIX_ROWS == 0 and MIX_ROWS % CHUNK == 0
    assert n_g % ROUTE_ROWS == 0 and n_g % COMBINE_ROWS == 0
    assert (n_g * TOP_K) % EXPERT_ROWS == 0 and n_g % SIDE_ROWS == 0
    assert 2 <= n_g // SIDE_ROWS < (n_g * TOP_K) // EXPERT_ROWS + N_EXPERTS

    meta_pad = jnp.zeros((CHUNK, d), jnp.float32).at[CHUNK - N_META:].set(meta_tokens)
    mix = functools.partial(_mixer, meta_pad=meta_pad, gmix=norm_mix_g[0:1], w_in=w_in[0].astype(bf),
                            lb_table=lb_table, ghg=hgrn_norm_g[0:1], conv_w=conv_w[0], gcv=conv_norm_g[0:1],
                            w_out=w_out[0].astype(bf))
    route = functools.partial(_router, gffn=norm_ffn_g[0:1], w_router=w_router[0].T, bias=router_bias[0][:, None],
                              wgs=w_gate_s[0].astype(bf), wus=w_up_s[0].astype(bf), wds=w_down_s[0].astype(bf))
    steps = lambda off, t: off.reshape(n_g // t, 1, t * TOP_K)
    gfin = norm_final_g[None, :]
    wg, wu, wd = w_gate_e[0], w_up_e[0], w_down_e[0]

    h1_0 = mix(x, 0, b_g)
    base0, xn0, idx0, gate0, rank0, counts0 = route(h1_0.reshape(n_g, d))
    be0, br0, off0, rows0 = _group_layout(counts0[:, 0], idx0, rank0)
    h1_1, xs0 = mix(x, b_g, b_g, send=(steps(off0, MIX_ROWS), xn0, rows0))
    base1, xn1, idx1, gate1, rank1, counts1 = route(h1_1.reshape(n_g, d))
    be1, br1, off1, rows1 = _group_layout(counts1[:, 0], idx1, rank1)
    y0, xs1 = _experts_out(be0, br0, xs0, wg, wu, wd, steps(off1, SIDE_ROWS), xn1, 0, rows1)
    y1, out = _experts_in(be1, br1, xs1, wg, wu, wd, steps(off0, SIDE_ROWS), base0, gate0.T, gfin, y0, n)
    out = _combine(steps(off1, COMBINE_ROWS), base1, gate1.T, gfin, y1, out, n_g // COMBINE_ROWS)
    return out.reshape(bsz, seq, d)
```

```python
import functools

import numpy as np
import jax
import jax.numpy as jnp
from jax import lax
from jax.experimental import pallas as pl
from jax.experimental.pallas import tpu as pltpu

N_META = 16
CHUNK = 128
HEADS = 4
HEAD_DIM = 128
HGRN_W = HEADS * HEAD_DIM
CONV_W = 512
CONV_GROUPS = 4
CONV_K = 3
N_EXPERTS = 64
TOP_K = 8
ROUTED_SCALE = 2.5
EPS = 1e-6

V7X_VMEM_BYTES = 64 * 1024 * 1024
VMEM_LIMIT = V7X_VMEM_BYTES - 8 * 1024 * 1024

MIX_ROWS = 512
CHUNK_UNROLL = 4
ROUTE_ROWS = 512
EXPERT_ROWS = 1024
COMBINE_ROWS = 256
DMA_QUEUES = 2
ISSUE_UNROLL = 8
TOKEN_GROUPS = 2
SIDE_ROWS = 128

LANES = 128
ROW_TILE = 4

HALF_SPANS = (64, 32, 16, 8, 4, 2, 1)
N_LEVELS = len(HALF_SPANS) + 1


def _decay_sum_matrix():
    a = np.zeros((N_LEVELS, CHUNK, CHUNK), np.float32)
    a[0] = np.tril(np.ones((CHUNK, CHUNK), np.float32))
    for i, m in enumerate(HALF_SPANS):
        for t in range(CHUNK):
            mid = (t // (2 * m)) * 2 * m + m
            if t >= mid:
                a[1 + i, t, mid:t + 1] = 1.0
            else:
                a[1 + i, t, t + 1:mid] = 1.0
    return a.reshape(N_LEVELS * CHUNK, CHUNK)


def _level_matrix():
    lv = np.full((CHUNK, CHUNK), -1, np.int32)
    for t in range(CHUNK):
        lv[t, t] = len(HALF_SPANS)
        for s in range(t):
            top = (t ^ s).bit_length() - 1
            lv[t, s] = HALF_SPANS.index(1 << top)
    return lv


def _rms(x, g):
    return x * lax.rsqrt(jnp.mean(x * x, axis=-1, keepdims=True) + EPS) * g


def _group_rms(x, g, width):
    outs = []
    for j in range(x.shape[-1] // width):
        xs = x[:, j * width:(j + 1) * width]
        outs.append(xs * lax.rsqrt(jnp.mean(xs * xs, axis=-1, keepdims=True) + EPS))
    return jnp.concatenate(outs, axis=-1) * g


def _sigmoid_pair(z):
    t = jnp.exp(-jnp.abs(z))
    inv = 1.0 / (1.0 + t)
    big, small = inv, t * inv
    pos = z >= 0
    return jnp.where(pos, big, small), jnp.where(pos, small, big)


def _pack_rows(x):
    half = x.shape[1] // 2
    bits = lambda v: lax.bitcast_convert_type(v.astype(jnp.bfloat16).astype(jnp.float32), jnp.uint32)
    return (bits(x[:, :half]) >> 16) | (bits(x[:, half:]) & jnp.uint32(0xFFFF0000))


def _unpack_words(w):
    lo = lax.bitcast_convert_type(w << 16, jnp.float32)
    hi = lax.bitcast_convert_type(w & jnp.uint32(0xFFFF0000), jnp.float32)
    return lo, hi


def _store_row_tiles(ref, words):
    t = words.shape[0]
    for c in range(ROW_TILE):
        ref[pl.ds(c, t, stride=ROW_TILE), :] = words[:, c * LANES:(c + 1) * LANES]


def _load_row_tiles(ref, t):
    parts = [_unpack_words(ref[pl.ds(c, t, stride=ROW_TILE), :]) for c in range(ROW_TILE)]
    return [p[0] for p in parts], [p[1] for p in parts]


def _dot(a, b):
    return jnp.dot(a, b, preferred_element_type=jnp.float32)


def _dot_nt(a, b):
    return lax.dot_general(a, b, (((1,), (1,)), ((), ())), preferred_element_type=jnp.float32)


def _dot_tn(a, b):
    return lax.dot_general(a, b, (((0,), (0,)), ((), ())), preferred_element_type=jnp.float32)


def _hgrn_chunk(q, z, iv, lb, amat, level, st_ref, first_valid_row):
    sig, sig_neg = _sigmoid_pair(z)
    lf = jnp.log(lb + (1.0 - lb) * sig)
    k = (1.0 - lb) * sig_neg
    row = lax.broadcasted_iota(jnp.int32, (CHUNK, HGRN_W), 0)
    if first_valid_row:
        valid = row >= first_valid_row
        lf = jnp.where(valid, lf, 0.0)
        k = jnp.where(valid, k, 0.0)

    h1 = lf.astype(jnp.bfloat16)
    h2 = (lf - h1.astype(jnp.float32)).astype(jnp.bfloat16)
    e_all = _dot(amat, jnp.concatenate([h1, h2], axis=0))

    b = e_all[0:CHUNK]
    b_last = b[CHUNK - 1:CHUNK]
    q_in = (q * jnp.exp(b)).astype(jnp.bfloat16)
    k_out = (k * jnp.exp(b_last - b)).astype(jnp.bfloat16)
    st_decay = jnp.exp(b_last)
    v_bf = iv.astype(jnp.bfloat16)

    q_lv = [q.astype(jnp.bfloat16)]
    k_lv = [k.astype(jnp.bfloat16)]
    for i, m in enumerate(HALF_SPANS):
        ex = jnp.exp(e_all[(1 + i) * CHUNK:(2 + i) * CHUNK])
        right = (row & m) != 0
        q_lv.append(jnp.where(right, q * ex, 0.0).astype(jnp.bfloat16))
        k_lv.append(jnp.where(right, 0.0, k * ex).astype(jnp.bfloat16))
    lv_of = [len(HALF_SPANS)] + list(range(len(HALF_SPANS)))

    outs = []
    for h in range(HEADS):
        cols = slice(h * HEAD_DIM, (h + 1) * HEAD_DIM)
        scores = jnp.zeros((CHUNK, CHUNK), jnp.float32)
        for ql, kl, lv in zip(q_lv, k_lv, lv_of):
            scores = jnp.where(level == lv, _dot_nt(ql[:, cols], kl[:, cols]), scores)
        st = st_ref[h]
        o = _dot(scores.astype(jnp.bfloat16), v_bf[:, cols]) + _dot_nt(q_in[:, cols], st.astype(jnp.bfloat16))
        st_ref[h] = st * st_decay[:, cols] + _dot_tn(v_bf[:, cols], k_out[:, cols])
        outs.append(o)
    return jnp.concatenate(outs, axis=-1)


def _mixer_kernel(with_send, x_ref, meta_ref, gmix_ref, win_ref, lbt_ref, ghg_ref, cw_ref, gcv_ref, wout_ref,
                  amat_ref, level_ref, *rest):
    if with_send:
        off_ref, rows_ref, h1_ref, dst_hbm, proj_ref, o_ref, u_ref, st_ref, stage_ref, sem = rest
    else:
        h1_ref, proj_ref, o_ref, u_ref, st_ref = rest
    j = pl.program_id(1)
    rows = x_ref.shape[0]
    n_in = win_ref.shape[1]

    lbt = lbt_ref[...]
    lbe = jnp.exp(lbt - jnp.max(lbt, axis=0, keepdims=True))
    lb = lbe[0:1] / jnp.sum(lbe, axis=0, keepdims=True)

    amat = amat_ref[...]
    level = level_ref[...]
    gmix = gmix_ref[...]

    def project(xv, dst_rows):
        xn = _rms(xv, gmix).astype(jnp.bfloat16)
        for c0 in range(0, n_in, 512):
            proj_ref[dst_rows, c0:c0 + 512] = _dot(xn, win_ref[:, c0:c0 + 512])

    @pl.when(j == 0)
    def _():
        st_ref[...] = jnp.zeros_like(st_ref)
        project(meta_ref[...], pl.ds(0, CHUNK))
        pm = proj_ref[0:CHUNK, :]
        _hgrn_chunk(pm[:, 0:512], pm[:, 512:1024], pm[:, 1024:1536], lb, amat, level, st_ref,
                    CHUNK - N_META)
        u_ref[0:8, :] = (pm[:, 2560:3072] * pm[:, 3072:3584])[CHUNK - 8:CHUNK]

    if with_send:
        step = pl.program_id(0) * pl.num_programs(1) + j
        slot = step % 2
        stage_ref[slot] = rows_ref[...]
        for tok in range(rows_ref.shape[0] // ROW_TILE):
            _start_rows_out(tok, stage_ref.at[slot], off_ref, dst_hbm, sem.at[slot])

    project(x_ref[...], pl.ds(0, rows))

    def chunk_body(c, carry):
        for u in range(CHUNK_UNROLL):
            r0 = pl.multiple_of((c * CHUNK_UNROLL + u) * CHUNK, CHUNK)
            q = proj_ref[pl.ds(r0, CHUNK), 0:512]
            z = proj_ref[pl.ds(r0, CHUNK), 512:1024]
            iv = proj_ref[pl.ds(r0, CHUNK), 1024:1536]
            o_ref[pl.ds(r0, CHUNK), :] = _hgrn_chunk(q, z, iv, lb, amat, level, st_ref, 0)
        return carry

    lax.fori_loop(0, rows // (CHUNK * CHUNK_UNROLL), chunk_body, 0)

    g_out = proj_ref[:, 1536:2048]
    g_sig, _ = _sigmoid_pair(g_out)
    y_hgrn = _group_rms(o_ref[...], ghg_ref[...], HEAD_DIM) * (g_out * g_sig)

    u = proj_ref[:, 2560:3072] * proj_ref[:, 3072:3584]
    u_ref[8:8 + rows, :] = u
    cw = cw_ref[...]
    y = cw[2:3] * u + cw[1:2] * u_ref[7:7 + rows, :] + cw[0:1] * u_ref[6:6 + rows, :]
    u_ref[0:8, :] = u[rows - 8:rows]
    y_conv = _group_rms(proj_ref[:, 2048:2560] * y, gcv_ref[...], CONV_W // CONV_GROUPS)

    mixed = jnp.concatenate([y_hgrn, y_conv], axis=-1).astype(jnp.bfloat16)
    h1_ref[...] = x_ref[...] + _dot(mixed, wout_ref[...])

    if with_send:
        @pl.when(step >= 1)
        def _():
            _wait_rows_out(stage_ref.at[1 - slot], dst_hbm, sem.at[1 - slot])

        @pl.when(step == pl.num_programs(0) * pl.num_programs(1) - 1)
        def _():
            _wait_rows_out(stage_ref.at[slot], dst_hbm, sem.at[slot])


def _mixer(x, batch0, n_batch, meta_pad, gmix, w_in, lb_table, ghg, conv_w, gcv, w_out, send=None):
    _, seq, d = x.shape
    n_in = w_in.shape[1]
    rows = MIX_ROWS
    n_j = seq // rows
    const = lambda *shape: pl.BlockSpec(shape, lambda b, j: (0,) * len(shape))
    in_specs = [
        pl.BlockSpec((None, rows, d), lambda b, j: (batch0 + b, j, 0)),
        const(CHUNK, d), const(1, d), const(d, n_in), const(*lb_table.shape), const(1, HGRN_W),
        const(CONV_K, CONV_W), const(1, CONV_W), const(d, d),
        const(N_LEVELS * CHUNK, 2 * CHUNK), const(CHUNK, CHUNK),
    ]
    args = [x, meta_pad, gmix, w_in, lb_table, ghg, conv_w, gcv, w_out,
            jnp.asarray(np.tile(_decay_sum_matrix(), (1, 2)), jnp.bfloat16), jnp.asarray(_level_matrix())]
    out_shape = jax.ShapeDtypeStruct((n_batch, seq, d), jnp.float32)
    out_specs = pl.BlockSpec((None, rows, d), lambda b, j: (b, j, 0))
    scratch = [
        pltpu.VMEM((rows, n_in), jnp.float32),
        pltpu.VMEM((rows, HGRN_W), jnp.float32),
        pltpu.VMEM((rows + 8, CONV_W), jnp.float32),
        pltpu.VMEM((HEADS, HEAD_DIM, HEAD_DIM), jnp.float32),
    ]
    if send is not None:
        off, xn, n_rows = send
        assert off.shape[0] == n_batch * n_j and off.shape[2] == rows * TOP_K
        in_specs += [_offsets_spec(rows, lambda b, j: (b * n_j + j, 0, 0)),
                     pl.BlockSpec((rows * ROW_TILE, LANES), lambda b, j: (b * n_j + j, 0))]
        args += [off, xn]
        out_shape = (out_shape, jax.ShapeDtypeStruct((n_rows * ROW_TILE, LANES), xn.dtype))
        out_specs = (out_specs, pl.BlockSpec(memory_space=pl.ANY))
        scratch += [pltpu.VMEM((2, rows * ROW_TILE, LANES), xn.dtype), pltpu.SemaphoreType.DMA((2,))]
    return pl.pallas_call(
        functools.partial(_mixer_kernel, send is not None),
        out_shape=out_shape,
        grid=(n_batch, n_j),
        in_specs=in_specs,
        out_specs=out_specs,
        scratch_shapes=scratch,
        compiler_params=pltpu.CompilerParams(
            dimension_semantics=("arbitrary", "arbitrary"), vmem_limit_bytes=VMEM_LIMIT),
        name="mixer_send" if send is not None else "mixer",
    )(*args)


def _router_kernel(h1_ref, gffn_ref, wr_ref, bias_ref, wgs_ref, wus_ref, wds_ref, tri_ref,
                   base_ref, xn_ref, idx_ref, gate_ref, rank_ref, counts_ref, carry_ref):
    i = pl.program_id(0)
    n_tok = h1_ref.shape[0]

    @pl.when(i == 0)
    def _():
        carry_ref[...] = jnp.zeros_like(carry_ref)

    h1 = h1_ref[...]
    xn = _rms(h1, gffn_ref[...])
    _store_row_tiles(xn_ref, _pack_rows(xn))
    xb = xn.astype(jnp.bfloat16)

    g_pre = _dot(xb, wgs_ref[...])
    gate_s, _ = _sigmoid_pair(g_pre)
    hid = (g_pre * gate_s) * _dot(xb, wus_ref[...])
    base_ref[...] = h1 + _dot(hid.astype(jnp.bfloat16), wds_ref[...])

    x_lo = (xn - xb.astype(jnp.float32)).astype(jnp.bfloat16)
    wr = wr_ref[...]
    w_hi = wr.astype(jnp.bfloat16)
    w_lo = (wr - w_hi.astype(jnp.float32)).astype(jnp.bfloat16)
    logits = _dot_nt(w_hi, xb) + _dot_nt(w_hi, x_lo) + _dot_nt(w_lo, xb)
    scores, _ = _sigmoid_pair(logits)
    sel = scores + bias_ref[...]
    eid = lax.broadcasted_iota(jnp.int32, (N_EXPERTS, n_tok), 0).astype(jnp.float32)
    picks, top_s = [], []
    for _ in range(TOP_K):
        best = jnp.max(sel, axis=0, keepdims=True)
        pick = jnp.min(jnp.where(sel == best, eid, float(N_EXPERTS)), axis=0, keepdims=True)
        hit = eid == pick
        top_s.append(jnp.sum(jnp.where(hit, scores, 0.0), axis=0, keepdims=True))
        sel = jnp.where(hit, -jnp.inf, sel)
        picks.append(pick)
    top_s = jnp.concatenate(top_s, axis=0)
    gate_ref[...] = top_s / jnp.sum(top_s, axis=0, keepdims=True) * ROUTED_SCALE
    idx_ref[...] = jnp.concatenate(picks, axis=0).astype(jnp.int32)

    chosen = jnp.zeros((N_EXPERTS, n_tok), jnp.float32)
    for pick in picks:
        chosen = chosen + jnp.where(eid == pick, 1.0, 0.0)
    incl = _dot(chosen.astype(jnp.bfloat16), tri_ref[...])
    before = carry_ref[...] + incl - 1.0
    ranks = [jnp.sum(jnp.where(eid == pick, before, 0.0), axis=0, keepdims=True) for pick in picks]
    rank_ref[...] = jnp.concatenate(ranks, axis=0).astype(jnp.int32)
    carry_ref[...] = carry_ref[...] + incl[:, n_tok - 1:n_tok]
    counts_ref[...] = jnp.broadcast_to(carry_ref[...], counts_ref.shape).astype(jnp.int32)


def _router(h1, gffn, w_router, bias, wgs, wus, wds):
    n, d = h1.shape
    t = ROUTE_ROWS
    ff = wgs.shape[1]
    tri = jnp.asarray(np.triu(np.ones((t, t), np.float32)), jnp.bfloat16)
    const = lambda *shape: pl.BlockSpec(shape, lambda i: (0,) * len(shape))
    tok = lambda width: pl.BlockSpec((t, width), lambda i: (i, 0))
    slot = pl.BlockSpec((TOP_K, t), lambda i: (0, i))
    return pl.pallas_call(
        _router_kernel,
        out_shape=(
            jax.ShapeDtypeStruct((n, d), jnp.float32),
            jax.ShapeDtypeStruct((n * ROW_TILE, LANES), jnp.uint32),
            jax.ShapeDtypeStruct((TOP_K, n), jnp.int32),
            jax.ShapeDtypeStruct((TOP_K, n), jnp.float32),
            jax.ShapeDtypeStruct((TOP_K, n), jnp.int32),
            jax.ShapeDtypeStruct((N_EXPERTS, 128), jnp.int32),
        ),
        grid=(n // t,),
        in_specs=[tok(d), const(1, d), const(N_EXPERTS, d), const(N_EXPERTS, 1),
                  const(d, ff), const(d, ff), const(ff, d), const(t, t)],
        out_specs=(tok(d), pl.BlockSpec((t * ROW_TILE, LANES), lambda i: (i, 0)), slot, slot, slot,
                   const(N_EXPERTS, 128)),
        scratch_shapes=[pltpu.VMEM((N_EXPERTS, 1), jnp.float32)],
        compiler_params=pltpu.CompilerParams(
            dimension_semantics=("arbitrary",), vmem_limit_bytes=VMEM_LIMIT),
        name="router",
    )(h1, gffn, w_router, bias, wgs, wus, wds, tri)


def _slab(ref, first_sublane):
    if not isinstance(first_sublane, int):
        first_sublane = pl.multiple_of(first_sublane, ROW_TILE)
    return ref.at[pl.ds(first_sublane, ROW_TILE)]


def _start_rows_out(tok, rows_ref, off_ref, dst_hbm, sem):
    for s in range(TOP_K):
        pltpu.make_async_copy(_slab(rows_ref, tok * ROW_TILE), _slab(dst_hbm, off_ref[0, tok * TOP_K + s]),
                              sem).start(priority=s % DMA_QUEUES)


def _wait_rows_out(rows_ref, dst_hbm, sem):
    for s in range(TOP_K):
        pltpu.make_async_copy(rows_ref, dst_hbm.at[pl.ds(0, rows_ref.shape[0])], sem).wait()


def _start_rows_in(tok, n_tok, src_hbm, off_ref, buf_ref, sem):
    for s in range(TOP_K):
        pltpu.make_async_copy(_slab(src_hbm, off_ref[0, tok * TOP_K + s]),
                              _slab(buf_ref, (s * n_tok + tok) * ROW_TILE), sem).start(priority=s % DMA_QUEUES)


def _wait_rows_in(src_hbm, buf_ref, sem):
    pltpu.make_async_copy(src_hbm.at[pl.ds(0, buf_ref.shape[0])], buf_ref, sem).wait()


def _weighted_sum(base_ref, gates_ref, gfin_ref, buf_ref, out_ref):
    n_tok = base_ref.shape[0]
    gates = gates_ref[...]
    width = ROW_TILE * LANES
    lo_chunks, hi_chunks = [], []
    for c in range(ROW_TILE):
        acc_lo = base_ref[:, c * LANES:(c + 1) * LANES]
        acc_hi = base_ref[:, width + c * LANES:width + (c + 1) * LANES]
        for s in range(TOP_K):
            lo, hi = _unpack_words(buf_ref[pl.ds(s * n_tok * ROW_TILE + c, n_tok, stride=ROW_TILE), :])
            acc_lo = acc_lo + gates[:, s:s + 1] * lo
            acc_hi = acc_hi + gates[:, s:s + 1] * hi
        lo_chunks.append(acc_lo)
        hi_chunks.append(acc_hi)
    out_ref[...] = _rms(jnp.concatenate(lo_chunks + hi_chunks, axis=-1), gfin_ref[...])


def _offsets_spec(n_tok, index_map):
    return pl.BlockSpec((None, 1, n_tok * TOP_K), index_map, memory_space=pltpu.SMEM)


def _swiglu_block(n_valid, xs_ref, wgb_ref, wub_ref, wdb_ref, y_ref):
    r = EXPERT_ROWS
    keep = lax.broadcasted_iota(jnp.int32, (r, LANES), 0) < n_valid
    lo, hi = _load_row_tiles(xs_ref, r)
    xb = jnp.concatenate([jnp.where(keep, c, 0.0).astype(jnp.bfloat16) for c in lo + hi], axis=-1)
    g = _dot(xb, wgb_ref[...])
    u = _dot(xb, wub_ref[...])
    g_sig, _ = _sigmoid_pair(g)
    hid = ((g * g_sig) * u).astype(jnp.bfloat16)
    _store_row_tiles(y_ref, _pack_rows(_dot(hid, wdb_ref[...])))


def _refresh_weights(i, blk_e_ref, wg_ref, wu_ref, wd_ref, wgb_ref, wub_ref, wdb_ref):
    @pl.when((i == 0) | (blk_e_ref[i] != blk_e_ref[jnp.maximum(i - 1, 0)]))
    def _():
        wgb_ref[...] = wg_ref[...].astype(jnp.bfloat16)
        wub_ref[...] = wu_ref[...].astype(jnp.bfloat16)
        wdb_ref[...] = wd_ref[...].astype(jnp.bfloat16)


def _experts_out_kernel(side_steps, blk_e_ref, blk_rows_ref, blk_i_ref, xs_ref, wg_ref, wu_ref, wd_ref, off_ref,
                        rows_ref, y_ref, dst_hbm, wgb_ref, wub_ref, wdb_ref, stage_ref, sem):
    del blk_i_ref
    i = pl.program_id(0)
    n_valid = blk_rows_ref[i]
    side = i < side_steps
    slot = i % 2
    _refresh_weights(i, blk_e_ref, wg_ref, wu_ref, wd_ref, wgb_ref, wub_ref, wdb_ref)

    def send():
        stage_ref[slot] = rows_ref[...]
        for tok in range(SIDE_ROWS):
            _start_rows_out(tok, stage_ref.at[slot], off_ref, dst_hbm, sem.at[slot])

    @pl.when((n_valid > 0) & side)
    def _():
        send()
        _swiglu_block(n_valid, xs_ref, wgb_ref, wub_ref, wdb_ref, y_ref)

    @pl.when((n_valid > 0) & jnp.logical_not(side))
    def _():
        _swiglu_block(n_valid, xs_ref, wgb_ref, wub_ref, wdb_ref, y_ref)

    @pl.when((n_valid == 0) & side)
    def _():
        send()

    @pl.when((i >= 1) & (i <= side_steps))
    def _():
        _wait_rows_out(stage_ref.at[1 - slot], dst_hbm, sem.at[1 - slot])


def _experts_in_kernel(side_steps, blk_e_ref, blk_rows_ref, blk_i_ref, xs_ref, wg_ref, wu_ref, wd_ref,
                       off_ref, off_nx_ref, off_nx2_ref, base_ref, gates_ref, gfin_ref, src_hbm, y_ref, out_ref,
                       wgb_ref, wub_ref, wdb_ref, buf_ref, sem):
    del blk_i_ref
    i = pl.program_id(0)
    n_valid = blk_rows_ref[i]
    slot = i % 2
    _refresh_weights(i, blk_e_ref, wg_ref, wu_ref, wd_ref, wgb_ref, wub_ref, wdb_ref)

    def fetch(off_r, into):
        for tok in range(SIDE_ROWS):
            _start_rows_in(tok, SIDE_ROWS, src_hbm, off_r, buf_ref.at[into], sem.at[into])

    @pl.when(i == 0)
    def _():
        fetch(off_ref, 0)
        fetch(off_nx_ref, 1)

    @pl.when(i < side_steps)
    def _():
        _wait_rows_in(src_hbm, buf_ref.at[slot], sem.at[slot])
        _weighted_sum(base_ref, gates_ref, gfin_ref, buf_ref.at[slot], out_ref)

    more = i + 2 < side_steps

    @pl.when((n_valid > 0) & more)
    def _():
        fetch(off_nx2_ref, slot)
        _swiglu_block(n_valid, xs_ref, wgb_ref, wub_ref, wdb_ref, y_ref)

    @pl.when((n_valid > 0) & jnp.logical_not(more))
    def _():
        _swiglu_block(n_valid, xs_ref, wgb_ref, wub_ref, wdb_ref, y_ref)

    @pl.when((n_valid == 0) & more)
    def _():
        fetch(off_nx2_ref, slot)


def _expert_specs(d, ff):
    r = EXPERT_ROWS
    rows = pl.BlockSpec((r * ROW_TILE, LANES), lambda i, be, br, bi: (bi[i], 0))
    weights = [pl.BlockSpec((None, d, ff), lambda i, be, br, bi: (be[i], 0, 0)),
               pl.BlockSpec((None, d, ff), lambda i, be, br, bi: (be[i], 0, 0)),
               pl.BlockSpec((None, ff, d), lambda i, be, br, bi: (be[i], 0, 0))]
    scratch = [pltpu.VMEM((d, ff), jnp.bfloat16), pltpu.VMEM((d, ff), jnp.bfloat16),
               pltpu.VMEM((ff, d), jnp.bfloat16)]
    return rows, weights, scratch


def _experts_out(blk, xs, wg, wu, wd, off, xn, first_block, n_rows_next):
    side_steps = off.shape[0]
    d, ff = wg.shape[1], wg.shape[2]
    rows, weights, scratch = _expert_specs(d, ff)
    side = lambda i, be, br, bi: jnp.minimum(i, side_steps - 1)
    return pl.pallas_call(
        functools.partial(_experts_out_kernel, side_steps),
        out_shape=(jax.ShapeDtypeStruct(xs.shape, xs.dtype),
                   jax.ShapeDtypeStruct((n_rows_next * ROW_TILE, LANES), xs.dtype)),
        grid_spec=pltpu.PrefetchScalarGridSpec(
            num_scalar_prefetch=3,
            grid=(xs.shape[0] // (EXPERT_ROWS * ROW_TILE),),
            in_specs=[rows] + weights + [
                _offsets_spec(SIDE_ROWS, lambda i, be, br, bi: (side(i, be, br, bi), 0, 0)),
                pl.BlockSpec((SIDE_ROWS * ROW_TILE, LANES), lambda i, be, br, bi: (first_block + side(i, be, br, bi), 0))],
            out_specs=(rows, pl.BlockSpec(memory_space=pl.ANY)),
            scratch_shapes=scratch + [pltpu.VMEM((2, SIDE_ROWS * ROW_TILE, LANES), xs.dtype),
                                      pltpu.SemaphoreType.DMA((2,))],
        ),
        compiler_params=pltpu.CompilerParams(
            dimension_semantics=("arbitrary",), vmem_limit_bytes=VMEM_LIMIT),
        name="experts_send",
    )(*blk, xs, wg, wu, wd, off, xn)


def _experts_in(blk, xs, wg, wu, wd, off, base, gates, gfin, y_prev, n):
    side_steps = off.shape[0]
    d, ff = wg.shape[1], wg.shape[2]
    rows, weights, scratch = _expert_specs(d, ff)
    side = lambda i, be, br, bi: jnp.minimum(i, side_steps - 1)
    nxt = lambda i, be, br, bi: jnp.minimum(i + 1, side_steps - 1)
    nxt2 = lambda i, be, br, bi: jnp.minimum(i + 2, side_steps - 1)
    return pl.pallas_call(
        functools.partial(_experts_in_kernel, side_steps),
        out_shape=(jax.ShapeDtypeStruct(xs.shape, xs.dtype), jax.ShapeDtypeStruct((n, d), jnp.float32)),
        grid_spec=pltpu.PrefetchScalarGridSpec(
            num_scalar_prefetch=3,
            grid=(xs.shape[0] // (EXPERT_ROWS * ROW_TILE),),
            in_specs=[rows] + weights + [
                _offsets_spec(SIDE_ROWS, lambda i, be, br, bi: (side(i, be, br, bi), 0, 0)),
                _offsets_spec(SIDE_ROWS, lambda i, be, br, bi: (nxt(i, be, br, bi), 0, 0)),
                _offsets_spec(SIDE_ROWS, lambda i, be, br, bi: (nxt2(i, be, br, bi), 0, 0)),
                pl.BlockSpec((SIDE_ROWS, d), lambda i, be, br, bi: (side(i, be, br, bi), 0)),
                pl.BlockSpec((SIDE_ROWS, TOP_K), lambda i, be, br, bi: (side(i, be, br, bi), 0)),
                pl.BlockSpec((1, d), lambda i, be, br, bi: (0, 0)),
                pl.BlockSpec(memory_space=pl.ANY)],
            out_specs=(rows, pl.BlockSpec((SIDE_ROWS, d), lambda i, be, br, bi: (side(i, be, br, bi), 0))),
            scratch_shapes=scratch + [pltpu.VMEM((2, TOP_K * SIDE_ROWS * ROW_TILE, LANES), xs.dtype),
                                      pltpu.SemaphoreType.DMA((2,))],
        ),
        compiler_params=pltpu.CompilerParams(
            dimension_semantics=("arbitrary",), vmem_limit_bytes=VMEM_LIMIT),
        name="experts_fetch",
    )(*blk, xs, wg, wu, wd, off, off, off, base, gates, gfin, y_prev)


def _combine_kernel(off_ref, off_nx_ref, base_ref, gates_ref, gfin_ref, y_hbm, out_in_ref, out_ref, buf_ref, sem):
    del out_in_ref
    i = pl.program_id(0)
    n_steps = pl.num_programs(0)
    n_tok = base_ref.shape[0]

    def issue(off_r, half):
        def body(g, carry):
            for u in range(ISSUE_UNROLL):
                _start_rows_in(g * ISSUE_UNROLL + u, n_tok, y_hbm, off_r, buf_ref.at[half], sem.at[half])
            return carry
        lax.fori_loop(0, n_tok // ISSUE_UNROLL, body, 0)

    half = i % 2

    @pl.when(i == 0)
    def _():
        issue(off_ref, 0)

    @pl.when(i + 1 < n_steps)
    def _():
        issue(off_nx_ref, 1 - half)

    _wait_rows_in(y_hbm, buf_ref.at[half], sem.at[half])
    _weighted_sum(base_ref, gates_ref, gfin_ref, buf_ref.at[half], out_ref)


def _combine(off, base, gates, gfin, y, out_prev, first_block):
    d = base.shape[1]
    t = COMBINE_ROWS
    n_steps = off.shape[0]
    tok = lambda width, first=0: pl.BlockSpec((t, width), lambda i: (first + i, 0))
    return pl.pallas_call(
        _combine_kernel,
        out_shape=jax.ShapeDtypeStruct(out_prev.shape, jnp.float32),
        grid=(n_steps,),
        in_specs=[_offsets_spec(t, lambda i: (i, 0, 0)),
                  _offsets_spec(t, lambda i: (jnp.minimum(i + 1, n_steps - 1), 0, 0)),
                  tok(d), tok(TOP_K), pl.BlockSpec((1, d), lambda i: (0, 0)),
                  pl.BlockSpec(memory_space=pl.ANY), pl.BlockSpec(memory_space=pl.ANY)],
        out_specs=tok(d, first_block),
        scratch_shapes=[pltpu.VMEM((2, TOP_K * t * ROW_TILE, LANES), y.dtype),
                        pltpu.SemaphoreType.DMA((2,))],
        input_output_aliases={6: 0},
        compiler_params=pltpu.CompilerParams(
            dimension_semantics=("arbitrary",), vmem_limit_bytes=VMEM_LIMIT),
        name="combine",
    )(off, off, base, gates, gfin, y, out_prev)


def _group_layout(counts, idx, rank):
    r = EXPERT_ROWS
    n_g = idx.shape[1]
    n_blocks = (n_g * TOP_K) // r + N_EXPERTS
    padded = (counts + r - 1) // r * r
    ends = jnp.cumsum(padded)
    starts = (ends - padded).astype(jnp.int32)
    blk_row0 = jnp.arange(n_blocks, dtype=jnp.int32) * r
    blk_e = jnp.minimum(jnp.sum((ends[None, :] <= blk_row0[:, None]).astype(jnp.int32), axis=1), N_EXPERTS - 1)
    onehot = (blk_e[:, None] == jnp.arange(N_EXPERTS, dtype=jnp.int32)[None, :]).astype(jnp.int32)
    blk_rows = jnp.clip(onehot @ counts - (blk_row0 - onehot @ starts), 0, r).astype(jnp.int32)
    dest = rank + jnp.sum(jnp.where(idx[None] == jnp.arange(N_EXPERTS, dtype=jnp.int32)[:, None, None],
                                    starts[:, None, None], 0), axis=0)
    blk_i = jnp.minimum(jnp.arange(n_blocks, dtype=jnp.int32), ends[-1].astype(jnp.int32) // r - 1)
    return (blk_e, blk_rows, blk_i), (dest * ROW_TILE).T.reshape(-1), n_blocks * r


def kernel(x, meta_tokens, norm_mix_g, w_in, lb_table, hgrn_norm_g, conv_w, conv_norm_g, w_out,
           norm_ffn_g, w_router, router_bias, w_gate_e, w_up_e, w_down_e, w_gate_s, w_up_s, w_down_s,
           norm_final_g):
    bsz, seq, d = x.shape
    n = bsz * seq
    assert TOKEN_GROUPS == 2 and bsz % TOKEN_GROUPS == 0
    b_g = bsz // TOKEN_GROUPS
    n_g = n // TOKEN_GROUPS
    bf = jnp.bfloat16
    assert seq % MIX_ROWS == 0 and MIX_ROWS % CHUNK == 0
    assert n_g % ROUTE_ROWS == 0 and n_g % COMBINE_ROWS == 0
    assert (n_g * TOP_K) % EXPERT_ROWS == 0 and n_g % SIDE_ROWS == 0
    assert 2 <= n_g // SIDE_ROWS < (n_g * TOP_K) // EXPERT_ROWS + N_EXPERTS

    meta_pad = jnp.zeros((CHUNK, d), jnp.float32).at[CHUNK - N_META:].set(meta_tokens)
    mix = functools.partial(_mixer, meta_pad=meta_pad, gmix=norm_mix_g[0:1], w_in=w_in[0].astype(bf),
                            lb_table=lb_table, ghg=hgrn_norm_g[0:1], conv_w=conv_w[0], gcv=conv_norm_g[0:1],
                            w_out=w_out[0].astype(bf))
    route = functools.partial(_router, gffn=norm_ffn_g[0:1], w_router=w_router[0].T, bias=router_bias[0][:, None],
                              wgs=w_gate_s[0].astype(bf), wus=w_up_s[0].astype(bf), wds=w_down_s[0].astype(bf))
    steps = lambda off, t: off.reshape(n_g // t, 1, t * TOP_K)
    gfin = norm_final_g[None, :]
    wg, wu, wd = w_gate_e[0], w_up_e[0], w_down_e[0]

    h1_0 = mix(x, 0, b_g)
    base0, xn0, idx0, gate0, rank0, counts0 = route(h1_0.reshape(n_g, d))
    blk0, off0, rows0 = _group_layout(counts0[:, 0], idx0, rank0)
    h1_1, xs0 = mix(x, b_g, b_g, send=(steps(off0, MIX_ROWS), xn0, rows0))
    base1, xn1, idx1, gate1, rank1, counts1 = route(h1_1.reshape(n_g, d))
    blk1, off1, rows1 = _group_layout(counts1[:, 0], idx1, rank1)
    y0, xs1 = _experts_out(blk0, xs0, wg, wu, wd, steps(off1, SIDE_ROWS), xn1, 0, rows1)
    y1, out = _experts_in(blk1, xs1, wg, wu, wd, steps(off0, SIDE_ROWS), base0, gate0.T, gfin, y0, n)
    out = _combine(steps(off1, COMBINE_ROWS), base1, gate1.T, gfin, y1, out, n_g // COMBINE_ROWS)
    return out.reshape(bsz, seq, d)
```

```python
import functools

import numpy as np
import jax
import jax.numpy as jnp
from jax import lax
from jax.experimental import pallas as pl
from jax.experimental.pallas import tpu as pltpu

N_META = 16
CHUNK = 128
HEADS = 4
HEAD_DIM = 128
HGRN_W = HEADS * HEAD_DIM
CONV_W = 512
CONV_GROUPS = 4
CONV_K = 3
N_EXPERTS = 64
TOP_K = 8
ROUTED_SCALE = 2.5
EPS = 1e-6

V7X_VMEM_BYTES = 64 * 1024 * 1024
VMEM_LIMIT = V7X_VMEM_BYTES - 8 * 1024 * 1024

MIX_ROWS = 512
CHUNK_UNROLL = 4
ROUTE_ROWS = 512
EXPERT_ROWS = 1024
COMBINE_ROWS = 256
DMA_QUEUES = 2
ISSUE_UNROLL = 8
TOKEN_GROUPS = 2
SIDE_ROWS = 128

LANES = 128
ROW_TILE = 4

HALF_SPANS = (64, 32, 16, 8, 4, 2, 1)
N_LEVELS = len(HALF_SPANS) + 1


def _decay_sum_matrix():
    a = np.zeros((N_LEVELS, CHUNK, CHUNK), np.float32)
    a[0] = np.tril(np.ones((CHUNK, CHUNK), np.float32))
    for i, m in enumerate(HALF_SPANS):
        for t in range(CHUNK):
            mid = (t // (2 * m)) * 2 * m + m
            if t >= mid:
                a[1 + i, t, mid:t + 1] = 1.0
            else:
                a[1 + i, t, t + 1:mid] = 1.0
    return a.reshape(N_LEVELS * CHUNK, CHUNK)


def _level_matrix():
    lv = np.full((CHUNK, CHUNK), -1, np.int32)
    for t in range(CHUNK):
        lv[t, t] = len(HALF_SPANS)
        for s in range(t):
            top = (t ^ s).bit_length() - 1
            lv[t, s] = HALF_SPANS.index(1 << top)
    return lv


def _rms(x, g):
    return x * lax.rsqrt(jnp.mean(x * x, axis=-1, keepdims=True) + EPS) * g


def _group_rms(x, g, width):
    outs = []
    for j in range(x.shape[-1] // width):
        xs = x[:, j * width:(j + 1) * width]
        outs.append(xs * lax.rsqrt(jnp.mean(xs * xs, axis=-1, keepdims=True) + EPS))
    return jnp.concatenate(outs, axis=-1) * g


def _sigmoid_pair(z):
    t = jnp.exp(-jnp.abs(z))
    inv = 1.0 / (1.0 + t)
    big, small = inv, t * inv
    pos = z >= 0
    return jnp.where(pos, big, small), jnp.where(pos, small, big)


def _pack_rows(x):
    half = x.shape[1] // 2
    bits = lambda v: lax.bitcast_convert_type(v.astype(jnp.bfloat16).astype(jnp.float32), jnp.uint32)
    return (bits(x[:, :half]) >> 16) | (bits(x[:, half:]) & jnp.uint32(0xFFFF0000))


def _unpack_words(w):
    lo = lax.bitcast_convert_type(w << 16, jnp.float32)
    hi = lax.bitcast_convert_type(w & jnp.uint32(0xFFFF0000), jnp.float32)
    return lo, hi


def _store_row_tiles(ref, words):
    t = words.shape[0]
    for c in range(ROW_TILE):
        ref[pl.ds(c, t, stride=ROW_TILE), :] = words[:, c * LANES:(c + 1) * LANES]


def _load_row_tiles(ref, t):
    parts = [_unpack_words(ref[pl.ds(c, t, stride=ROW_TILE), :]) for c in range(ROW_TILE)]
    return [p[0] for p in parts], [p[1] for p in parts]


def _dot(a, b):
    return jnp.dot(a, b, preferred_element_type=jnp.float32)


def _dot_nt(a, b):
    return lax.dot_general(a, b, (((1,), (1,)), ((), ())), preferred_element_type=jnp.float32)


def _dot_tn(a, b):
    return lax.dot_general(a, b, (((0,), (0,)), ((), ())), preferred_element_type=jnp.float32)


def _hgrn_chunk(q, z, iv, lb, amat, level, st_ref, first_valid_row):
    sig, sig_neg = _sigmoid_pair(z)
    lf = jnp.log(lb + (1.0 - lb) * sig)
    k = (1.0 - lb) * sig_neg
    row = lax.broadcasted_iota(jnp.int32, (CHUNK, HGRN_W), 0)
    if first_valid_row:
        valid = row >= first_valid_row
        lf = jnp.where(valid, lf, 0.0)
        k = jnp.where(valid, k, 0.0)

    h1 = lf.astype(jnp.bfloat16)
    h2 = (lf - h1.astype(jnp.float32)).astype(jnp.bfloat16)
    e_all = _dot(amat, jnp.concatenate([h1, h2], axis=0))

    b = e_all[0:CHUNK]
    b_last = b[CHUNK - 1:CHUNK]
    q_in = (q * jnp.exp(b)).astype(jnp.bfloat16)
    k_out = (k * jnp.exp(b_last - b)).astype(jnp.bfloat16)
    st_decay = jnp.exp(b_last)
    v_bf = iv.astype(jnp.bfloat16)

    q_lv = [q.astype(jnp.bfloat16)]
    k_lv = [k.astype(jnp.bfloat16)]
    for i, m in enumerate(HALF_SPANS):
        ex = jnp.exp(e_all[(1 + i) * CHUNK:(2 + i) * CHUNK])
        right = (row & m) != 0
        q_lv.append(jnp.where(right, q * ex, 0.0).astype(jnp.bfloat16))
        k_lv.append(jnp.where(right, 0.0, k * ex).astype(jnp.bfloat16))
    lv_of = [len(HALF_SPANS)] + list(range(len(HALF_SPANS)))

    outs = []
    for h in range(HEADS):
        cols = slice(h * HEAD_DIM, (h + 1) * HEAD_DIM)
        scores = jnp.zeros((CHUNK, CHUNK), jnp.float32)
        for ql, kl, lv in zip(q_lv, k_lv, lv_of):
            scores = jnp.where(level == lv, _dot_nt(ql[:, cols], kl[:, cols]), scores)
        st = st_ref[h]
        o = _dot(scores.astype(jnp.bfloat16), v_bf[:, cols]) + _dot_nt(q_in[:, cols], st.astype(jnp.bfloat16))
        st_ref[h] = st * st_decay[:, cols] + _dot_tn(v_bf[:, cols], k_out[:, cols])
        outs.append(o)
    return jnp.concatenate(outs, axis=-1)


def _mixer_kernel(with_send, x_ref, meta_ref, gmix_ref, win_ref, lbt_ref, ghg_ref, cw_ref, gcv_ref, wout_ref,
                  amat_ref, level_ref, *rest):
    if with_send:
        off_ref, rows_ref, h1_ref, dst_hbm, proj_ref, o_ref, u_ref, st_ref, stage_ref, sem = rest
    else:
        h1_ref, proj_ref, o_ref, u_ref, st_ref = rest
    j = pl.program_id(1)
    rows = x_ref.shape[0]
    n_in = win_ref.shape[1]

    lbt = lbt_ref[...]
    lbe = jnp.exp(lbt - jnp.max(lbt, axis=0, keepdims=True))
    lb = lbe[0:1] / jnp.sum(lbe, axis=0, keepdims=True)

    amat = amat_ref[...]
    level = level_ref[...]
    gmix = gmix_ref[...]

    def project(xv, dst_rows):
        xn = _rms(xv, gmix).astype(jnp.bfloat16)
        for c0 in range(0, n_in, 512):
            proj_ref[dst_rows, c0:c0 + 512] = _dot(xn, win_ref[:, c0:c0 + 512])

    @pl.when(j == 0)
    def _():
        st_ref[...] = jnp.zeros_like(st_ref)
        project(meta_ref[...], pl.ds(0, CHUNK))
        pm = proj_ref[0:CHUNK, :]
        _hgrn_chunk(pm[:, 0:512], pm[:, 512:1024], pm[:, 1024:1536], lb, amat, level, st_ref,
                    CHUNK - N_META)
        u_ref[0:8, :] = (pm[:, 2560:3072] * pm[:, 3072:3584])[CHUNK - 8:CHUNK]

    if with_send:
        step = pl.program_id(0) * pl.num_programs(1) + j
        slot = step % 2
        stage_ref[slot] = rows_ref[...]
        for tok in range(rows_ref.shape[0] // ROW_TILE):
            _start_rows_out(tok, stage_ref.at[slot], off_ref, dst_hbm, sem.at[slot])

    project(x_ref[...], pl.ds(0, rows))

    def chunk_body(c, carry):
        for u in range(CHUNK_UNROLL):
            r0 = pl.multiple_of((c * CHUNK_UNROLL + u) * CHUNK, CHUNK)
            q = proj_ref[pl.ds(r0, CHUNK), 0:512]
            z = proj_ref[pl.ds(r0, CHUNK), 512:1024]
            iv = proj_ref[pl.ds(r0, CHUNK), 1024:1536]
            o_ref[pl.ds(r0, CHUNK), :] = _hgrn_chunk(q, z, iv, lb, amat, level, st_ref, 0)
        return carry

    lax.fori_loop(0, rows // (CHUNK * CHUNK_UNROLL), chunk_body, 0)

    g_out = proj_ref[:, 1536:2048]
    g_sig, _ = _sigmoid_pair(g_out)
    y_hgrn = _group_rms(o_ref[...], ghg_ref[...], HEAD_DIM) * (g_out * g_sig)

    u = proj_ref[:, 2560:3072] * proj_ref[:, 3072:3584]
    u_ref[8:8 + rows, :] = u
    cw = cw_ref[...]
    y = cw[2:3] * u + cw[1:2] * u_ref[7:7 + rows, :] + cw[0:1] * u_ref[6:6 + rows, :]
    u_ref[0:8, :] = u[rows - 8:rows]
    y_conv = _group_rms(proj_ref[:, 2048:2560] * y, gcv_ref[...], CONV_W // CONV_GROUPS)

    mixed = jnp.concatenate([y_hgrn, y_conv], axis=-1).astype(jnp.bfloat16)
    h1_ref[...] = x_ref[...] + _dot(mixed, wout_ref[...])

    if with_send:
        @pl.when(step >= 1)
        def _():
            _wait_rows_out(stage_ref.at[1 - slot], dst_hbm, sem.at[1 - slot])

        @pl.when(step == pl.num_programs(0) * pl.num_programs(1) - 1)
        def _():
            _wait_rows_out(stage_ref.at[slot], dst_hbm, sem.at[slot])


def _mixer(x, batch0, n_batch, meta_pad, gmix, w_in, lb_table, ghg, conv_w, gcv, w_out, send=None):
    _, seq, d = x.shape
    n_in = w_in.shape[1]
    rows = MIX_ROWS
    n_j = seq // rows
    const = lambda *shape: pl.BlockSpec(shape, lambda b, j: (0,) * len(shape))
    in_specs = [
        pl.BlockSpec((None, rows, d), lambda b, j: (batch0 + b, j, 0)),
        const(CHUNK, d), const(1, d), const(d, n_in), const(*lb_table.shape), const(1, HGRN_W),
        const(CONV_K, CONV_W), const(1, CONV_W), const(d, d),
        const(N_LEVELS * CHUNK, 2 * CHUNK), const(CHUNK, CHUNK),
    ]
    args = [x, meta_pad, gmix, w_in, lb_table, ghg, conv_w, gcv, w_out,
            jnp.asarray(np.tile(_decay_sum_matrix(), (1, 2)), jnp.bfloat16), jnp.asarray(_level_matrix())]
    out_shape = jax.ShapeDtypeStruct((n_batch, seq, d), jnp.float32)
    out_specs = pl.BlockSpec((None, rows, d), lambda b, j: (b, j, 0))
    scratch = [
        pltpu.VMEM((rows, n_in), jnp.float32),
        pltpu.VMEM((rows, HGRN_W), jnp.float32),
        pltpu.VMEM((rows + 8, CONV_W), jnp.float32),
        pltpu.VMEM((HEADS, HEAD_DIM, HEAD_DIM), jnp.float32),
    ]
    if send is not None:
        off, xn, n_rows = send
        assert off.shape[0] == n_batch * n_j and off.shape[2] == rows * TOP_K
        in_specs += [_offsets_spec(rows, lambda b, j: (b * n_j + j, 0, 0)),
                     pl.BlockSpec((rows * ROW_TILE, LANES), lambda b, j: (b * n_j + j, 0))]
        args += [off, xn]
        out_shape = (out_shape, jax.ShapeDtypeStruct((n_rows * ROW_TILE, LANES), xn.dtype))
        out_specs = (out_specs, pl.BlockSpec(memory_space=pl.ANY))
        scratch += [pltpu.VMEM((2, rows * ROW_TILE, LANES), xn.dtype), pltpu.SemaphoreType.DMA((2,))]
    return pl.pallas_call(
        functools.partial(_mixer_kernel, send is not None),
        out_shape=out_shape,
        grid=(n_batch, n_j),
        in_specs=in_specs,
        out_specs=out_specs,
        scratch_shapes=scratch,
        compiler_params=pltpu.CompilerParams(
            dimension_semantics=("arbitrary", "arbitrary"), vmem_limit_bytes=VMEM_LIMIT),
        name="mixer_send" if send is not None else "mixer",
    )(*args)


def _router_kernel(h1_ref, gffn_ref, wr_ref, bias_ref, wgs_ref, wus_ref, wds_ref, tri_ref,
                   base_ref, xn_ref, idx_ref, gate_ref, rank_ref, counts_ref, carry_ref):
    i = pl.program_id(0)
    n_tok = h1_ref.shape[0]

    @pl.when(i == 0)
    def _():
        carry_ref[...] = jnp.zeros_like(carry_ref)

    h1 = h1_ref[...]
    xn = _rms(h1, gffn_ref[...])
    _store_row_tiles(xn_ref, _pack_rows(xn))
    xb = xn.astype(jnp.bfloat16)

    g_pre = _dot(xb, wgs_ref[...])
    gate_s, _ = _sigmoid_pair(g_pre)
    hid = (g_pre * gate_s) * _dot(xb, wus_ref[...])
    base_ref[...] = h1 + _dot(hid.astype(jnp.bfloat16), wds_ref[...])

    x_lo = (xn - xb.astype(jnp.float32)).astype(jnp.bfloat16)
    wr = wr_ref[...]
    w_hi = wr.astype(jnp.bfloat16)
    w_lo = (wr - w_hi.astype(jnp.float32)).astype(jnp.bfloat16)
    logits = _dot_nt(w_hi, xb) + _dot_nt(w_hi, x_lo) + _dot_nt(w_lo, xb)
    scores, _ = _sigmoid_pair(logits)
    sel = scores + bias_ref[...]
    eid = lax.broadcasted_iota(jnp.int32, (N_EXPERTS, n_tok), 0).astype(jnp.float32)
    picks, top_s = [], []
    for _ in range(TOP_K):
        best = jnp.max(sel, axis=0, keepdims=True)
        pick = jnp.min(jnp.where(sel == best, eid, float(N_EXPERTS)), axis=0, keepdims=True)
        hit = eid == pick
        top_s.append(jnp.sum(jnp.where(hit, scores, 0.0), axis=0, keepdims=True))
        sel = jnp.where(hit, -jnp.inf, sel)
        picks.append(pick)
    top_s = jnp.concatenate(top_s, axis=0)
    gate_ref[...] = top_s / jnp.sum(top_s, axis=0, keepdims=True) * ROUTED_SCALE
    idx_ref[...] = jnp.concatenate(picks, axis=0).astype(jnp.int32)

    chosen = jnp.zeros((N_EXPERTS, n_tok), jnp.float32)
    for pick in picks:
        chosen = chosen + jnp.where(eid == pick, 1.0, 0.0)
    incl = _dot(chosen.astype(jnp.bfloat16), tri_ref[...])
    before = carry_ref[...] + incl - 1.0
    ranks = [jnp.sum(jnp.where(eid == pick, before, 0.0), axis=0, keepdims=True) for pick in picks]
    rank_ref[...] = jnp.concatenate(ranks, axis=0).astype(jnp.int32)
    carry_ref[...] = carry_ref[...] + incl[:, n_tok - 1:n_tok]
    counts_ref[...] = jnp.broadcast_to(carry_ref[...], counts_ref.shape).astype(jnp.int32)


def _router(h1, gffn, w_router, bias, wgs, wus, wds):
    n, d = h1.shape
    t = ROUTE_ROWS
    ff = wgs.shape[1]
    tri = jnp.asarray(np.triu(np.ones((t, t), np.float32)), jnp.bfloat16)
    const = lambda *shape: pl.BlockSpec(shape, lambda i: (0,) * len(shape))
    tok = lambda width: pl.BlockSpec((t, width), lambda i: (i, 0))
    slot = pl.BlockSpec((TOP_K, t), lambda i: (0, i))
    return pl.pallas_call(
        _router_kernel,
        out_shape=(
            jax.ShapeDtypeStruct((n, d), jnp.float32),
            jax.ShapeDtypeStruct((n * ROW_TILE, LANES), jnp.uint32),
            jax.ShapeDtypeStruct((TOP_K, n), jnp.int32),
            jax.ShapeDtypeStruct((TOP_K, n), jnp.float32),
            jax.ShapeDtypeStruct((TOP_K, n), jnp.int32),
            jax.ShapeDtypeStruct((N_EXPERTS, 128), jnp.int32),
        ),
        grid=(n // t,),
        in_specs=[tok(d), const(1, d), const(N_EXPERTS, d), const(N_EXPERTS, 1),
                  const(d, ff), const(d, ff), const(ff, d), const(t, t)],
        out_specs=(tok(d), pl.BlockSpec((t * ROW_TILE, LANES), lambda i: (i, 0)), slot, slot, slot,
                   const(N_EXPERTS, 128)),
        scratch_shapes=[pltpu.VMEM((N_EXPERTS, 1), jnp.float32)],
        compiler_params=pltpu.CompilerParams(
            dimension_semantics=("arbitrary",), vmem_limit_bytes=VMEM_LIMIT),
        name="router",
    )(h1, gffn, w_router, bias, wgs, wus, wds, tri)


def _slab(ref, first_sublane):
    if not isinstance(first_sublane, int):
        first_sublane = pl.multiple_of(first_sublane, ROW_TILE)
    return ref.at[pl.ds(first_sublane, ROW_TILE)]


def _start_rows_out(tok, rows_ref, off_ref, dst_hbm, sem):
    for s in range(TOP_K):
        pltpu.make_async_copy(_slab(rows_ref, tok * ROW_TILE), _slab(dst_hbm, off_ref[0, tok * TOP_K + s]),
                              sem).start(priority=s % DMA_QUEUES)


def _wait_rows_out(rows_ref, dst_hbm, sem):
    for s in range(TOP_K):
        pltpu.make_async_copy(rows_ref, dst_hbm.at[pl.ds(0, rows_ref.shape[0])], sem).wait()


def _start_rows_in(tok, n_tok, src_hbm, off_ref, buf_ref, sem):
    for s in range(TOP_K):
        pltpu.make_async_copy(_slab(src_hbm, off_ref[0, tok * TOP_K + s]),
                              _slab(buf_ref, (s * n_tok + tok) * ROW_TILE), sem).start(priority=s % DMA_QUEUES)


def _wait_rows_in(src_hbm, buf_ref, sem):
    pltpu.make_async_copy(src_hbm.at[pl.ds(0, buf_ref.shape[0])], buf_ref, sem).wait()


def _weighted_sum(base_ref, gates_ref, gfin_ref, buf_ref, out_ref):
    n_tok = base_ref.shape[0]
    gates = gates_ref[...]
    width = ROW_TILE * LANES
    lo_chunks, hi_chunks = [], []
    for c in range(ROW_TILE):
        acc_lo = base_ref[:, c * LANES:(c + 1) * LANES]
        acc_hi = base_ref[:, width + c * LANES:width + (c + 1) * LANES]
        for s in range(TOP_K):
            lo, hi = _unpack_words(buf_ref[pl.ds(s * n_tok * ROW_TILE + c, n_tok, stride=ROW_TILE), :])
            acc_lo = acc_lo + gates[:, s:s + 1] * lo
            acc_hi = acc_hi + gates[:, s:s + 1] * hi
        lo_chunks.append(acc_lo)
        hi_chunks.append(acc_hi)
    out_ref[...] = _rms(jnp.concatenate(lo_chunks + hi_chunks, axis=-1), gfin_ref[...])


def _offsets_spec(n_tok, index_map):
    return pl.BlockSpec((None, 1, n_tok * TOP_K), index_map, memory_space=pltpu.SMEM)


def _swiglu_block(n_valid, xs_ref, wgb_ref, wub_ref, wdb_ref, y_ref):
    r = EXPERT_ROWS
    keep = lax.broadcasted_iota(jnp.int32, (r, LANES), 0) < n_valid
    lo, hi = _load_row_tiles(xs_ref, r)
    xb = jnp.concatenate([jnp.where(keep, c, 0.0).astype(jnp.bfloat16) for c in lo + hi], axis=-1)
    g = _dot(xb, wgb_ref[...])
    u = _dot(xb, wub_ref[...])
    g_sig, _ = _sigmoid_pair(g)
    hid = ((g * g_sig) * u).astype(jnp.bfloat16)
    _store_row_tiles(y_ref, _pack_rows(_dot(hid, wdb_ref[...])))


def _refresh_weights(i, blk_e_ref, wg_ref, wu_ref, wd_ref, wgb_ref, wub_ref, wdb_ref):
    @pl.when((i == 0) | (blk_e_ref[i] != blk_e_ref[jnp.maximum(i - 1, 0)]))
    def _():
        wgb_ref[...] = wg_ref[...].astype(jnp.bfloat16)
        wub_ref[...] = wu_ref[...].astype(jnp.bfloat16)
        wdb_ref[...] = wd_ref[...].astype(jnp.bfloat16)


def _experts_out_kernel(side_steps, blk_e_ref, blk_rows_ref, blk_i_ref, xs_ref, wg_ref, wu_ref, wd_ref, off_ref,
                        rows_ref, y_ref, dst_hbm, wgb_ref, wub_ref, wdb_ref, stage_ref, sem):
    del blk_i_ref
    i = pl.program_id(0)
    n_valid = blk_rows_ref[i]
    side = i < side_steps
    slot = i % 2
    _refresh_weights(i, blk_e_ref, wg_ref, wu_ref, wd_ref, wgb_ref, wub_ref, wdb_ref)

    def send():
        stage_ref[slot] = rows_ref[...]
        for tok in range(SIDE_ROWS):
            _start_rows_out(tok, stage_ref.at[slot], off_ref, dst_hbm, sem.at[slot])

    @pl.when((n_valid > 0) & side)
    def _():
        send()
        _swiglu_block(n_valid, xs_ref, wgb_ref, wub_ref, wdb_ref, y_ref)

    @pl.when((n_valid > 0) & jnp.logical_not(side))
    def _():
        _swiglu_block(n_valid, xs_ref, wgb_ref, wub_ref, wdb_ref, y_ref)

    @pl.when((n_valid == 0) & side)
    def _():
        send()

    @pl.when((i >= 1) & (i <= side_steps))
    def _():
        _wait_rows_out(stage_ref.at[1 - slot], dst_hbm, sem.at[1 - slot])


def _experts_in_kernel(side_steps, blk_e_ref, blk_rows_ref, blk_i_ref, xs_ref, wg_ref, wu_ref, wd_ref,
                       off_ref, off_nx_ref, off_nx2_ref, base_ref, gates_ref, gfin_ref, src_hbm, y_ref, out_ref,
                       wgb_ref, wub_ref, wdb_ref, buf_ref, sem):
    del blk_i_ref
    i = pl.program_id(0)
    n_valid = blk_rows_ref[i]
    slot = i % 2
    _refresh_weights(i, blk_e_ref, wg_ref, wu_ref, wd_ref, wgb_ref, wub_ref, wdb_ref)

    def fetch(off_r, into):
        for tok in range(SIDE_ROWS):
            _start_rows_in(tok, SIDE_ROWS, src_hbm, off_r, buf_ref.at[into], sem.at[into])

    @pl.when(i == 0)
    def _():
        fetch(off_ref, 0)
        fetch(off_nx_ref, 1)

    @pl.when(i < side_steps)
    def _():
        _wait_rows_in(src_hbm, buf_ref.at[slot], sem.at[slot])
        _weighted_sum(base_ref, gates_ref, gfin_ref, buf_ref.at[slot], out_ref)

    more = i + 2 < side_steps

    @pl.when((n_valid > 0) & more)
    def _():
        fetch(off_nx2_ref, slot)
        _swiglu_block(n_valid, xs_ref, wgb_ref, wub_ref, wdb_ref, y_ref)

    @pl.when((n_valid > 0) & jnp.logical_not(more))
    def _():
        _swiglu_block(n_valid, xs_ref, wgb_ref, wub_ref, wdb_ref, y_ref)

    @pl.when((n_valid == 0) & more)
    def _():
        fetch(off_nx2_ref, slot)


def _expert_specs(d, ff):
    r = EXPERT_ROWS
    rows = pl.BlockSpec((r * ROW_TILE, LANES), lambda i, be, br, bi: (bi[i], 0))
    weights = [pl.BlockSpec((None, d, ff), lambda i, be, br, bi: (be[i], 0, 0)),
               pl.BlockSpec((None, d, ff), lambda i, be, br, bi: (be[i], 0, 0)),
               pl.BlockSpec((None, ff, d), lambda i, be, br, bi: (be[i], 0, 0))]
    scratch = [pltpu.VMEM((d, ff), jnp.bfloat16), pltpu.VMEM((d, ff), jnp.bfloat16),
               pltpu.VMEM((ff, d), jnp.bfloat16)]
    return rows, weights, scratch


def _experts_out(blk, xs, wg, wu, wd, off, xn, first_block, n_rows_next):
    side_steps = off.shape[0]
    d, ff = wg.shape[1], wg.shape[2]
    rows, weights, scratch = _expert_specs(d, ff)
    side = lambda i, be, br, bi: jnp.minimum(i, side_steps - 1)
    return pl.pallas_call(
        functools.partial(_experts_out_kernel, side_steps),
        out_shape=(jax.ShapeDtypeStruct(xs.shape, xs.dtype),
                   jax.ShapeDtypeStruct((n_rows_next * ROW_TILE, LANES), xs.dtype)),
        grid_spec=pltpu.PrefetchScalarGridSpec(
            num_scalar_prefetch=3,
            grid=(xs.shape[0] // (EXPERT_ROWS * ROW_TILE),),
            in_specs=[rows] + weights + [
                _offsets_spec(SIDE_ROWS, lambda i, be, br, bi: (side(i, be, br, bi), 0, 0)),
                pl.BlockSpec((SIDE_ROWS * ROW_TILE, LANES), lambda i, be, br, bi: (first_block + side(i, be, br, bi), 0))],
            out_specs=(rows, pl.BlockSpec(memory_space=pl.ANY)),
            scratch_shapes=scratch + [pltpu.VMEM((2, SIDE_ROWS * ROW_TILE, LANES), xs.dtype),
                                      pltpu.SemaphoreType.DMA((2,))],
        ),
        compiler_params=pltpu.CompilerParams(
            dimension_semantics=("arbitrary",), vmem_limit_bytes=VMEM_LIMIT),
        name="experts_send",
    )(*blk, xs, wg, wu, wd, off, xn)


def _experts_in(blk, xs, wg, wu, wd, off, base, gates, gfin, y_prev, n):
    side_steps = off.shape[0]
    d, ff = wg.shape[1], wg.shape[2]
    rows, weights, scratch = _expert_specs(d, ff)
    side = lambda i, be, br, bi: jnp.minimum(i, side_steps - 1)
    nxt = lambda i, be, br, bi: jnp.minimum(i + 1, side_steps - 1)
    nxt2 = lambda i, be, br, bi: jnp.minimum(i + 2, side_steps - 1)
    return pl.pallas_call(
        functools.partial(_experts_in_kernel, side_steps),
        out_shape=(jax.ShapeDtypeStruct(xs.shape, xs.dtype), jax.ShapeDtypeStruct((n, d), jnp.float32)),
        grid_spec=pltpu.PrefetchScalarGridSpec(
            num_scalar_prefetch=3,
            grid=(xs.shape[0] // (EXPERT_ROWS * ROW_TILE),),
            in_specs=[rows] + weights + [
                _offsets_spec(SIDE_ROWS, lambda i, be, br, bi: (side(i, be, br, bi), 0, 0)),
                _offsets_spec(SIDE_ROWS, lambda i, be, br, bi: (nxt(i, be, br, bi), 0, 0)),
                _offsets_spec(SIDE_ROWS, lambda i, be, br, bi: (nxt2(i, be, br, bi), 0, 0)),
                pl.BlockSpec((SIDE_ROWS, d), lambda i, be, br, bi: (side(i, be, br, bi), 0)),
                pl.BlockSpec((SIDE_ROWS, TOP_K), lambda i, be, br, bi: (side(i, be, br, bi), 0)),
                pl.BlockSpec((1, d), lambda i, be, br, bi: (0, 0)),
                pl.BlockSpec(memory_space=pl.ANY)],
            out_specs=(rows, pl.BlockSpec((SIDE_ROWS, d), lambda i, be, br, bi: (side(i, be, br, bi), 0))),
            scratch_shapes=scratch + [pltpu.VMEM((2, TOP_K * SIDE_ROWS * ROW_TILE, LANES), xs.dtype),
                                      pltpu.SemaphoreType.DMA((2,))],
        ),
        compiler_params=pltpu.CompilerParams(
            dimension_semantics=("arbitrary",), vmem_limit_bytes=VMEM_LIMIT),
        name="experts_fetch",
    )(*blk, xs, wg, wu, wd, off, off, off, base, gates, gfin, y_prev)


def _combine_kernel(off_ref, off_nx_ref, base_ref, gates_ref, gfin_ref, y_hbm, out_in_ref, out_ref,
                    buf0_ref, buf1_ref, sem):
    del out_in_ref
    i = pl.program_id(0)
    n_steps = pl.num_programs(0)
    n_tok = base_ref.shape[0]
    bufs = (buf0_ref, buf1_ref)

    @pl.when(i == 0)
    def _():
        def body(g, carry):
            for u in range(ISSUE_UNROLL):
                _start_rows_in(g * ISSUE_UNROLL + u, n_tok, y_hbm, off_ref, buf0_ref, sem.at[0])
            return carry
        lax.fori_loop(0, n_tok // ISSUE_UNROLL, body, 0)

    for p in range(2):
        def step(fetch_next, p=p):
            _wait_rows_in(y_hbm, bufs[p], sem.at[p])
            if fetch_next:
                for tok in range(n_tok):
                    _start_rows_in(tok, n_tok, y_hbm, off_nx_ref, bufs[1 - p], sem.at[1 - p])
            _weighted_sum(base_ref, gates_ref, gfin_ref, bufs[p], out_ref)

        pl.when((i % 2 == p) & (i + 1 < n_steps))(functools.partial(step, True))
        pl.when((i % 2 == p) & (i + 1 >= n_steps))(functools.partial(step, False))


def _combine(off, base, gates, gfin, y, out_prev, first_block):
    d = base.shape[1]
    t = COMBINE_ROWS
    n_steps = off.shape[0]
    tok = lambda width, first=0: pl.BlockSpec((t, width), lambda i: (first + i, 0))
    return pl.pallas_call(
        _combine_kernel,
        out_shape=jax.ShapeDtypeStruct(out_prev.shape, jnp.float32),
        grid=(n_steps,),
        in_specs=[_offsets_spec(t, lambda i: (i, 0, 0)),
                  _offsets_spec(t, lambda i: (jnp.minimum(i + 1, n_steps - 1), 0, 0)),
                  tok(d), tok(TOP_K), pl.BlockSpec((1, d), lambda i: (0, 0)),
                  pl.BlockSpec(memory_space=pl.ANY), pl.BlockSpec(memory_space=pl.ANY)],
        out_specs=tok(d, first_block),
        scratch_shapes=[pltpu.VMEM((TOP_K * t * ROW_TILE, LANES), y.dtype),
                        pltpu.VMEM((TOP_K * t * ROW_TILE, LANES), y.dtype),
                        pltpu.SemaphoreType.DMA((2,))],
        input_output_aliases={6: 0},
        compiler_params=pltpu.CompilerParams(
            dimension_semantics=("arbitrary",), vmem_limit_bytes=VMEM_LIMIT),
        name="combine",
    )(off, off, base, gates, gfin, y, out_prev)


def _group_layout(counts, idx, rank):
    r = EXPERT_ROWS
    n_g = idx.shape[1]
    n_blocks = (n_g * TOP_K) // r + N_EXPERTS
    padded = (counts + r - 1) // r * r
    ends = jnp.cumsum(padded)
    starts = (ends - padded).astype(jnp.int32)
    blk_row0 = jnp.arange(n_blocks, dtype=jnp.int32) * r
    blk_e = jnp.minimum(jnp.sum((ends[None, :] <= blk_row0[:, None]).astype(jnp.int32), axis=1), N_EXPERTS - 1)
    onehot = (blk_e[:, None] == jnp.arange(N_EXPERTS, dtype=jnp.int32)[None, :]).astype(jnp.int32)
    blk_rows = jnp.clip(onehot @ counts - (blk_row0 - onehot @ starts), 0, r).astype(jnp.int32)
    dest = rank + jnp.sum(jnp.where(idx[None] == jnp.arange(N_EXPERTS, dtype=jnp.int32)[:, None, None],
                                    starts[:, None, None], 0), axis=0)
    blk_i = jnp.minimum(jnp.arange(n_blocks, dtype=jnp.int32), ends[-1].astype(jnp.int32) // r - 1)
    return (blk_e, blk_rows, blk_i), (dest * ROW_TILE).T.reshape(-1), n_blocks * r


def kernel(x, meta_tokens, norm_mix_g, w_in, lb_table, hgrn_norm_g, conv_w, conv_norm_g, w_out,
           norm_ffn_g, w_router, router_bias, w_gate_e, w_up_e, w_down_e, w_gate_s, w_up_s, w_down_s,
           norm_final_g):
    bsz, seq, d = x.shape
    n = bsz * seq
    assert TOKEN_GROUPS == 2 and bsz % TOKEN_GROUPS == 0
    b_g = bsz // TOKEN_GROUPS
    n_g = n // TOKEN_GROUPS
    bf = jnp.bfloat16
    assert seq % MIX_ROWS == 0 and MIX_ROWS % CHUNK == 0
    assert n_g % ROUTE_ROWS == 0 and n_g % COMBINE_ROWS == 0
    assert (n_g * TOP_K) % EXPERT_ROWS == 0 and n_g % SIDE_ROWS == 0
    assert 2 <= n_g // SIDE_ROWS < (n_g * TOP_K) // EXPERT_ROWS + N_EXPERTS

    meta_pad = jnp.zeros((CHUNK, d), jnp.float32).at[CHUNK - N_META:].set(meta_tokens)
    mix = functools.partial(_mixer, meta_pad=meta_pad, gmix=norm_mix_g[0:1], w_in=w_in[0].astype(bf),
                            lb_table=lb_table, ghg=hgrn_norm_g[0:1], conv_w=conv_w[0], gcv=conv_norm_g[0:1],
                            w_out=w_out[0].astype(bf))
    route = functools.partial(_router, gffn=norm_ffn_g[0:1], w_router=w_router[0].T, bias=router_bias[0][:, None],
                              wgs=w_gate_s[0].astype(bf), wus=w_up_s[0].astype(bf), wds=w_down_s[0].astype(bf))
    steps = lambda off, t: off.reshape(n_g // t, 1, t * TOP_K)
    gfin = norm_final_g[None, :]
    wg, wu, wd = w_gate_e[0], w_up_e[0], w_down_e[0]

    h1_0 = mix(x, 0, b_g)
    base0, xn0, idx0, gate0, rank0, counts0 = route(h1_0.reshape(n_g, d))
    blk0, off0, rows0 = _group_layout(counts0[:, 0], idx0, rank0)
    h1_1, xs0 = mix(x, b_g, b_g, send=(steps(off0, MIX_ROWS), xn0, rows0))
    base1, xn1, idx1, gate1, rank1, counts1 = route(h1_1.reshape(n_g, d))
    blk1, off1, rows1 = _group_layout(counts1[:, 0], idx1, rank1)
    y0, xs1 = _experts_out(blk0, xs0, wg, wu, wd, steps(off1, SIDE_ROWS), xn1, 0, rows1)
    y1, out = _experts_in(blk1, xs1, wg, wu, wd, steps(off0, SIDE_ROWS), base0, gate0.T, gfin, y0, n)
    out = _combine(steps(off1, COMBINE_ROWS), base1, gate1.T, gfin, y1, out, n_g // COMBINE_ROWS)
    return out.reshape(bsz, seq, d)
```

```python
import functools

import numpy as np
import jax
import jax.numpy as jnp
from jax import lax
from jax.experimental import pallas as pl
from jax.experimental.pallas import tpu as pltpu

N_META = 16
CHUNK = 128
HEADS = 4
HEAD_DIM = 128
HGRN_W = HEADS * HEAD_DIM
CONV_W = 512
CONV_GROUPS = 4
CONV_K = 3
N_EXPERTS = 64
TOP_K = 8
ROUTED_SCALE = 2.5
EPS = 1e-6

V7X_VMEM_BYTES = 64 * 1024 * 1024
VMEM_LIMIT = V7X_VMEM_BYTES - 8 * 1024 * 1024

MIX_ROWS = 512
CHUNK_UNROLL = 4
ROUTE_ROWS = 512
EXPERT_ROWS = 1024
COMBINE_ROWS = 256
EVEN_QUEUES = (0, 1, 0, 1, 0, 1, 0, 1)
SIDE_QUEUES = (0, 1, 1, 0, 1, 1, 0, 1)
ISSUE_UNROLL = 8
TOKEN_GROUPS = 2
SIDE_ROWS = 128

LANES = 128
ROW_TILE = 4

HALF_SPANS = (64, 32, 16, 8, 4, 2, 1)
N_LEVELS = len(HALF_SPANS) + 1


def _decay_sum_matrix():
    a = np.zeros((N_LEVELS, CHUNK, CHUNK), np.float32)
    a[0] = np.tril(np.ones((CHUNK, CHUNK), np.float32))
    for i, m in enumerate(HALF_SPANS):
        for t in range(CHUNK):
            mid = (t // (2 * m)) * 2 * m + m
            if t >= mid:
                a[1 + i, t, mid:t + 1] = 1.0
            else:
                a[1 + i, t, t + 1:mid] = 1.0
    return a.reshape(N_LEVELS * CHUNK, CHUNK)


def _level_matrix():
    lv = np.full((CHUNK, CHUNK), -1, np.int32)
    for t in range(CHUNK):
        lv[t, t] = len(HALF_SPANS)
        for s in range(t):
            top = (t ^ s).bit_length() - 1
            lv[t, s] = HALF_SPANS.index(1 << top)
    return lv


def _rms(x, g):
    return x * lax.rsqrt(jnp.mean(x * x, axis=-1, keepdims=True) + EPS) * g


def _group_rms(x, g, width):
    outs = []
    for j in range(x.shape[-1] // width):
        xs = x[:, j * width:(j + 1) * width]
        outs.append(xs * lax.rsqrt(jnp.mean(xs * xs, axis=-1, keepdims=True) + EPS))
    return jnp.concatenate(outs, axis=-1) * g


def _sigmoid_pair(z):
    t = jnp.exp(-jnp.abs(z))
    inv = 1.0 / (1.0 + t)
    big, small = inv, t * inv
    pos = z >= 0
    return jnp.where(pos, big, small), jnp.where(pos, small, big)


def _pack_rows(x):
    half = x.shape[1] // 2
    bits = lambda v: lax.bitcast_convert_type(v.astype(jnp.bfloat16).astype(jnp.float32), jnp.uint32)
    return (bits(x[:, :half]) >> 16) | (bits(x[:, half:]) & jnp.uint32(0xFFFF0000))


def _unpack_words(w):
    lo = lax.bitcast_convert_type(w << 16, jnp.float32)
    hi = lax.bitcast_convert_type(w & jnp.uint32(0xFFFF0000), jnp.float32)
    return lo, hi


def _store_row_tiles(ref, words):
    t = words.shape[0]
    for c in range(ROW_TILE):
        ref[pl.ds(c, t, stride=ROW_TILE), :] = words[:, c * LANES:(c + 1) * LANES]


def _load_row_tiles(ref, t):
    parts = [_unpack_words(ref[pl.ds(c, t, stride=ROW_TILE), :]) for c in range(ROW_TILE)]
    return [p[0] for p in parts], [p[1] for p in parts]


def _dot(a, b):
    return jnp.dot(a, b, preferred_element_type=jnp.float32)


def _dot_nt(a, b):
    return lax.dot_general(a, b, (((1,), (1,)), ((), ())), preferred_element_type=jnp.float32)


def _dot_tn(a, b):
    return lax.dot_general(a, b, (((0,), (0,)), ((), ())), preferred_element_type=jnp.float32)


def _hgrn_chunk(q, z, iv, lb, amat, level, st_ref, first_valid_row):
    sig, sig_neg = _sigmoid_pair(z)
    lf = jnp.log(lb + (1.0 - lb) * sig)
    k = (1.0 - lb) * sig_neg
    row = lax.broadcasted_iota(jnp.int32, (CHUNK, HGRN_W), 0)
    if first_valid_row:
        valid = row >= first_valid_row
        lf = jnp.where(valid, lf, 0.0)
        k = jnp.where(valid, k, 0.0)

    h1 = lf.astype(jnp.bfloat16)
    h2 = (lf - h1.astype(jnp.float32)).astype(jnp.bfloat16)
    e_all = _dot(amat, jnp.concatenate([h1, h2], axis=0))

    b = e_all[0:CHUNK]
    b_last = b[CHUNK - 1:CHUNK]
    q_in = (q * jnp.exp(b)).astype(jnp.bfloat16)
    k_out = (k * jnp.exp(b_last - b)).astype(jnp.bfloat16)
    st_decay = jnp.exp(b_last)
    v_bf = iv.astype(jnp.bfloat16)

    q_lv = [q.astype(jnp.bfloat16)]
    k_lv = [k.astype(jnp.bfloat16)]
    for i, m in enumerate(HALF_SPANS):
        ex = jnp.exp(e_all[(1 + i) * CHUNK:(2 + i) * CHUNK])
        right = (row & m) != 0
        q_lv.append(jnp.where(right, q * ex, 0.0).astype(jnp.bfloat16))
        k_lv.append(jnp.where(right, 0.0, k * ex).astype(jnp.bfloat16))
    lv_of = [len(HALF_SPANS)] + list(range(len(HALF_SPANS)))

    outs = []
    for h in range(HEADS):
        cols = slice(h * HEAD_DIM, (h + 1) * HEAD_DIM)
        scores = jnp.zeros((CHUNK, CHUNK), jnp.float32)
        for ql, kl, lv in zip(q_lv, k_lv, lv_of):
            scores = jnp.where(level == lv, _dot_nt(ql[:, cols], kl[:, cols]), scores)
        st = st_ref[h]
        o = _dot(scores.astype(jnp.bfloat16), v_bf[:, cols]) + _dot_nt(q_in[:, cols], st.astype(jnp.bfloat16))
        st_ref[h] = st * st_decay[:, cols] + _dot_tn(v_bf[:, cols], k_out[:, cols])
        outs.append(o)
    return jnp.concatenate(outs, axis=-1)


def _mixer_kernel(with_send, x_ref, meta_ref, gmix_ref, win_ref, lbt_ref, ghg_ref, cw_ref, gcv_ref, wout_ref,
                  amat_ref, level_ref, *rest):
    if with_send:
        off_ref, rows_ref, h1_ref, dst_hbm, proj_ref, o_ref, u_ref, st_ref, stage_ref, sem = rest
    else:
        h1_ref, proj_ref, o_ref, u_ref, st_ref = rest
    j = pl.program_id(1)
    rows = x_ref.shape[0]
    n_in = win_ref.shape[1]

    lbt = lbt_ref[...]
    lbe = jnp.exp(lbt - jnp.max(lbt, axis=0, keepdims=True))
    lb = lbe[0:1] / jnp.sum(lbe, axis=0, keepdims=True)

    amat = amat_ref[...]
    level = level_ref[...]
    gmix = gmix_ref[...]

    def project(xv, dst_rows):
        xn = _rms(xv, gmix).astype(jnp.bfloat16)
        for c0 in range(0, n_in, 512):
            proj_ref[dst_rows, c0:c0 + 512] = _dot(xn, win_ref[:, c0:c0 + 512])

    @pl.when(j == 0)
    def _():
        st_ref[...] = jnp.zeros_like(st_ref)
        project(meta_ref[...], pl.ds(0, CHUNK))
        pm = proj_ref[0:CHUNK, :]
        _hgrn_chunk(pm[:, 0:512], pm[:, 512:1024], pm[:, 1024:1536], lb, amat, level, st_ref,
                    CHUNK - N_META)
        u_ref[0:8, :] = (pm[:, 2560:3072] * pm[:, 3072:3584])[CHUNK - 8:CHUNK]

    if with_send:
        step = pl.program_id(0) * pl.num_programs(1) + j
        slot = step % 2
        stage_ref[slot] = rows_ref[...]
        for tok in range(rows_ref.shape[0] // ROW_TILE):
            _start_rows_out(tok, stage_ref.at[slot], off_ref, dst_hbm, sem.at[slot], SIDE_QUEUES)

    project(x_ref[...], pl.ds(0, rows))

    def chunk_body(c, carry):
        for u in range(CHUNK_UNROLL):
            r0 = pl.multiple_of((c * CHUNK_UNROLL + u) * CHUNK, CHUNK)
            q = proj_ref[pl.ds(r0, CHUNK), 0:512]
            z = proj_ref[pl.ds(r0, CHUNK), 512:1024]
            iv = proj_ref[pl.ds(r0, CHUNK), 1024:1536]
            o_ref[pl.ds(r0, CHUNK), :] = _hgrn_chunk(q, z, iv, lb, amat, level, st_ref, 0)
        return carry

    lax.fori_loop(0, rows // (CHUNK * CHUNK_UNROLL), chunk_body, 0)

    g_out = proj_ref[:, 1536:2048]
    g_sig, _ = _sigmoid_pair(g_out)
    y_hgrn = _group_rms(o_ref[...], ghg_ref[...], HEAD_DIM) * (g_out * g_sig)

    u = proj_ref[:, 2560:3072] * proj_ref[:, 3072:3584]
    u_ref[8:8 + rows, :] = u
    cw = cw_ref[...]
    y = cw[2:3] * u + cw[1:2] * u_ref[7:7 + rows, :] + cw[0:1] * u_ref[6:6 + rows, :]
    u_ref[0:8, :] = u[rows - 8:rows]
    y_conv = _group_rms(proj_ref[:, 2048:2560] * y, gcv_ref[...], CONV_W // CONV_GROUPS)

    mixed = jnp.concatenate([y_hgrn, y_conv], axis=-1).astype(jnp.bfloat16)
    h1_ref[...] = x_ref[...] + _dot(mixed, wout_ref[...])

    if with_send:
        @pl.when(step >= 1)
        def _():
            _wait_rows_out(stage_ref.at[1 - slot], dst_hbm, sem.at[1 - slot])

        @pl.when(step == pl.num_programs(0) * pl.num_programs(1) - 1)
        def _():
            _wait_rows_out(stage_ref.at[slot], dst_hbm, sem.at[slot])


def _mixer(x, batch0, n_batch, meta_pad, gmix, w_in, lb_table, ghg, conv_w, gcv, w_out, send=None):
    _, seq, d = x.shape
    n_in = w_in.shape[1]
    rows = MIX_ROWS
    n_j = seq // rows
    const = lambda *shape: pl.BlockSpec(shape, lambda b, j: (0,) * len(shape))
    in_specs = [
        pl.BlockSpec((None, rows, d), lambda b, j: (batch0 + b, j, 0)),
        const(CHUNK, d), const(1, d), const(d, n_in), const(*lb_table.shape), const(1, HGRN_W),
        const(CONV_K, CONV_W), const(1, CONV_W), const(d, d),
        const(N_LEVELS * CHUNK, 2 * CHUNK), const(CHUNK, CHUNK),
    ]
    args = [x, meta_pad, gmix, w_in, lb_table, ghg, conv_w, gcv, w_out,
            jnp.asarray(np.tile(_decay_sum_matrix(), (1, 2)), jnp.bfloat16), jnp.asarray(_level_matrix())]
    out_shape = jax.ShapeDtypeStruct((n_batch, seq, d), jnp.float32)
    out_specs = pl.BlockSpec((None, rows, d), lambda b, j: (b, j, 0))
    scratch = [
        pltpu.VMEM((rows, n_in), jnp.float32),
        pltpu.VMEM((rows, HGRN_W), jnp.float32),
        pltpu.VMEM((rows + 8, CONV_W), jnp.float32),
        pltpu.VMEM((HEADS, HEAD_DIM, HEAD_DIM), jnp.float32),
    ]
    if send is not None:
        off, xn, n_rows = send
        assert off.shape[0] == n_batch * n_j and off.shape[2] == rows * TOP_K
        in_specs += [_offsets_spec(rows, lambda b, j: (b * n_j + j, 0, 0)),
                     pl.BlockSpec((rows * ROW_TILE, LANES), lambda b, j: (b * n_j + j, 0))]
        args += [off, xn]
        out_shape = (out_shape, jax.ShapeDtypeStruct((n_rows * ROW_TILE, LANES), xn.dtype))
        out_specs = (out_specs, pl.BlockSpec(memory_space=pl.ANY))
        scratch += [pltpu.VMEM((2, rows * ROW_TILE, LANES), xn.dtype), pltpu.SemaphoreType.DMA((2,))]
    return pl.pallas_call(
        functools.partial(_mixer_kernel, send is not None),
        out_shape=out_shape,
        grid=(n_batch, n_j),
        in_specs=in_specs,
        out_specs=out_specs,
        scratch_shapes=scratch,
        compiler_params=pltpu.CompilerParams(
            dimension_semantics=("arbitrary", "arbitrary"), vmem_limit_bytes=VMEM_LIMIT),
        name="mixer_send" if send is not None else "mixer",
    )(*args)


def _router_kernel(h1_ref, gffn_ref, wr_ref, bias_ref, wgs_ref, wus_ref, wds_ref, tri_ref,
                   base_ref, xn_ref, idx_ref, gate_ref, rank_ref, counts_ref, carry_ref):
    i = pl.program_id(0)
    n_tok = h1_ref.shape[0]

    @pl.when(i == 0)
    def _():
        carry_ref[...] = jnp.zeros_like(carry_ref)

    h1 = h1_ref[...]
    xn = _rms(h1, gffn_ref[...])
    _store_row_tiles(xn_ref, _pack_rows(xn))
    xb = xn.astype(jnp.bfloat16)

    g_pre = _dot(xb, wgs_ref[...])
    gate_s, _ = _sigmoid_pair(g_pre)
    hid = (g_pre * gate_s) * _dot(xb, wus_ref[...])
    base_ref[...] = h1 + _dot(hid.astype(jnp.bfloat16), wds_ref[...])

    x_lo = (xn - xb.astype(jnp.float32)).astype(jnp.bfloat16)
    wr = wr_ref[...]
    w_hi = wr.astype(jnp.bfloat16)
    w_lo = (wr - w_hi.astype(jnp.float32)).astype(jnp.bfloat16)
    logits = _dot_nt(w_hi, xb) + _dot_nt(w_hi, x_lo) + _dot_nt(w_lo, xb)
    scores, _ = _sigmoid_pair(logits)
    sel = scores + bias_ref[...]
    eid = lax.broadcasted_iota(jnp.int32, (N_EXPERTS, n_tok), 0).astype(jnp.float32)
    picks, top_s = [], []
    for _ in range(TOP_K):
        best = jnp.max(sel, axis=0, keepdims=True)
        pick = jnp.min(jnp.where(sel == best, eid, float(N_EXPERTS)), axis=0, keepdims=True)
        hit = eid == pick
        top_s.append(jnp.sum(jnp.where(hit, scores, 0.0), axis=0, keepdims=True))
        sel = jnp.where(hit, -jnp.inf, sel)
        picks.append(pick)
    top_s = jnp.concatenate(top_s, axis=0)
    gate_ref[...] = top_s / jnp.sum(top_s, axis=0, keepdims=True) * ROUTED_SCALE
    idx_ref[...] = jnp.concatenate(picks, axis=0).astype(jnp.int32)

    chosen = jnp.zeros((N_EXPERTS, n_tok), jnp.float32)
    for pick in picks:
        chosen = chosen + jnp.where(eid == pick, 1.0, 0.0)
    incl = _dot(chosen.astype(jnp.bfloat16), tri_ref[...])
    before = carry_ref[...] + incl - 1.0
    ranks = [jnp.sum(jnp.where(eid == pick, before, 0.0), axis=0, keepdims=True) for pick in picks]
    rank_ref[...] = jnp.concatenate(ranks, axis=0).astype(jnp.int32)
    carry_ref[...] = carry_ref[...] + incl[:, n_tok - 1:n_tok]
    counts_ref[...] = jnp.broadcast_to(carry_ref[...], counts_ref.shape).astype(jnp.int32)


def _router(h1, gffn, w_router, bias, wgs, wus, wds):
    n, d = h1.shape
    t = ROUTE_ROWS
    ff = wgs.shape[1]
    tri = jnp.asarray(np.triu(np.ones((t, t), np.float32)), jnp.bfloat16)
    const = lambda *shape: pl.BlockSpec(shape, lambda i: (0,) * len(shape))
    tok = lambda width: pl.BlockSpec((t, width), lambda i: (i, 0))
    slot = pl.BlockSpec((TOP_K, t), lambda i: (0, i))
    return pl.pallas_call(
        _router_kernel,
        out_shape=(
            jax.ShapeDtypeStruct((n, d), jnp.float32),
            jax.ShapeDtypeStruct((n * ROW_TILE, LANES), jnp.uint32),
            jax.ShapeDtypeStruct((TOP_K, n), jnp.int32),
            jax.ShapeDtypeStruct((TOP_K, n), jnp.float32),
            jax.ShapeDtypeStruct((TOP_K, n), jnp.int32),
            jax.ShapeDtypeStruct((N_EXPERTS, 128), jnp.int32),
        ),
        grid=(n // t,),
        in_specs=[tok(d), const(1, d), const(N_EXPERTS, d), const(N_EXPERTS, 1),
                  const(d, ff), const(d, ff), const(ff, d), const(t, t)],
        out_specs=(tok(d), pl.BlockSpec((t * ROW_TILE, LANES), lambda i: (i, 0)), slot, slot, slot,
                   const(N_EXPERTS, 128)),
        scratch_shapes=[pltpu.VMEM((N_EXPERTS, 1), jnp.float32)],
        compiler_params=pltpu.CompilerParams(
            dimension_semantics=("arbitrary",), vmem_limit_bytes=VMEM_LIMIT),
        name="router",
    )(h1, gffn, w_router, bias, wgs, wus, wds, tri)


def _slab(ref, first_sublane):
    if not isinstance(first_sublane, int):
        first_sublane = pl.multiple_of(first_sublane, ROW_TILE)
    return ref.at[pl.ds(first_sublane, ROW_TILE)]


def _start_rows_out(tok, rows_ref, off_ref, dst_hbm, sem, queues=EVEN_QUEUES):
    for s in range(TOP_K):
        pltpu.make_async_copy(_slab(rows_ref, tok * ROW_TILE), _slab(dst_hbm, off_ref[0, tok * TOP_K + s]),
                              sem).start(priority=queues[s])


def _wait_rows_out(rows_ref, dst_hbm, sem):
    for s in range(TOP_K):
        pltpu.make_async_copy(rows_ref, dst_hbm.at[pl.ds(0, rows_ref.shape[0])], sem).wait()


def _start_rows_in(tok, n_tok, src_hbm, off_ref, buf_ref, sem, queues=EVEN_QUEUES):
    for s in range(TOP_K):
        pltpu.make_async_copy(_slab(src_hbm, off_ref[0, tok * TOP_K + s]),
                              _slab(buf_ref, (s * n_tok + tok) * ROW_TILE), sem).start(priority=queues[s])


def _wait_rows_in(src_hbm, buf_ref, sem):
    pltpu.make_async_copy(src_hbm.at[pl.ds(0, buf_ref.shape[0])], buf_ref, sem).wait()


def _weighted_sum(base_ref, gates_ref, gfin_ref, buf_ref, out_ref):
    n_tok = base_ref.shape[0]
    gates = gates_ref[...]
    width = ROW_TILE * LANES
    lo_chunks, hi_chunks = [], []
    for c in range(ROW_TILE):
        acc_lo = base_ref[:, c * LANES:(c + 1) * LANES]
        acc_hi = base_ref[:, width + c * LANES:width + (c + 1) * LANES]
        for s in range(TOP_K):
            lo, hi = _unpack_words(buf_ref[pl.ds(s * n_tok * ROW_TILE + c, n_tok, stride=ROW_TILE), :])
            acc_lo = acc_lo + gates[:, s:s + 1] * lo
            acc_hi = acc_hi + gates[:, s:s + 1] * hi
        lo_chunks.append(acc_lo)
        hi_chunks.append(acc_hi)
    out_ref[...] = _rms(jnp.concatenate(lo_chunks + hi_chunks, axis=-1), gfin_ref[...])


def _offsets_spec(n_tok, index_map):
    return pl.BlockSpec((None, 1, n_tok * TOP_K), index_map, memory_space=pltpu.SMEM)


def _swiglu_block(n_valid, xs_ref, wgb_ref, wub_ref, wdb_ref, y_ref):
    r = EXPERT_ROWS
    keep = lax.broadcasted_iota(jnp.int32, (r, LANES), 0) < n_valid
    lo, hi = _load_row_tiles(xs_ref, r)
    xb = jnp.concatenate([jnp.where(keep, c, 0.0).astype(jnp.bfloat16) for c in lo + hi], axis=-1)
    g = _dot(xb, wgb_ref[...])
    u = _dot(xb, wub_ref[...])
    g_sig, _ = _sigmoid_pair(g)
    hid = ((g * g_sig) * u).astype(jnp.bfloat16)
    _store_row_tiles(y_ref, _pack_rows(_dot(hid, wdb_ref[...])))


def _refresh_weights(i, blk_e_ref, wg_ref, wu_ref, wd_ref, wgb_ref, wub_ref, wdb_ref):
    @pl.when((i == 0) | (blk_e_ref[i] != blk_e_ref[jnp.maximum(i - 1, 0)]))
    def _():
        wgb_ref[...] = wg_ref[...].astype(jnp.bfloat16)
        wub_ref[...] = wu_ref[...].astype(jnp.bfloat16)
        wdb_ref[...] = wd_ref[...].astype(jnp.bfloat16)


def _experts_out_kernel(side_steps, blk_e_ref, blk_rows_ref, blk_i_ref, xs_ref, wg_ref, wu_ref, wd_ref, off_ref,
                        rows_ref, y_ref, dst_hbm, wgb_ref, wub_ref, wdb_ref, stage_ref, sem):
    del blk_i_ref
    i = pl.program_id(0)
    n_valid = blk_rows_ref[i]
    side = i < side_steps
    slot = i % 2
    _refresh_weights(i, blk_e_ref, wg_ref, wu_ref, wd_ref, wgb_ref, wub_ref, wdb_ref)

    def send():
        stage_ref[slot] = rows_ref[...]
        for tok in range(SIDE_ROWS):
            _start_rows_out(tok, stage_ref.at[slot], off_ref, dst_hbm, sem.at[slot], SIDE_QUEUES)

    @pl.when((n_valid > 0) & side)
    def _():
        send()
        _swiglu_block(n_valid, xs_ref, wgb_ref, wub_ref, wdb_ref, y_ref)

    @pl.when((n_valid > 0) & jnp.logical_not(side))
    def _():
        _swiglu_block(n_valid, xs_ref, wgb_ref, wub_ref, wdb_ref, y_ref)

    @pl.when((n_valid == 0) & side)
    def _():
        send()

    @pl.when((i >= 1) & (i <= side_steps))
    def _():
        _wait_rows_out(stage_ref.at[1 - slot], dst_hbm, sem.at[1 - slot])


def _experts_in_kernel(side_steps, blk_e_ref, blk_rows_ref, blk_i_ref, xs_ref, wg_ref, wu_ref, wd_ref,
                       off_ref, off_nx_ref, off_nx2_ref, base_ref, gates_ref, gfin_ref, src_hbm, y_ref, out_ref,
                       wgb_ref, wub_ref, wdb_ref, buf_ref, sem):
    del blk_i_ref
    i = pl.program_id(0)
    n_valid = blk_rows_ref[i]
    slot = i % 2
    _refresh_weights(i, blk_e_ref, wg_ref, wu_ref, wd_ref, wgb_ref, wub_ref, wdb_ref)

    def fetch(off_r, into):
        for tok in range(SIDE_ROWS):
            _start_rows_in(tok, SIDE_ROWS, src_hbm, off_r, buf_ref.at[into], sem.at[into], SIDE_QUEUES)

    @pl.when(i == 0)
    def _():
        fetch(off_ref, 0)
        fetch(off_nx_ref, 1)

    @pl.when(i < side_steps)
    def _():
        _wait_rows_in(src_hbm, buf_ref.at[slot], sem.at[slot])
        _weighted_sum(base_ref, gates_ref, gfin_ref, buf_ref.at[slot], out_ref)

    more = i + 2 < side_steps

    @pl.when((n_valid > 0) & more)
    def _():
        fetch(off_nx2_ref, slot)
        _swiglu_block(n_valid, xs_ref, wgb_ref, wub_ref, wdb_ref, y_ref)

    @pl.when((n_valid > 0) & jnp.logical_not(more))
    def _():
        _swiglu_block(n_valid, xs_ref, wgb_ref, wub_ref, wdb_ref, y_ref)

    @pl.when((n_valid == 0) & more)
    def _():
        fetch(off_nx2_ref, slot)


def _expert_specs(d, ff):
    r = EXPERT_ROWS
    rows = pl.BlockSpec((r * ROW_TILE, LANES), lambda i, be, br, bi: (bi[i], 0))
    weights = [pl.BlockSpec((None, d, ff), lambda i, be, br, bi: (be[i], 0, 0)),
               pl.BlockSpec((None, d, ff), lambda i, be, br, bi: (be[i], 0, 0)),
               pl.BlockSpec((None, ff, d), lambda i, be, br, bi: (be[i], 0, 0))]
    scratch = [pltpu.VMEM((d, ff), jnp.bfloat16), pltpu.VMEM((d, ff), jnp.bfloat16),
               pltpu.VMEM((ff, d), jnp.bfloat16)]
    return rows, weights, scratch


def _experts_out(blk, xs, wg, wu, wd, off, xn, first_block, n_rows_next):
    side_steps = off.shape[0]
    d, ff = wg.shape[1], wg.shape[2]
    rows, weights, scratch = _expert_specs(d, ff)
    side = lambda i, be, br, bi: jnp.minimum(i, side_steps - 1)
    return pl.pallas_call(
        functools.partial(_experts_out_kernel, side_steps),
        out_shape=(jax.ShapeDtypeStruct(xs.shape, xs.dtype),
                   jax.ShapeDtypeStruct((n_rows_next * ROW_TILE, LANES), xs.dtype)),
        grid_spec=pltpu.PrefetchScalarGridSpec(
            num_scalar_prefetch=3,
            grid=(xs.shape[0] // (EXPERT_ROWS * ROW_TILE),),
            in_specs=[rows] + weights + [
                _offsets_spec(SIDE_ROWS, lambda i, be, br, bi: (side(i, be, br, bi), 0, 0)),
                pl.BlockSpec((SIDE_ROWS * ROW_TILE, LANES), lambda i, be, br, bi: (first_block + side(i, be, br, bi), 0))],
            out_specs=(rows, pl.BlockSpec(memory_space=pl.ANY)),
            scratch_shapes=scratch + [pltpu.VMEM((2, SIDE_ROWS * ROW_TILE, LANES), xs.dtype),
                                      pltpu.SemaphoreType.DMA((2,))],
        ),
        compiler_params=pltpu.CompilerParams(
            dimension_semantics=("arbitrary",), vmem_limit_bytes=VMEM_LIMIT),
        name="experts_send",
    )(*blk, xs, wg, wu, wd, off, xn)


def _experts_in(blk, xs, wg, wu, wd, off, base, gates, gfin, y_prev, n):
    side_steps = off.shape[0]
    d, ff = wg.shape[1], wg.shape[2]
    rows, weights, scratch = _expert_specs(d, ff)
    side = lambda i, be, br, bi: jnp.minimum(i, side_steps - 1)
    nxt = lambda i, be, br, bi: jnp.minimum(i + 1, side_steps - 1)
    nxt2 = lambda i, be, br, bi: jnp.minimum(i + 2, side_steps - 1)
    return pl.pallas_call(
        functools.partial(_experts_in_kernel, side_steps),
        out_shape=(jax.ShapeDtypeStruct(xs.shape, xs.dtype), jax.ShapeDtypeStruct((n, d), jnp.float32)),
        grid_spec=pltpu.PrefetchScalarGridSpec(
            num_scalar_prefetch=3,
            grid=(xs.shape[0] // (EXPERT_ROWS * ROW_TILE),),
            in_specs=[rows] + weights + [
                _offsets_spec(SIDE_ROWS, lambda i, be, br, bi: (side(i, be, br, bi), 0, 0)),
                _offsets_spec(SIDE_ROWS, lambda i, be, br, bi: (nxt(i, be, br, bi), 0, 0)),
                _offsets_spec(SIDE_ROWS, lambda i, be, br, bi: (nxt2(i, be, br, bi), 0, 0)),
                pl.BlockSpec((SIDE_ROWS, d), lambda i, be, br, bi: (side(i, be, br, bi), 0)),
                pl.BlockSpec((SIDE_ROWS, TOP_K), lambda i, be, br, bi: (side(i, be, br, bi), 0)),
                pl.BlockSpec((1, d), lambda i, be, br, bi: (0, 0)),
                pl.BlockSpec(memory_space=pl.ANY)],
            out_specs=(rows, pl.BlockSpec((SIDE_ROWS, d), lambda i, be, br, bi: (side(i, be, br, bi), 0))),
            scratch_shapes=scratch + [pltpu.VMEM((2, TOP_K * SIDE_ROWS * ROW_TILE, LANES), xs.dtype),
                                      pltpu.SemaphoreType.DMA((2,))],
        ),
        compiler_params=pltpu.CompilerParams(
            dimension_semantics=("arbitrary",), vmem_limit_bytes=VMEM_LIMIT),
        name="experts_fetch",
    )(*blk, xs, wg, wu, wd, off, off, off, base, gates, gfin, y_prev)


def _combine_kernel(off_ref, off_nx_ref, base_ref, gates_ref, gfin_ref, y_hbm, out_in_ref, out_ref,
                    buf0_ref, buf1_ref, sem):
    del out_in_ref
    i = pl.program_id(0)
    n_steps = pl.num_programs(0)
    n_tok = base_ref.shape[0]
    bufs = (buf0_ref, buf1_ref)

    @pl.when(i == 0)
    def _():
        def body(g, carry):
            for u in range(ISSUE_UNROLL):
                _start_rows_in(g * ISSUE_UNROLL + u, n_tok, y_hbm, off_ref, buf0_ref, sem.at[0])
            return carry
        lax.fori_loop(0, n_tok // ISSUE_UNROLL, body, 0)

    for p in range(2):
        def step(fetch_next, p=p):
            _wait_rows_in(y_hbm, bufs[p], sem.at[p])
            if fetch_next:
                for tok in range(n_tok):
                    _start_rows_in(tok, n_tok, y_hbm, off_nx_ref, bufs[1 - p], sem.at[1 - p])
            _weighted_sum(base_ref, gates_ref, gfin_ref, bufs[p], out_ref)

        pl.when((i % 2 == p) & (i + 1 < n_steps))(functools.partial(step, True))
        pl.when((i % 2 == p) & (i + 1 >= n_steps))(functools.partial(step, False))


def _combine(off, base, gates, gfin, y, out_prev, first_block):
    d = base.shape[1]
    t = COMBINE_ROWS
    n_steps = off.shape[0]
    tok = lambda width, first=0: pl.BlockSpec((t, width), lambda i: (first + i, 0))
    return pl.pallas_call(
        _combine_kernel,
        out_shape=jax.ShapeDtypeStruct(out_prev.shape, jnp.float32),
        grid=(n_steps,),
        in_specs=[_offsets_spec(t, lambda i: (i, 0, 0)),
                  _offsets_spec(t, lambda i: (jnp.minimum(i + 1, n_steps - 1), 0, 0)),
                  tok(d), tok(TOP_K), pl.BlockSpec((1, d), lambda i: (0, 0)),
                  pl.BlockSpec(memory_space=pl.ANY), pl.BlockSpec(memory_space=pl.ANY)],
        out_specs=tok(d, first_block),
        scratch_shapes=[pltpu.VMEM((TOP_K * t * ROW_TILE, LANES), y.dtype),
                        pltpu.VMEM((TOP_K * t * ROW_TILE, LANES), y.dtype),
                        pltpu.SemaphoreType.DMA((2,))],
        input_output_aliases={6: 0},
        compiler_params=pltpu.CompilerParams(
            dimension_semantics=("arbitrary",), vmem_limit_bytes=VMEM_LIMIT),
        name="combine",
    )(off, off, base, gates, gfin, y, out_prev)


def _group_layout(counts, idx, rank):
    r = EXPERT_ROWS
    n_g = idx.shape[1]
    n_blocks = (n_g * TOP_K) // r + N_EXPERTS
    padded = (counts + r - 1) // r * r
    ends = jnp.cumsum(padded)
    starts = (ends - padded).astype(jnp.int32)
    blk_row0 = jnp.arange(n_blocks, dtype=jnp.int32) * r
    blk_e = jnp.minimum(jnp.sum((ends[None, :] <= blk_row0[:, None]).astype(jnp.int32), axis=1), N_EXPERTS - 1)
    onehot = (blk_e[:, None] == jnp.arange(N_EXPERTS, dtype=jnp.int32)[None, :]).astype(jnp.int32)
    blk_rows = jnp.clip(onehot @ counts - (blk_row0 - onehot @ starts), 0, r).astype(jnp.int32)
    dest = rank + jnp.sum(jnp.where(idx[None] == jnp.arange(N_EXPERTS, dtype=jnp.int32)[:, None, None],
                                    starts[:, None, None], 0), axis=0)
    blk_i = jnp.minimum(jnp.arange(n_blocks, dtype=jnp.int32), ends[-1].astype(jnp.int32) // r - 1)
    return (blk_e, blk_rows, blk_i), (dest * ROW_TILE).T.reshape(-1), n_blocks * r


def kernel(x, meta_tokens, norm_mix_g, w_in, lb_table, hgrn_norm_g, conv_w, conv_norm_g, w_out,
           norm_ffn_g, w_router, router_bias, w_gate_e, w_up_e, w_down_e, w_gate_s, w_up_s, w_down_s,
           norm_final_g):
    bsz, seq, d = x.shape
    n = bsz * seq
    assert TOKEN_GROUPS == 2 and bsz % TOKEN_GROUPS == 0
    b_g = bsz // TOKEN_GROUPS
    n_g = n // TOKEN_GROUPS
    bf = jnp.bfloat16
    assert seq % MIX_ROWS == 0 and MIX_ROWS % CHUNK == 0
    assert n_g % ROUTE_ROWS == 0 and n_g % COMBINE_ROWS == 0
    assert (n_g * TOP_K) % EXPERT_ROWS == 0 and n_g % SIDE_ROWS == 0
    assert 2 <= n_g // SIDE_ROWS < (n_g * TOP_K) // EXPERT_ROWS + N_EXPERTS

    meta_pad = jnp.zeros((CHUNK, d), jnp.float32).at[CHUNK - N_META:].set(meta_tokens)
    mix = functools.partial(_mixer, meta_pad=meta_pad, gmix=norm_mix_g[0:1], w_in=w_in[0].astype(bf),
                            lb_table=lb_table, ghg=hgrn_norm_g[0:1], conv_w=conv_w[0], gcv=conv_norm_g[0:1],
                            w_out=w_out[0].astype(bf))
    route = functools.partial(_router, gffn=norm_ffn_g[0:1], w_router=w_router[0].T, bias=router_bias[0][:, None],
                              wgs=w_gate_s[0].astype(bf), wus=w_up_s[0].astype(bf), wds=w_down_s[0].astype(bf))
    steps = lambda off, t: off.reshape(n_g // t, 1, t * TOP_K)
    gfin = norm_final_g[None, :]
    wg, wu, wd = w_gate_e[0], w_up_e[0], w_down_e[0]

    h1_0 = mix(x, 0, b_g)
    base0, xn0, idx0, gate0, rank0, counts0 = route(h1_0.reshape(n_g, d))
    blk0, off0, rows0 = _group_layout(counts0[:, 0], idx0, rank0)
    h1_1, xs0 = mix(x, b_g, b_g, send=(steps(off0, MIX_ROWS), xn0, rows0))
    base1, xn1, idx1, gate1, rank1, counts1 = route(h1_1.reshape(n_g, d))
    blk1, off1, rows1 = _group_layout(counts1[:, 0], idx1, rank1)
    y0, xs1 = _experts_out(blk0, xs0, wg, wu, wd, steps(off1, SIDE_ROWS), xn1, 0, rows1)
    y1, out = _experts_in(blk1, xs1, wg, wu, wd, steps(off0, SIDE_ROWS), base0, gate0.T, gfin, y0, n)
    out = _combine(steps(off1, COMBINE_ROWS), base1, gate1.T, gfin, y1, out, n_g // COMBINE_ROWS)
    return out.reshape(bsz, seq, d)
```

```python
import functools

import numpy as np
import jax
import jax.numpy as jnp
from jax import lax
from jax.experimental import pallas as pl
from jax.experimental.pallas import tpu as pltpu

N_META = 16
CHUNK = 128
HEADS = 4
HEAD_DIM = 128
HGRN_W = HEADS * HEAD_DIM
CONV_W = 512
CONV_GROUPS = 4
CONV_K = 3
N_EXPERTS = 64
TOP_K = 8
ROUTED_SCALE = 2.5
EPS = 1e-6

V7X_VMEM_BYTES = 64 * 1024 * 1024
VMEM_LIMIT = V7X_VMEM_BYTES - 8 * 1024 * 1024

MIX_ROWS = 512
CHUNK_UNROLL = 4
ROUTE_ROWS = 512
EXPERT_ROWS = 1024
COMBINE_ROWS = 256
DMA_QUEUES = 2
ISSUE_UNROLL = 8
TOKEN_GROUPS = 2
SIDE_ROWS = 128

LANES = 128
ROW_TILE = 4

HALF_SPANS = (64, 32, 16, 8, 4, 2, 1)
N_LEVELS = len(HALF_SPANS) + 1


def _decay_sum_matrix():
    a = np.zeros((N_LEVELS, CHUNK, CHUNK), np.float32)
    a[0] = np.tril(np.ones((CHUNK, CHUNK), np.float32))
    for i, m in enumerate(HALF_SPANS):
        for t in range(CHUNK):
            mid = (t // (2 * m)) * 2 * m + m
            if t >= mid:
                a[1 + i, t, mid:t + 1] = 1.0
            else:
                a[1 + i, t, t + 1:mid] = 1.0
    return a.reshape(N_LEVELS * CHUNK, CHUNK)


def _level_matrix():
    lv = np.full((CHUNK, CHUNK), -1, np.int32)
    for t in range(CHUNK):
        lv[t, t] = len(HALF_SPANS)
        for s in range(t):
            top = (t ^ s).bit_length() - 1
            lv[t, s] = HALF_SPANS.index(1 << top)
    return lv


def _rms(x, g):
    return x * lax.rsqrt(jnp.mean(x * x, axis=-1, keepdims=True) + EPS) * g


def _group_rms(x, g, width):
    outs = []
    for j in range(x.shape[-1] // width):
        xs = x[:, j * width:(j + 1) * width]
        outs.append(xs * lax.rsqrt(jnp.mean(xs * xs, axis=-1, keepdims=True) + EPS))
    return jnp.concatenate(outs, axis=-1) * g


def _sigmoid_pair(z):
    t = jnp.exp(-jnp.abs(z))
    inv = 1.0 / (1.0 + t)
    big, small = inv, t * inv
    pos = z >= 0
    return jnp.where(pos, big, small), jnp.where(pos, small, big)


def _pack_rows(x):
    half = x.shape[1] // 2
    bits = lambda v: lax.bitcast_convert_type(v.astype(jnp.bfloat16).astype(jnp.float32), jnp.uint32)
    return (bits(x[:, :half]) >> 16) | (bits(x[:, half:]) & jnp.uint32(0xFFFF0000))


def _unpack_words(w):
    lo = lax.bitcast_convert_type(w << 16, jnp.float32)
    hi = lax.bitcast_convert_type(w & jnp.uint32(0xFFFF0000), jnp.float32)
    return lo, hi


def _store_row_tiles(ref, words):
    t = words.shape[0]
    for c in range(ROW_TILE):
        ref[pl.ds(c, t, stride=ROW_TILE), :] = words[:, c * LANES:(c + 1) * LANES]


def _load_row_tiles(ref, t):
    parts = [_unpack_words(ref[pl.ds(c, t, stride=ROW_TILE), :]) for c in range(ROW_TILE)]
    return [p[0] for p in parts], [p[1] for p in parts]


def _dot(a, b):
    return jnp.dot(a, b, preferred_element_type=jnp.float32)


def _dot_nt(a, b):
    return lax.dot_general(a, b, (((1,), (1,)), ((), ())), preferred_element_type=jnp.float32)


def _dot_tn(a, b):
    return lax.dot_general(a, b, (((0,), (0,)), ((), ())), preferred_element_type=jnp.float32)


def _hgrn_chunk(q, z, iv, lb, amat, level, st_ref, first_valid_row):
    sig, sig_neg = _sigmoid_pair(z)
    lf = jnp.log(lb + (1.0 - lb) * sig)
    k = (1.0 - lb) * sig_neg
    row = lax.broadcasted_iota(jnp.int32, (CHUNK, HGRN_W), 0)
    if first_valid_row:
        valid = row >= first_valid_row
        lf = jnp.where(valid, lf, 0.0)
        k = jnp.where(valid, k, 0.0)

    h1 = lf.astype(jnp.bfloat16)
    h2 = (lf - h1.astype(jnp.float32)).astype(jnp.bfloat16)
    e_all = _dot(amat, jnp.concatenate([h1, h2], axis=0))

    b = e_all[0:CHUNK]
    b_last = b[CHUNK - 1:CHUNK]
    q_in = (q * jnp.exp(b)).astype(jnp.bfloat16)
    k_out = (k * jnp.exp(b_last - b)).astype(jnp.bfloat16)
    st_decay = jnp.exp(b_last)
    v_bf = iv.astype(jnp.bfloat16)

    q_lv = [q.astype(jnp.bfloat16)]
    k_lv = [k.astype(jnp.bfloat16)]
    for i, m in enumerate(HALF_SPANS):
        ex = jnp.exp(e_all[(1 + i) * CHUNK:(2 + i) * CHUNK])
        right = (row & m) != 0
        q_lv.append(jnp.where(right, q * ex, 0.0).astype(jnp.bfloat16))
        k_lv.append(jnp.where(right, 0.0, k * ex).astype(jnp.bfloat16))
    lv_of = [len(HALF_SPANS)] + list(range(len(HALF_SPANS)))

    outs = []
    for h in range(HEADS):
        cols = slice(h * HEAD_DIM, (h + 1) * HEAD_DIM)
        scores = jnp.zeros((CHUNK, CHUNK), jnp.float32)
        for ql, kl, lv in zip(q_lv, k_lv, lv_of):
            scores = jnp.where(level == lv, _dot_nt(ql[:, cols], kl[:, cols]), scores)
        st = st_ref[h]
        o = _dot(scores.astype(jnp.bfloat16), v_bf[:, cols]) + _dot_nt(q_in[:, cols], st.astype(jnp.bfloat16))
        st_ref[h] = st * st_decay[:, cols] + _dot_tn(v_bf[:, cols], k_out[:, cols])
        outs.append(o)
    return jnp.concatenate(outs, axis=-1)


def _mixer_kernel(with_send, x_ref, meta_ref, gmix_ref, win_ref, lbt_ref, ghg_ref, cw_ref, gcv_ref, wout_ref,
                  amat_ref, level_ref, *rest):
    if with_send:
        off_ref, rows_ref, h1_ref, dst_hbm, proj_ref, o_ref, u_ref, st_ref, sem = rest
    else:
        h1_ref, proj_ref, o_ref, u_ref, st_ref = rest
    j = pl.program_id(1)
    rows = x_ref.shape[0]
    n_in = win_ref.shape[1]

    lbt = lbt_ref[...]
    lbe = jnp.exp(lbt - jnp.max(lbt, axis=0, keepdims=True))
    lb = lbe[0:1] / jnp.sum(lbe, axis=0, keepdims=True)

    amat = amat_ref[...]
    level = level_ref[...]
    gmix = gmix_ref[...]

    def project(xv, dst_rows):
        xn = _rms(xv, gmix).astype(jnp.bfloat16)
        for c0 in range(0, n_in, 512):
            proj_ref[dst_rows, c0:c0 + 512] = _dot(xn, win_ref[:, c0:c0 + 512])

    @pl.when(j == 0)
    def _():
        st_ref[...] = jnp.zeros_like(st_ref)
        project(meta_ref[...], pl.ds(0, CHUNK))
        pm = proj_ref[0:CHUNK, :]
        _hgrn_chunk(pm[:, 0:512], pm[:, 512:1024], pm[:, 1024:1536], lb, amat, level, st_ref,
                    CHUNK - N_META)
        u_ref[0:8, :] = (pm[:, 2560:3072] * pm[:, 3072:3584])[CHUNK - 8:CHUNK]

    if with_send:
        for tok in range(rows_ref.shape[0] // ROW_TILE):
            _start_rows_out(tok, rows_ref, off_ref, dst_hbm, sem)

    project(x_ref[...], pl.ds(0, rows))

    def chunk_body(c, carry):
        for u in range(CHUNK_UNROLL):
            r0 = pl.multiple_of((c * CHUNK_UNROLL + u) * CHUNK, CHUNK)
            q = proj_ref[pl.ds(r0, CHUNK), 0:512]
            z = proj_ref[pl.ds(r0, CHUNK), 512:1024]
            iv = proj_ref[pl.ds(r0, CHUNK), 1024:1536]
            o_ref[pl.ds(r0, CHUNK), :] = _hgrn_chunk(q, z, iv, lb, amat, level, st_ref, 0)
        return carry

    lax.fori_loop(0, rows // (CHUNK * CHUNK_UNROLL), chunk_body, 0)

    g_out = proj_ref[:, 1536:2048]
    g_sig, _ = _sigmoid_pair(g_out)
    y_hgrn = _group_rms(o_ref[...], ghg_ref[...], HEAD_DIM) * (g_out * g_sig)

    u = proj_ref[:, 2560:3072] * proj_ref[:, 3072:3584]
    u_ref[8:8 + rows, :] = u
    cw = cw_ref[...]
    y = cw[2:3] * u + cw[1:2] * u_ref[7:7 + rows, :] + cw[0:1] * u_ref[6:6 + rows, :]
    u_ref[0:8, :] = u[rows - 8:rows]
    y_conv = _group_rms(proj_ref[:, 2048:2560] * y, gcv_ref[...], CONV_W // CONV_GROUPS)

    mixed = jnp.concatenate([y_hgrn, y_conv], axis=-1).astype(jnp.bfloat16)
    h1_ref[...] = x_ref[...] + _dot(mixed, wout_ref[...])

    if with_send:
        _wait_rows_out(rows_ref, dst_hbm, sem)


def _mixer(x, batch0, n_batch, meta_pad, gmix, w_in, lb_table, ghg, conv_w, gcv, w_out, send=None):
    _, seq, d = x.shape
    n_in = w_in.shape[1]
    rows = MIX_ROWS
    n_j = seq // rows
    const = lambda *shape: pl.BlockSpec(shape, lambda b, j: (0,) * len(shape))
    in_specs = [
        pl.BlockSpec((None, rows, d), lambda b, j: (batch0 + b, j, 0)),
        const(CHUNK, d), const(1, d), const(d, n_in), const(*lb_table.shape), const(1, HGRN_W),
        const(CONV_K, CONV_W), const(1, CONV_W), const(d, d),
        const(N_LEVELS * CHUNK, 2 * CHUNK), const(CHUNK, CHUNK),
    ]
    args = [x, meta_pad, gmix, w_in, lb_table, ghg, conv_w, gcv, w_out,
            jnp.asarray(np.tile(_decay_sum_matrix(), (1, 2)), jnp.bfloat16), jnp.asarray(_level_matrix())]
    out_shape = jax.ShapeDtypeStruct((n_batch, seq, d), jnp.float32)
    out_specs = pl.BlockSpec((None, rows, d), lambda b, j: (b, j, 0))
    scratch = [
        pltpu.VMEM((rows, n_in), jnp.float32),
        pltpu.VMEM((rows, HGRN_W), jnp.float32),
        pltpu.VMEM((rows + 8, CONV_W), jnp.float32),
        pltpu.VMEM((HEADS, HEAD_DIM, HEAD_DIM), jnp.float32),
    ]
    if send is not None:
        off, xn, n_rows = send
        assert off.shape[0] == n_batch * n_j and off.shape[2] == rows * TOP_K
        in_specs += [_offsets_spec(rows, lambda b, j: (b * n_j + j, 0, 0)),
                     pl.BlockSpec((rows * ROW_TILE, LANES), lambda b, j: (b * n_j + j, 0))]
        args += [off, xn]
        out_shape = (out_shape, jax.ShapeDtypeStruct((n_rows * ROW_TILE, LANES), xn.dtype))
        out_specs = (out_specs, pl.BlockSpec(memory_space=pl.ANY))
        scratch += [pltpu.SemaphoreType.DMA(())]
    return pl.pallas_call(
        functools.partial(_mixer_kernel, send is not None),
        out_shape=out_shape,
        grid=(n_batch, n_j),
        in_specs=in_specs,
        out_specs=out_specs,
        scratch_shapes=scratch,
        compiler_params=pltpu.CompilerParams(
            dimension_semantics=("arbitrary", "arbitrary"), vmem_limit_bytes=VMEM_LIMIT),
        name="mixer_send" if send is not None else "mixer",
    )(*args)


def _router_kernel(h1_ref, gffn_ref, wr_ref, bias_ref, wgs_ref, wus_ref, wds_ref, tri_ref,
                   base_ref, xn_ref, idx_ref, gate_ref, rank_ref, counts_ref, carry_ref):
    i = pl.program_id(0)
    n_tok = h1_ref.shape[0]

    @pl.when(i == 0)
    def _():
        carry_ref[...] = jnp.zeros_like(carry_ref)

    h1 = h1_ref[...]
    xn = _rms(h1, gffn_ref[...])
    _store_row_tiles(xn_ref, _pack_rows(xn))
    xb = xn.astype(jnp.bfloat16)

    g_pre = _dot(xb, wgs_ref[...])
    gate_s, _ = _sigmoid_pair(g_pre)
    hid = (g_pre * gate_s) * _dot(xb, wus_ref[...])
    base_ref[...] = h1 + _dot(hid.astype(jnp.bfloat16), wds_ref[...])

    x_lo = (xn - xb.astype(jnp.float32)).astype(jnp.bfloat16)
    wr = wr_ref[...]
    w_hi = wr.astype(jnp.bfloat16)
    w_lo = (wr - w_hi.astype(jnp.float32)).astype(jnp.bfloat16)
    logits = _dot_nt(w_hi, xb) + _dot_nt(w_hi, x_lo) + _dot_nt(w_lo, xb)
    scores, _ = _sigmoid_pair(logits)
    sel = scores + bias_ref[...]
    eid = lax.broadcasted_iota(jnp.int32, (N_EXPERTS, n_tok), 0).astype(jnp.float32)
    picks, top_s = [], []
    for _ in range(TOP_K):
        best = jnp.max(sel, axis=0, keepdims=True)
        pick = jnp.min(jnp.where(sel == best, eid, float(N_EXPERTS)), axis=0, keepdims=True)
        hit = eid == pick
        top_s.append(jnp.sum(jnp.where(hit, scores, 0.0), axis=0, keepdims=True))
        sel = jnp.where(hit, -jnp.inf, sel)
        picks.append(pick)
    top_s = jnp.concatenate(top_s, axis=0)
    gate_ref[...] = top_s / jnp.sum(top_s, axis=0, keepdims=True) * ROUTED_SCALE
    idx_ref[...] = jnp.concatenate(picks, axis=0).astype(jnp.int32)

    chosen = jnp.zeros((N_EXPERTS, n_tok), jnp.float32)
    for pick in picks:
        chosen = chosen + jnp.where(eid == pick, 1.0, 0.0)
    incl = _dot(chosen.astype(jnp.bfloat16), tri_ref[...])
    before = carry_ref[...] + incl - 1.0
    ranks = [jnp.sum(jnp.where(eid == pick, before, 0.0), axis=0, keepdims=True) for pick in picks]
    rank_ref[...] = jnp.concatenate(ranks, axis=0).astype(jnp.int32)
    carry_ref[...] = carry_ref[...] + incl[:, n_tok - 1:n_tok]
    counts_ref[...] = jnp.broadcast_to(carry_ref[...], counts_ref.shape).astype(jnp.int32)


def _router(h1, gffn, w_router, bias, wgs, wus, wds):
    n, d = h1.shape
    t = ROUTE_ROWS
    ff = wgs.shape[1]
    tri = jnp.asarray(np.triu(np.ones((t, t), np.float32)), jnp.bfloat16)
    const = lambda *shape: pl.BlockSpec(shape, lambda i: (0,) * len(shape))
    tok = lambda width: pl.BlockSpec((t, width), lambda i: (i, 0))
    slot = pl.BlockSpec((TOP_K, t), lambda i: (0, i))
    return pl.pallas_call(
        _router_kernel,
        out_shape=(
            jax.ShapeDtypeStruct((n, d), jnp.float32),
            jax.ShapeDtypeStruct((n * ROW_TILE, LANES), jnp.uint32),
            jax.ShapeDtypeStruct((TOP_K, n), jnp.int32),
            jax.ShapeDtypeStruct((TOP_K, n), jnp.float32),
            jax.ShapeDtypeStruct((TOP_K, n), jnp.int32),
            jax.ShapeDtypeStruct((N_EXPERTS, 128), jnp.int32),
        ),
        grid=(n // t,),
        in_specs=[tok(d), const(1, d), const(N_EXPERTS, d), const(N_EXPERTS, 1),
                  const(d, ff), const(d, ff), const(ff, d), const(t, t)],
        out_specs=(tok(d), pl.BlockSpec((t * ROW_TILE, LANES), lambda i: (i, 0)), slot, slot, slot,
                   const(N_EXPERTS, 128)),
        scratch_shapes=[pltpu.VMEM((N_EXPERTS, 1), jnp.float32)],
        compiler_params=pltpu.CompilerParams(
            dimension_semantics=("arbitrary",), vmem_limit_bytes=VMEM_LIMIT),
        name="router",
    )(h1, gffn, w_router, bias, wgs, wus, wds, tri)


def _slab(ref, first_sublane):
    if not isinstance(first_sublane, int):
        first_sublane = pl.multiple_of(first_sublane, ROW_TILE)
    return ref.at[pl.ds(first_sublane, ROW_TILE)]


def _start_rows_out(tok, rows_ref, off_ref, dst_hbm, sem):
    for s in range(TOP_K):
        pltpu.make_async_copy(_slab(rows_ref, tok * ROW_TILE), _slab(dst_hbm, off_ref[0, tok * TOP_K + s]),
                              sem).start(priority=s % DMA_QUEUES)


def _wait_rows_out(rows_ref, dst_hbm, sem):
    for s in range(TOP_K):
        pltpu.make_async_copy(rows_ref, dst_hbm.at[pl.ds(0, rows_ref.shape[0])], sem).wait()


def _start_rows_in(tok, n_tok, src_hbm, off_ref, buf_ref, sem):
    for s in range(TOP_K):
        pltpu.make_async_copy(_slab(src_hbm, off_ref[0, tok * TOP_K + s]),
                              _slab(buf_ref, (s * n_tok + tok) * ROW_TILE), sem).start(priority=s % DMA_QUEUES)


def _wait_rows_in(src_hbm, buf_ref, sem):
    pltpu.make_async_copy(src_hbm.at[pl.ds(0, buf_ref.shape[0])], buf_ref, sem).wait()


def _weighted_sum(base_ref, gates_ref, gfin_ref, buf_ref, out_ref):
    n_tok = base_ref.shape[0]
    gates = gates_ref[...]
    width = ROW_TILE * LANES
    lo_chunks, hi_chunks = [], []
    for c in range(ROW_TILE):
        acc_lo = base_ref[:, c * LANES:(c + 1) * LANES]
        acc_hi = base_ref[:, width + c * LANES:width + (c + 1) * LANES]
        for s in range(TOP_K):
            lo, hi = _unpack_words(buf_ref[pl.ds(s * n_tok * ROW_TILE + c, n_tok, stride=ROW_TILE), :])
            acc_lo = acc_lo + gates[:, s:s + 1] * lo
            acc_hi = acc_hi + gates[:, s:s + 1] * hi
        lo_chunks.append(acc_lo)
        hi_chunks.append(acc_hi)
    out_ref[...] = _rms(jnp.concatenate(lo_chunks + hi_chunks, axis=-1), gfin_ref[...])


def _offsets_spec(n_tok, index_map):
    return pl.BlockSpec((None, 1, n_tok * TOP_K), index_map, memory_space=pltpu.SMEM)


def _swiglu_block(n_valid, xs_ref, wgb_ref, wub_ref, wdb_ref, y_ref):
    r = EXPERT_ROWS
    keep = lax.broadcasted_iota(jnp.int32, (r, LANES), 0) < n_valid
    lo, hi = _load_row_tiles(xs_ref, r)
    xb = jnp.concatenate([jnp.where(keep, c, 0.0).astype(jnp.bfloat16) for c in lo + hi], axis=-1)
    g = _dot(xb, wgb_ref[...])
    u = _dot(xb, wub_ref[...])
    g_sig, _ = _sigmoid_pair(g)
    hid = ((g * g_sig) * u).astype(jnp.bfloat16)
    _store_row_tiles(y_ref, _pack_rows(_dot(hid, wdb_ref[...])))


def _refresh_weights(i, blk_e_ref, wg_ref, wu_ref, wd_ref, wgb_ref, wub_ref, wdb_ref):
    @pl.when((i == 0) | (blk_e_ref[i] != blk_e_ref[jnp.maximum(i - 1, 0)]))
    def _():
        wgb_ref[...] = wg_ref[...].astype(jnp.bfloat16)
        wub_ref[...] = wu_ref[...].astype(jnp.bfloat16)
        wdb_ref[...] = wd_ref[...].astype(jnp.bfloat16)


def _experts_out_kernel(side_steps, blk_e_ref, blk_rows_ref, blk_i_ref, xs_ref, wg_ref, wu_ref, wd_ref, off_ref,
                        rows_ref, y_ref, dst_hbm, wgb_ref, wub_ref, wdb_ref, stage_ref, sem):
    del blk_i_ref
    i = pl.program_id(0)
    n_valid = blk_rows_ref[i]
    side = i < side_steps
    slot = i % 2
    _refresh_weights(i, blk_e_ref, wg_ref, wu_ref, wd_ref, wgb_ref, wub_ref, wdb_ref)

    def send():
        stage_ref[slot] = rows_ref[...]
        for tok in range(SIDE_ROWS):
            _start_rows_out(tok, stage_ref.at[slot], off_ref, dst_hbm, sem.at[slot])

    @pl.when((n_valid > 0) & side)
    def _():
        send()
        _swiglu_block(n_valid, xs_ref, wgb_ref, wub_ref, wdb_ref, y_ref)

    @pl.when((n_valid > 0) & jnp.logical_not(side))
    def _():
        _swiglu_block(n_valid, xs_ref, wgb_ref, wub_ref, wdb_ref, y_ref)

    @pl.when((n_valid == 0) & side)
    def _():
        send()

    @pl.when((i >= 1) & (i <= side_steps))
    def _():
        _wait_rows_out(stage_ref.at[1 - slot], dst_hbm, sem.at[1 - slot])


def _experts_in_kernel(side_steps, blk_e_ref, blk_rows_ref, blk_i_ref, xs_ref, wg_ref, wu_ref, wd_ref,
                       off_ref, off_nx_ref, off_nx2_ref, base_ref, gates_ref, gfin_ref, src_hbm, y_ref, out_ref,
                       wgb_ref, wub_ref, wdb_ref, buf_ref, sem):
    del blk_i_ref
    i = pl.program_id(0)
    n_valid = blk_rows_ref[i]
    slot = i % 2
    _refresh_weights(i, blk_e_ref, wg_ref, wu_ref, wd_ref, wgb_ref, wub_ref, wdb_ref)

    def fetch(off_r, into):
        for tok in range(SIDE_ROWS):
            _start_rows_in(tok, SIDE_ROWS, src_hbm, off_r, buf_ref.at[into], sem.at[into])

    @pl.when(i == 0)
    def _():
        fetch(off_ref, 0)
        fetch(off_nx_ref, 1)

    @pl.when(i < side_steps)
    def _():
        _wait_rows_in(src_hbm, buf_ref.at[slot], sem.at[slot])
        _weighted_sum(base_ref, gates_ref, gfin_ref, buf_ref.at[slot], out_ref)

    more = i + 2 < side_steps

    @pl.when((n_valid > 0) & more)
    def _():
        fetch(off_nx2_ref, slot)
        _swiglu_block(n_valid, xs_ref, wgb_ref, wub_ref, wdb_ref, y_ref)

    @pl.when((n_valid > 0) & jnp.logical_not(more))
    def _():
        _swiglu_block(n_valid, xs_ref, wgb_ref, wub_ref, wdb_ref, y_ref)

    @pl.when((n_valid == 0) & more)
    def _():
        fetch(off_nx2_ref, slot)


def _expert_specs(d, ff):
    r = EXPERT_ROWS
    rows = pl.BlockSpec((r * ROW_TILE, LANES), lambda i, be, br, bi: (bi[i], 0))
    weights = [pl.BlockSpec((None, d, ff), lambda i, be, br, bi: (be[i], 0, 0)),
               pl.BlockSpec((None, d, ff), lambda i, be, br, bi: (be[i], 0, 0)),
               pl.BlockSpec((None, ff, d), lambda i, be, br, bi: (be[i], 0, 0))]
    scratch = [pltpu.VMEM((d, ff), jnp.bfloat16), pltpu.VMEM((d, ff), jnp.bfloat16),
               pltpu.VMEM((ff, d), jnp.bfloat16)]
    return rows, weights, scratch


def _experts_out(blk, xs, wg, wu, wd, off, xn, first_block, n_rows_next):
    side_steps = off.shape[0]
    d, ff = wg.shape[1], wg.shape[2]
    rows, weights, scratch = _expert_specs(d, ff)
    side = lambda i, be, br, bi: jnp.minimum(i, side_steps - 1)
    return pl.pallas_call(
        functools.partial(_experts_out_kernel, side_steps),
        out_shape=(jax.ShapeDtypeStruct(xs.shape, xs.dtype),
                   jax.ShapeDtypeStruct((n_rows_next * ROW_TILE, LANES), xs.dtype)),
        grid_spec=pltpu.PrefetchScalarGridSpec(
            num_scalar_prefetch=3,
            grid=(xs.shape[0] // (EXPERT_ROWS * ROW_TILE),),
            in_specs=[rows] + weights + [
                _offsets_spec(SIDE_ROWS, lambda i, be, br, bi: (side(i, be, br, bi), 0, 0)),
                pl.BlockSpec((SIDE_ROWS * ROW_TILE, LANES), lambda i, be, br, bi: (first_block + side(i, be, br, bi), 0))],
            out_specs=(rows, pl.BlockSpec(memory_space=pl.ANY)),
            scratch_shapes=scratch + [pltpu.VMEM((2, SIDE_ROWS * ROW_TILE, LANES), xs.dtype),
                                      pltpu.SemaphoreType.DMA((2,))],
        ),
        compiler_params=pltpu.CompilerParams(
            dimension_semantics=("arbitrary",), vmem_limit_bytes=VMEM_LIMIT),
        name="experts_send",
    )(*blk, xs, wg, wu, wd, off, xn)


def _experts_in(blk, xs, wg, wu, wd, off, base, gates, gfin, y_prev, n):
    side_steps = off.shape[0]
    d, ff = wg.shape[1], wg.shape[2]
    rows, weights, scratch = _expert_specs(d, ff)
    side = lambda i, be, br, bi: jnp.minimum(i, side_steps - 1)
    nxt = lambda i, be, br, bi: jnp.minimum(i + 1, side_steps - 1)
    nxt2 = lambda i, be, br, bi: jnp.minimum(i + 2, side_steps - 1)
    return pl.pallas_call(
        functools.partial(_experts_in_kernel, side_steps),
        out_shape=(jax.ShapeDtypeStruct(xs.shape, xs.dtype), jax.ShapeDtypeStruct((n, d), jnp.float32)),
        grid_spec=pltpu.PrefetchScalarGridSpec(
            num_scalar_prefetch=3,
            grid=(xs.shape[0] // (EXPERT_ROWS * ROW_TILE),),
            in_specs=[rows] + weights + [
                _offsets_spec(SIDE_ROWS, lambda i, be, br, bi: (side(i, be, br, bi), 0, 0)),
                _offsets_spec(SIDE_ROWS, lambda i, be, br, bi: (nxt(i, be, br, bi), 0, 0)),
                _offsets_spec(SIDE_ROWS, lambda i, be, br, bi: (nxt2(i, be, br, bi), 0, 0)),
                pl.BlockSpec((SIDE_ROWS, d), lambda i, be, br, bi: (side(i, be, br, bi), 0)),
                pl.BlockSpec((SIDE_ROWS, TOP_K), lambda i, be, br, bi: (side(i, be, br, bi), 0)),
                pl.BlockSpec((1, d), lambda i, be, br, bi: (0, 0)),
                pl.BlockSpec(memory_space=pl.ANY)],
            out_specs=(rows, pl.BlockSpec((SIDE_ROWS, d), lambda i, be, br, bi: (side(i, be, br, bi), 0))),
            scratch_shapes=scratch + [pltpu.VMEM((2, TOP_K * SIDE_ROWS * ROW_TILE, LANES), xs.dtype),
                                      pltpu.SemaphoreType.DMA((2,))],
        ),
        compiler_params=pltpu.CompilerParams(
            dimension_semantics=("arbitrary",), vmem_limit_bytes=VMEM_LIMIT),
        name="experts_fetch",
    )(*blk, xs, wg, wu, wd, off, off, off, base, gates, gfin, y_prev)


def _combine_kernel(off_ref, off_nx_ref, base_ref, gates_ref, gfin_ref, y_hbm, out_in_ref, out_ref,
                    buf0_ref, buf1_ref, sem):
    del out_in_ref
    i = pl.program_id(0)
    n_steps = pl.num_programs(0)
    n_tok = base_ref.shape[0]
    bufs = (buf0_ref, buf1_ref)

    @pl.when(i == 0)
    def _():
        def body(g, carry):
            for u in range(ISSUE_UNROLL):
                _start_rows_in(g * ISSUE_UNROLL + u, n_tok, y_hbm, off_ref, buf0_ref, sem.at[0])
            return carry
        lax.fori_loop(0, n_tok // ISSUE_UNROLL, body, 0)

    for p in range(2):
        def step(fetch_next, p=p):
            _wait_rows_in(y_hbm, bufs[p], sem.at[p])
            if fetch_next:
                for tok in range(n_tok):
                    _start_rows_in(tok, n_tok, y_hbm, off_nx_ref, bufs[1 - p], sem.at[1 - p])
            _weighted_sum(base_ref, gates_ref, gfin_ref, bufs[p], out_ref)

        pl.when((i % 2 == p) & (i + 1 < n_steps))(functools.partial(step, True))
        pl.when((i % 2 == p) & (i + 1 >= n_steps))(functools.partial(step, False))


def _combine(off, base, gates, gfin, y, out_prev, first_block):
    d = base.shape[1]
    t = COMBINE_ROWS
    n_steps = off.shape[0]
    tok = lambda width, first=0: pl.BlockSpec((t, width), lambda i: (first + i, 0))
    return pl.pallas_call(
        _combine_kernel,
        out_shape=jax.ShapeDtypeStruct(out_prev.shape, jnp.float32),
        grid=(n_steps,),
        in_specs=[_offsets_spec(t, lambda i: (i, 0, 0)),
                  _offsets_spec(t, lambda i: (jnp.minimum(i + 1, n_steps - 1), 0, 0)),
                  tok(d), tok(TOP_K), pl.BlockSpec((1, d), lambda i: (0, 0)),
                  pl.BlockSpec(memory_space=pl.ANY), pl.BlockSpec(memory_space=pl.ANY)],
        out_specs=tok(d, first_block),
        scratch_shapes=[pltpu.VMEM((TOP_K * t * ROW_TILE, LANES), y.dtype),
                        pltpu.VMEM((TOP_K * t * ROW_TILE, LANES), y.dtype),
                        pltpu.SemaphoreType.DMA((2,))],
        input_output_aliases={6: 0},
        compiler_params=pltpu.CompilerParams(
            dimension_semantics=("arbitrary",), vmem_limit_bytes=VMEM_LIMIT),
        name="combine",
    )(off, off, base, gates, gfin, y, out_prev)


def _group_layout(counts, idx, rank):
    r = EXPERT_ROWS
    n_g = idx.shape[1]
    n_blocks = (n_g * TOP_K) // r + N_EXPERTS
    padded = (counts + r - 1) // r * r
    ends = jnp.cumsum(padded)
    starts = (ends - padded).astype(jnp.int32)
    blk_row0 = jnp.arange(n_blocks, dtype=jnp.int32) * r
    blk_e = jnp.minimum(jnp.sum((ends[None, :] <= blk_row0[:, None]).astype(jnp.int32), axis=1), N_EXPERTS - 1)
    onehot = (blk_e[:, None] == jnp.arange(N_EXPERTS, dtype=jnp.int32)[None, :]).astype(jnp.int32)
    blk_rows = jnp.clip(onehot @ counts - (blk_row0 - onehot @ starts), 0, r).astype(jnp.int32)
    dest = rank + jnp.sum(jnp.where(idx[None] == jnp.arange(N_EXPERTS, dtype=jnp.int32)[:, None, None],
                                    starts[:, None, None], 0), axis=0)
    blk_i = jnp.minimum(jnp.arange(n_blocks, dtype=jnp.int32), ends[-1].astype(jnp.int32) // r - 1)
    return (blk_e, blk_rows, blk_i), (dest * ROW_TILE).T.reshape(-1), n_blocks * r


def kernel(x, meta_tokens, norm_mix_g, w_in, lb_table, hgrn_norm_g, conv_w, conv_norm_g, w_out,
           norm_ffn_g, w_router, router_bias, w_gate_e, w_up_e, w_down_e, w_gate_s, w_up_s, w_down_s,
           norm_final_g):
    bsz, seq, d = x.shape
    n = bsz * seq
    assert TOKEN_GROUPS == 2 and bsz % TOKEN_GROUPS == 0
    b_g = bsz // TOKEN_GROUPS
    n_g = n // TOKEN_GROUPS
    bf = jnp.bfloat16
    assert seq % MIX_ROWS == 0 and MIX_ROWS % CHUNK == 0
    assert n_g % ROUTE_ROWS == 0 and n_g % COMBINE_ROWS == 0
    assert (n_g * TOP_K) % EXPERT_ROWS == 0 and n_g % SIDE_ROWS == 0
    assert 2 <= n_g // SIDE_ROWS < (n_g * TOP_K) // EXPERT_ROWS + N_EXPERTS

    meta_pad = jnp.zeros((CHUNK, d), jnp.float32).at[CHUNK - N_META:].set(meta_tokens)
    mix = functools.partial(_mixer, meta_pad=meta_pad, gmix=norm_mix_g[0:1], w_in=w_in[0].astype(bf),
                            lb_table=lb_table, ghg=hgrn_norm_g[0:1], conv_w=conv_w[0], gcv=conv_norm_g[0:1],
                            w_out=w_out[0].astype(bf))
    route = functools.partial(_router, gffn=norm_ffn_g[0:1], w_router=w_router[0].T, bias=router_bias[0][:, None],
                              wgs=w_gate_s[0].astype(bf), wus=w_up_s[0].astype(bf), wds=w_down_s[0].astype(bf))
    steps = lambda off, t: off.reshape(n_g // t, 1, t * TOP_K)
    gfin = norm_final_g[None, :]
    wg, wu, wd = w_gate_e[0], w_up_e[0], w_down_e[0]

    h1_0 = mix(x, 0, b_g)
    base0, xn0, idx0, gate0, rank0, counts0 = route(h1_0.reshape(n_g, d))
    blk0, off0, rows0 = _group_layout(counts0[:, 0], idx0, rank0)
    h1_1, xs0 = mix(x, b_g, b_g, send=(steps(off0, MIX_ROWS), xn0, rows0))
    base1, xn1, idx1, gate1, rank1, counts1 = route(h1_1.reshape(n_g, d))
    blk1, off1, rows1 = _group_layout(counts1[:, 0], idx1, rank1)
    y0, xs1 = _experts_out(blk0, xs0, wg, wu, wd, steps(off1, SIDE_ROWS), xn1, 0, rows1)
    y1, out = _experts_in(blk1, xs1, wg, wu, wd, steps(off0, SIDE_ROWS), base0, gate0.T, gfin, y0, n)
    out = _combine(steps(off1, COMBINE_ROWS), base1, gate1.T, gfin, y1, out, n_g // COMBINE_ROWS)
    return out.reshape(bsz, seq, d)
```

```python
import functools

import numpy as np
import jax
import jax.numpy as jnp
from jax import lax
from jax.experimental import pallas as pl
from jax.experimental.pallas import tpu as pltpu

N_META = 16
CHUNK = 128
HEADS = 4
HEAD_DIM = 128
HGRN_W = HEADS * HEAD_DIM
CONV_W = 512
CONV_GROUPS = 4
CONV_K = 3
N_EXPERTS = 64
TOP_K = 8
ROUTED_SCALE = 2.5
EPS = 1e-6

V7X_VMEM_BYTES = 64 * 1024 * 1024
VMEM_LIMIT = V7X_VMEM_BYTES - 8 * 1024 * 1024

MIX_ROWS = 512
CHUNK_UNROLL = 4
ROUTE_ROWS = 512
EXPERT_ROWS = 1024
COMBINE_ROWS = 256
DMA_QUEUES = 2
ISSUE_UNROLL = 8
TOKEN_GROUPS = 2
SIDE_ROWS = 128

LANES = 128
ROW_TILE = 4

HALF_SPANS = (64, 32, 16, 8, 4, 2, 1)
N_LEVELS = len(HALF_SPANS) + 1


def _decay_sum_matrix():
    a = np.zeros((N_LEVELS, CHUNK, CHUNK), np.float32)
    a[0] = np.tril(np.ones((CHUNK, CHUNK), np.float32))
    for i, m in enumerate(HALF_SPANS):
        for t in range(CHUNK):
            mid = (t // (2 * m)) * 2 * m + m
            if t >= mid:
                a[1 + i, t, mid:t + 1] = 1.0
            else:
                a[1 + i, t, t + 1:mid] = 1.0
    return a.reshape(N_LEVELS * CHUNK, CHUNK)


def _level_matrix():
    lv = np.full((CHUNK, CHUNK), -1, np.int32)
    for t in range(CHUNK):
        lv[t, t] = len(HALF_SPANS)
        for s in range(t):
            top = (t ^ s).bit_length() - 1
            lv[t, s] = HALF_SPANS.index(1 << top)
    return lv


def _rms(x, g):
    return x * lax.rsqrt(jnp.mean(x * x, axis=-1, keepdims=True) + EPS) * g


def _group_rms(x, g, width):
    outs = []
    for j in range(x.shape[-1] // width):
        xs = x[:, j * width:(j + 1) * width]
        outs.append(xs * lax.rsqrt(jnp.mean(xs * xs, axis=-1, keepdims=True) + EPS))
    return jnp.concatenate(outs, axis=-1) * g


def _sigmoid_pair(z):
    t = jnp.exp(-jnp.abs(z))
    inv = 1.0 / (1.0 + t)
    big, small = inv, t * inv
    pos = z >= 0
    return jnp.where(pos, big, small), jnp.where(pos, small, big)


def _pack_rows(x):
    half = x.shape[1] // 2
    bits = lambda v: lax.bitcast_convert_type(v.astype(jnp.bfloat16).astype(jnp.float32), jnp.uint32)
    return (bits(x[:, :half]) >> 16) | (bits(x[:, half:]) & jnp.uint32(0xFFFF0000))


def _unpack_words(w):
    lo = lax.bitcast_convert_type(w << 16, jnp.float32)
    hi = lax.bitcast_convert_type(w & jnp.uint32(0xFFFF0000), jnp.float32)
    return lo, hi


def _store_row_tiles(ref, words):
    t = words.shape[0]
    for c in range(ROW_TILE):
        ref[pl.ds(c, t, stride=ROW_TILE), :] = words[:, c * LANES:(c + 1) * LANES]


def _load_row_tiles(ref, t):
    parts = [_unpack_words(ref[pl.ds(c, t, stride=ROW_TILE), :]) for c in range(ROW_TILE)]
    return [p[0] for p in parts], [p[1] for p in parts]


def _dot(a, b):
    return jnp.dot(a, b, preferred_element_type=jnp.float32)


def _dot_nt(a, b):
    return lax.dot_general(a, b, (((1,), (1,)), ((), ())), preferred_element_type=jnp.float32)


def _dot_tn(a, b):
    return lax.dot_general(a, b, (((0,), (0,)), ((), ())), preferred_element_type=jnp.float32)


def _hgrn_chunk(q, z, iv, lb, amat, level, st_ref, first_valid_row):
    sig, sig_neg = _sigmoid_pair(z)
    lf = jnp.log(lb + (1.0 - lb) * sig)
    k = (1.0 - lb) * sig_neg
    row = lax.broadcasted_iota(jnp.int32, (CHUNK, HGRN_W), 0)
    if first_valid_row:
        valid = row >= first_valid_row
        lf = jnp.where(valid, lf, 0.0)
        k = jnp.where(valid, k, 0.0)

    h1 = lf.astype(jnp.bfloat16)
    h2 = (lf - h1.astype(jnp.float32)).astype(jnp.bfloat16)
    e_all = _dot(amat, jnp.concatenate([h1, h2], axis=0))

    b = e_all[0:CHUNK]
    b_last = b[CHUNK - 1:CHUNK]
    q_in = (q * jnp.exp(b)).astype(jnp.bfloat16)
    k_out = (k * jnp.exp(b_last - b)).astype(jnp.bfloat16)
    st_decay = jnp.exp(b_last)
    v_bf = iv.astype(jnp.bfloat16)

    q_lv = [q.astype(jnp.bfloat16)]
    k_lv = [k.astype(jnp.bfloat16)]
    for i, m in enumerate(HALF_SPANS):
        ex = jnp.exp(e_all[(1 + i) * CHUNK:(2 + i) * CHUNK])
        right = (row & m) != 0
        q_lv.append(jnp.where(right, q * ex, 0.0).astype(jnp.bfloat16))
        k_lv.append(jnp.where(right, 0.0, k * ex).astype(jnp.bfloat16))
    lv_of = [len(HALF_SPANS)] + list(range(len(HALF_SPANS)))

    outs = []
    for h in range(HEADS):
        cols = slice(h * HEAD_DIM, (h + 1) * HEAD_DIM)
        scores = jnp.zeros((CHUNK, CHUNK), jnp.float32)
        for ql, kl, lv in zip(q_lv, k_lv, lv_of):
            scores = jnp.where(level == lv, _dot_nt(ql[:, cols], kl[:, cols]), scores)
        st = st_ref[h]
        o = _dot(scores.astype(jnp.bfloat16), v_bf[:, cols]) + _dot_nt(q_in[:, cols], st.astype(jnp.bfloat16))
        st_ref[h] = st * st_decay[:, cols] + _dot_tn(v_bf[:, cols], k_out[:, cols])
        outs.append(o)
    return jnp.concatenate(outs, axis=-1)


def _mixer_kernel(with_send, x_ref, meta_ref, gmix_ref, win_ref, lbt_ref, ghg_ref, cw_ref, gcv_ref, wout_ref,
                  amat_ref, level_ref, *rest):
    if with_send:
        off_ref, rows_ref, h1_ref, dst_hbm, proj_ref, o_ref, u_ref, st_ref, sem = rest
    else:
        h1_ref, proj_ref, o_ref, u_ref, st_ref = rest
    j = pl.program_id(1)
    rows = x_ref.shape[0]
    n_in = win_ref.shape[1]

    lbt = lbt_ref[...]
    lbe = jnp.exp(lbt - jnp.max(lbt, axis=0, keepdims=True))
    lb = lbe[0:1] / jnp.sum(lbe, axis=0, keepdims=True)

    amat = amat_ref[...]
    level = level_ref[...]
    gmix = gmix_ref[...]

    def project(xv, dst_rows):
        xn = _rms(xv, gmix).astype(jnp.bfloat16)
        for c0 in range(0, n_in, 512):
            proj_ref[dst_rows, c0:c0 + 512] = _dot(xn, win_ref[:, c0:c0 + 512])

    @pl.when(j == 0)
    def _():
        st_ref[...] = jnp.zeros_like(st_ref)
        project(meta_ref[...], pl.ds(0, CHUNK))
        pm = proj_ref[0:CHUNK, :]
        _hgrn_chunk(pm[:, 0:512], pm[:, 512:1024], pm[:, 1024:1536], lb, amat, level, st_ref,
                    CHUNK - N_META)
        u_ref[0:8, :] = (pm[:, 2560:3072] * pm[:, 3072:3584])[CHUNK - 8:CHUNK]

    if with_send:
        for tok in range(rows_ref.shape[0] // ROW_TILE):
            _start_rows_out(tok, rows_ref, off_ref, dst_hbm, sem)

    project(x_ref[...], pl.ds(0, rows))

    def chunk_body(c, carry):
        for u in range(CHUNK_UNROLL):
            r0 = pl.multiple_of((c * CHUNK_UNROLL + u) * CHUNK, CHUNK)
            q = proj_ref[pl.ds(r0, CHUNK), 0:512]
            z = proj_ref[pl.ds(r0, CHUNK), 512:1024]
            iv = proj_ref[pl.ds(r0, CHUNK), 1024:1536]
            o_ref[pl.ds(r0, CHUNK), :] = _hgrn_chunk(q, z, iv, lb, amat, level, st_ref, 0)
        return carry

    lax.fori_loop(0, rows // (CHUNK * CHUNK_UNROLL), chunk_body, 0)

    g_out = proj_ref[:, 1536:2048]
    g_sig, _ = _sigmoid_pair(g_out)
    y_hgrn = _group_rms(o_ref[...], ghg_ref[...], HEAD_DIM) * (g_out * g_sig)

    u = proj_ref[:, 2560:3072] * proj_ref[:, 3072:3584]
    u_ref[8:8 + rows, :] = u
    cw = cw_ref[...]
    y = cw[2:3] * u + cw[1:2] * u_ref[7:7 + rows, :] + cw[0:1] * u_ref[6:6 + rows, :]
    u_ref[0:8, :] = u[rows - 8:rows]
    y_conv = _group_rms(proj_ref[:, 2048:2560] * y, gcv_ref[...], CONV_W // CONV_GROUPS)

    mixed = jnp.concatenate([y_hgrn, y_conv], axis=-1).astype(jnp.bfloat16)
    h1_ref[...] = x_ref[...] + _dot(mixed, wout_ref[...])

    if with_send:
        _wait_rows_out(rows_ref, dst_hbm, sem)


def _mixer(x, batch0, n_batch, meta_pad, gmix, w_in, lb_table, ghg, conv_w, gcv, w_out, send=None):
    _, seq, d = x.shape
    n_in = w_in.shape[1]
    rows = MIX_ROWS
    n_j = seq // rows
    const = lambda *shape: pl.BlockSpec(shape, lambda b, j: (0,) * len(shape))
    in_specs = [
        pl.BlockSpec((None, rows, d), lambda b, j: (batch0 + b, j, 0)),
        const(CHUNK, d), const(1, d), const(d, n_in), const(*lb_table.shape), const(1, HGRN_W),
        const(CONV_K, CONV_W), const(1, CONV_W), const(d, d),
        const(N_LEVELS * CHUNK, 2 * CHUNK), const(CHUNK, CHUNK),
    ]
    args = [x, meta_pad, gmix, w_in, lb_table, ghg, conv_w, gcv, w_out,
            jnp.asarray(np.tile(_decay_sum_matrix(), (1, 2)), jnp.bfloat16), jnp.asarray(_level_matrix())]
    out_shape = jax.ShapeDtypeStruct((n_batch, seq, d), jnp.float32)
    out_specs = pl.BlockSpec((None, rows, d), lambda b, j: (b, j, 0))
    scratch = [
        pltpu.VMEM((rows, n_in), jnp.float32),
        pltpu.VMEM((rows, HGRN_W), jnp.float32),
        pltpu.VMEM((rows + 8, CONV_W), jnp.float32),
        pltpu.VMEM((HEADS, HEAD_DIM, HEAD_DIM), jnp.float32),
    ]
    if send is not None:
        off, xn, n_rows = send
        assert off.shape[0] == n_batch * n_j and off.shape[2] == rows * TOP_K
        in_specs += [_offsets_spec(rows, lambda b, j: (b * n_j + j, 0, 0)),
                     pl.BlockSpec((rows * ROW_TILE, LANES), lambda b, j: (b * n_j + j, 0))]
        args += [off, xn]
        out_shape = (out_shape, jax.ShapeDtypeStruct((n_rows * ROW_TILE, LANES), xn.dtype))
        out_specs = (out_specs, pl.BlockSpec(memory_space=pl.ANY))
        scratch += [pltpu.SemaphoreType.DMA(())]
    return pl.pallas_call(
        functools.partial(_mixer_kernel, send is not None),
        out_shape=out_shape,
        grid=(n_batch, n_j),
        in_specs=in_specs,
        out_specs=out_specs,
        scratch_shapes=scratch,
        compiler_params=pltpu.CompilerParams(
            dimension_semantics=("arbitrary", "arbitrary"), vmem_limit_bytes=VMEM_LIMIT),
        name="mixer_send" if send is not None else "mixer",
    )(*args)


def _router_kernel(h1_ref, gffn_ref, wr_ref, bias_ref, wgs_ref, wus_ref, wds_ref, tri_ref,
                   base_ref, xn_ref, idx_ref, gate_ref, rank_ref, counts_ref, carry_ref):
    i = pl.program_id(0)
    n_tok = h1_ref.shape[0]

    @pl.when(i == 0)
    def _():
        carry_ref[...] = jnp.zeros_like(carry_ref)

    h1 = h1_ref[...]
    xn = _rms(h1, gffn_ref[...])
    _store_row_tiles(xn_ref, _pack_rows(xn))
    xb = xn.astype(jnp.bfloat16)

    g_pre = _dot(xb, wgs_ref[...])
    gate_s, _ = _sigmoid_pair(g_pre)
    hid = (g_pre * gate_s) * _dot(xb, wus_ref[...])
    base_ref[...] = h1 + _dot(hid.astype(jnp.bfloat16), wds_ref[...])

    x_lo = (xn - xb.astype(jnp.float32)).astype(jnp.bfloat16)
    wr = wr_ref[...]
    w_hi = wr.astype(jnp.bfloat16)
    w_lo = (wr - w_hi.astype(jnp.float32)).astype(jnp.bfloat16)
    logits = _dot_nt(w_hi, xb) + _dot_nt(w_hi, x_lo) + _dot_nt(w_lo, xb)
    scores, _ = _sigmoid_pair(logits)
    sel = scores + bias_ref[...]
    eid = lax.broadcasted_iota(jnp.int32, (N_EXPERTS, n_tok), 0).astype(jnp.float32)
    picks, top_s = [], []
    for _ in range(TOP_K):
        best = jnp.max(sel, axis=0, keepdims=True)
        pick = jnp.min(jnp.where(sel == best, eid, float(N_EXPERTS)), axis=0, keepdims=True)
        hit = eid == pick
        top_s.append(jnp.sum(jnp.where(hit, scores, 0.0), axis=0, keepdims=True))
        sel = jnp.where(hit, -jnp.inf, sel)
        picks.append(pick)
    top_s = jnp.concatenate(top_s, axis=0)
    gate_ref[...] = top_s / jnp.sum(top_s, axis=0, keepdims=True) * ROUTED_SCALE
    idx_ref[...] = jnp.concatenate(picks, axis=0).astype(jnp.int32)

    chosen = jnp.zeros((N_EXPERTS, n_tok), jnp.float32)
    for pick in picks:
        chosen = chosen + jnp.where(eid == pick, 1.0, 0.0)
    incl = _dot(chosen.astype(jnp.bfloat16), tri_ref[...])
    before = carry_ref[...] + incl - 1.0
    ranks = [jnp.sum(jnp.where(eid == pick, before, 0.0), axis=0, keepdims=True) for pick in picks]
    rank_ref[...] = jnp.concatenate(ranks, axis=0).astype(jnp.int32)
    carry_ref[...] = carry_ref[...] + incl[:, n_tok - 1:n_tok]
    counts_ref[...] = jnp.broadcast_to(carry_ref[...], counts_ref.shape).astype(jnp.int32)


def _router(h1, gffn, w_router, bias, wgs, wus, wds):
    n, d = h1.shape
    t = ROUTE_ROWS
    ff = wgs.shape[1]
    tri = jnp.asarray(np.triu(np.ones((t, t), np.float32)), jnp.bfloat16)
    const = lambda *shape: pl.BlockSpec(shape, lambda i: (0,) * len(shape))
    tok = lambda width: pl.BlockSpec((t, width), lambda i: (i, 0))
    slot = pl.BlockSpec((TOP_K, t), lambda i: (0, i))
    return pl.pallas_call(
        _router_kernel,
        out_shape=(
            jax.ShapeDtypeStruct((n, d), jnp.float32),
            jax.ShapeDtypeStruct((n * ROW_TILE, LANES), jnp.uint32),
            jax.ShapeDtypeStruct((TOP_K, n), jnp.int32),
            jax.ShapeDtypeStruct((TOP_K, n), jnp.float32),
            jax.ShapeDtypeStruct((TOP_K, n), jnp.int32),
            jax.ShapeDtypeStruct((N_EXPERTS, 128), jnp.int32),
        ),
        grid=(n // t,),
        in_specs=[tok(d), const(1, d), const(N_EXPERTS, d), const(N_EXPERTS, 1),
                  const(d, ff), const(d, ff), const(ff, d), const(t, t)],
        out_specs=(tok(d), pl.BlockSpec((t * ROW_TILE, LANES), lambda i: (i, 0)), slot, slot, slot,
                   const(N_EXPERTS, 128)),
        scratch_shapes=[pltpu.VMEM((N_EXPERTS, 1), jnp.float32)],
        compiler_params=pltpu.CompilerParams(
            dimension_semantics=("arbitrary",), vmem_limit_bytes=VMEM_LIMIT),
        name="router",
    )(h1, gffn, w_router, bias, wgs, wus, wds, tri)


def _slab(ref, first_sublane):
    if not isinstance(first_sublane, int):
        first_sublane = pl.multiple_of(first_sublane, ROW_TILE)
    return ref.at[pl.ds(first_sublane, ROW_TILE)]


def _start_rows_out(tok, rows_ref, off_ref, dst_hbm, sem):
    for s in range(TOP_K):
        pltpu.make_async_copy(_slab(rows_ref, tok * ROW_TILE), _slab(dst_hbm, off_ref[0, tok * TOP_K + s]),
                              sem).start(priority=s % DMA_QUEUES)


def _wait_rows_out(rows_ref, dst_hbm, sem):
    for s in range(TOP_K):
        pltpu.make_async_copy(rows_ref, dst_hbm.at[pl.ds(0, rows_ref.shape[0])], sem).wait()


def _start_rows_in(tok, n_tok, src_hbm, off_ref, buf_ref, sem):
    for s in range(TOP_K):
        pltpu.make_async_copy(_slab(src_hbm, off_ref[0, tok * TOP_K + s]),
                              _slab(buf_ref, (s * n_tok + tok) * ROW_TILE), sem).start(priority=s % DMA_QUEUES)


def _wait_rows_in(src_hbm, buf_ref, sem):
    pltpu.make_async_copy(src_hbm.at[pl.ds(0, buf_ref.shape[0])], buf_ref, sem).wait()


def _weighted_sum(base_ref, gates_ref, gfin_ref, buf_ref, out_ref):
    n_tok = base_ref.shape[0]
    gates = gates_ref[...]
    width = ROW_TILE * LANES
    lo_chunks, hi_chunks = [], []
    for c in range(ROW_TILE):
        acc_lo = base_ref[:, c * LANES:(c + 1) * LANES]
        acc_hi = base_ref[:, width + c * LANES:width + (c + 1) * LANES]
        for s in range(TOP_K):
            lo, hi = _unpack_words(buf_ref[pl.ds(s * n_tok * ROW_TILE + c, n_tok, stride=ROW_TILE), :])
            acc_lo = acc_lo + gates[:, s:s + 1] * lo
            acc_hi = acc_hi + gates[:, s:s + 1] * hi
        lo_chunks.append(acc_lo)
        hi_chunks.append(acc_hi)
    out_ref[...] = _rms(jnp.concatenate(lo_chunks + hi_chunks, axis=-1), gfin_ref[...])


def _offsets_spec(n_tok, index_map):
    return pl.BlockSpec((None, 1, n_tok * TOP_K), index_map, memory_space=pltpu.SMEM)


def _swiglu_block(n_valid, xs_ref, wgb_ref, wub_ref, wdb_ref, y_ref):
    r = EXPERT_ROWS
    keep = lax.broadcasted_iota(jnp.int32, (r, LANES), 0) < n_valid
    lo, hi = _load_row_tiles(xs_ref, r)
    xb = jnp.concatenate([jnp.where(keep, c, 0.0).astype(jnp.bfloat16) for c in lo + hi], axis=-1)
    g = _dot(xb, wgb_ref[...])
    u = _dot(xb, wub_ref[...])
    g_sig, _ = _sigmoid_pair(g)
    hid = ((g * g_sig) * u).astype(jnp.bfloat16)
    _store_row_tiles(y_ref, _pack_rows(_dot(hid, wdb_ref[...])))


def _refresh_weights(i, blk_e_ref, wg_ref, wu_ref, wd_ref, wgb_ref, wub_ref, wdb_ref):
    @pl.when((i == 0) | (blk_e_ref[i] != blk_e_ref[jnp.maximum(i - 1, 0)]))
    def _():
        wgb_ref[...] = wg_ref[...].astype(jnp.bfloat16)
        wub_ref[...] = wu_ref[...].astype(jnp.bfloat16)
        wdb_ref[...] = wd_ref[...].astype(jnp.bfloat16)


def _experts_out_kernel(side_steps, blk_e_ref, blk_rows_ref, blk_i_ref, xs_ref, wg_ref, wu_ref, wd_ref, off_ref,
                        rows_ref, y_ref, dst_hbm, wgb_ref, wub_ref, wdb_ref, sem):
    del blk_i_ref
    i = pl.program_id(0)
    n_valid = blk_rows_ref[i]
    side = i < side_steps
    _refresh_weights(i, blk_e_ref, wg_ref, wu_ref, wd_ref, wgb_ref, wub_ref, wdb_ref)

    def send():
        for tok in range(SIDE_ROWS):
            _start_rows_out(tok, rows_ref, off_ref, dst_hbm, sem)

    @pl.when((n_valid > 0) & side)
    def _():
        send()
        _swiglu_block(n_valid, xs_ref, wgb_ref, wub_ref, wdb_ref, y_ref)

    @pl.when((n_valid > 0) & jnp.logical_not(side))
    def _():
        _swiglu_block(n_valid, xs_ref, wgb_ref, wub_ref, wdb_ref, y_ref)

    @pl.when((n_valid == 0) & side)
    def _():
        send()

    @pl.when(side)
    def _():
        _wait_rows_out(rows_ref, dst_hbm, sem)


def _experts_in_kernel(side_steps, blk_e_ref, blk_rows_ref, blk_i_ref, xs_ref, wg_ref, wu_ref, wd_ref,
                       off_ref, off_nx_ref, off_nx2_ref, base_ref, gates_ref, gfin_ref, src_hbm, y_ref, out_ref,
                       wgb_ref, wub_ref, wdb_ref, buf_ref, sem):
    del blk_i_ref
    i = pl.program_id(0)
    n_valid = blk_rows_ref[i]
    slot = i % 2
    _refresh_weights(i, blk_e_ref, wg_ref, wu_ref, wd_ref, wgb_ref, wub_ref, wdb_ref)

    def fetch(off_r, into):
        for tok in range(SIDE_ROWS):
            _start_rows_in(tok, SIDE_ROWS, src_hbm, off_r, buf_ref.at[into], sem.at[into])

    @pl.when(i == 0)
    def _():
        fetch(off_ref, 0)
        fetch(off_nx_ref, 1)

    @pl.when(i < side_steps)
    def _():
        _wait_rows_in(src_hbm, buf_ref.at[slot], sem.at[slot])
        _weighted_sum(base_ref, gates_ref, gfin_ref, buf_ref.at[slot], out_ref)

    more = i + 2 < side_steps

    @pl.when((n_valid > 0) & more)
    def _():
        fetch(off_nx2_ref, slot)
        _swiglu_block(n_valid, xs_ref, wgb_ref, wub_ref, wdb_ref, y_ref)

    @pl.when((n_valid > 0) & jnp.logical_not(more))
    def _():
        _swiglu_block(n_valid, xs_ref, wgb_ref, wub_ref, wdb_ref, y_ref)

    @pl.when((n_valid == 0) & more)
    def _():
        fetch(off_nx2_ref, slot)


def _expert_specs(d, ff):
    r = EXPERT_ROWS
    rows = pl.BlockSpec((r * ROW_TILE, LANES), lambda i, be, br, bi: (bi[i], 0))
    weights = [pl.BlockSpec((None, d, ff), lambda i, be, br, bi: (be[i], 0, 0)),
               pl.BlockSpec((None, d, ff), lambda i, be, br, bi: (be[i], 0, 0)),
               pl.BlockSpec((None, ff, d), lambda i, be, br, bi: (be[i], 0, 0))]
    scratch = [pltpu.VMEM((d, ff), jnp.bfloat16), pltpu.VMEM((d, ff), jnp.bfloat16),
               pltpu.VMEM((ff, d), jnp.bfloat16)]
    return rows, weights, scratch


def _experts_out(blk, xs, wg, wu, wd, off, xn, first_block, n_rows_next):
    side_steps = off.shape[0]
    d, ff = wg.shape[1], wg.shape[2]
    rows, weights, scratch = _expert_specs(d, ff)
    side = lambda i, be, br, bi: jnp.minimum(i, side_steps - 1)
    return pl.pallas_call(
        functools.partial(_experts_out_kernel, side_steps),
        out_shape=(jax.ShapeDtypeStruct(xs.shape, xs.dtype),
                   jax.ShapeDtypeStruct((n_rows_next * ROW_TILE, LANES), xs.dtype)),
        grid_spec=pltpu.PrefetchScalarGridSpec(
            num_scalar_prefetch=3,
            grid=(xs.shape[0] // (EXPERT_ROWS * ROW_TILE),),
            in_specs=[rows] + weights + [
                _offsets_spec(SIDE_ROWS, lambda i, be, br, bi: (side(i, be, br, bi), 0, 0)),
                pl.BlockSpec((SIDE_ROWS * ROW_TILE, LANES), lambda i, be, br, bi: (first_block + side(i, be, br, bi), 0))],
            out_specs=(rows, pl.BlockSpec(memory_space=pl.ANY)),
            scratch_shapes=scratch + [pltpu.SemaphoreType.DMA(())],
        ),
        compiler_params=pltpu.CompilerParams(
            dimension_semantics=("arbitrary",), vmem_limit_bytes=VMEM_LIMIT),
        name="experts_send",
    )(*blk, xs, wg, wu, wd, off, xn)


def _experts_in(blk, xs, wg, wu, wd, off, base, gates, gfin, y_prev, n):
    side_steps = off.shape[0]
    d, ff = wg.shape[1], wg.shape[2]
    rows, weights, scratch = _expert_specs(d, ff)
    side = lambda i, be, br, bi: jnp.minimum(i, side_steps - 1)
    nxt = lambda i, be, br, bi: jnp.minimum(i + 1, side_steps - 1)
    nxt2 = lambda i, be, br, bi: jnp.minimum(i + 2, side_steps - 1)
    return pl.pallas_call(
        functools.partial(_experts_in_kernel, side_steps),
        out_shape=(jax.ShapeDtypeStruct(xs.shape, xs.dtype), jax.ShapeDtypeStruct((n, d), jnp.float32)),
        grid_spec=pltpu.PrefetchScalarGridSpec(
            num_scalar_prefetch=3,
            grid=(xs.shape[0] // (EXPERT_ROWS * ROW_TILE),),
            in_specs=[rows] + weights + [
                _offsets_spec(SIDE_ROWS, lambda i, be, br, bi: (side(i, be, br, bi), 0, 0)),
                _offsets_spec(SIDE_ROWS, lambda i, be, br, bi: (nxt(i, be, br, bi), 0, 0)),
                _offsets_spec(SIDE_ROWS, lambda i, be, br, bi: (nxt2(i, be, br, bi), 0, 0)),
                pl.BlockSpec((SIDE_ROWS, d), lambda i, be, br, bi: (side(i, be, br, bi), 0)),
                pl.BlockSpec((SIDE_ROWS, TOP_K), lambda i, be, br, bi: (side(i, be, br, bi), 0)),
                pl.BlockSpec((1, d), lambda i, be, br, bi: (0, 0)),
                pl.BlockSpec(memory_space=pl.ANY)],
            out_specs=(rows, pl.BlockSpec((SIDE_ROWS, d), lambda i, be, br, bi: (side(i, be, br, bi), 0))),
            scratch_shapes=scratch + [pltpu.VMEM((2, TOP_K * SIDE_ROWS * ROW_TILE, LANES), xs.dtype),
                                      pltpu.SemaphoreType.DMA((2,))],
        ),
        compiler_params=pltpu.CompilerParams(
            dimension_semantics=("arbitrary",), vmem_limit_bytes=VMEM_LIMIT),
        name="experts_fetch",
    )(*blk, xs, wg, wu, wd, off, off, off, base, gates, gfin, y_prev)


def _combine_kernel(off_ref, off_nx_ref, base_ref, gates_ref, gfin_ref, y_hbm, out_in_ref, out_ref,
                    buf0_ref, buf1_ref, sem):
    del out_in_ref
    i = pl.program_id(0)
    n_steps = pl.num_programs(0)
    n_tok = base_ref.shape[0]
    bufs = (buf0_ref, buf1_ref)

    @pl.when(i == 0)
    def _():
        def body(g, carry):
            for u in range(ISSUE_UNROLL):
                _start_rows_in(g * ISSUE_UNROLL + u, n_tok, y_hbm, off_ref, buf0_ref, sem.at[0])
            return carry
        lax.fori_loop(0, n_tok // ISSUE_UNROLL, body, 0)

    for p in range(2):
        def step(fetch_next, p=p):
            _wait_rows_in(y_hbm, bufs[p], sem.at[p])
            if fetch_next:
                for tok in range(n_tok):
                    _start_rows_in(tok, n_tok, y_hbm, off_nx_ref, bufs[1 - p], sem.at[1 - p])
            _weighted_sum(base_ref, gates_ref, gfin_ref, bufs[p], out_ref)

        pl.when((i % 2 == p) & (i + 1 < n_steps))(functools.partial(step, True))
        pl.when((i % 2 == p) & (i + 1 >= n_steps))(functools.partial(step, False))


def _combine(off, base, gates, gfin, y, out_prev, first_block):
    d = base.shape[1]
    t = COMBINE_ROWS
    n_steps = off.shape[0]
    tok = lambda width, first=0: pl.BlockSpec((t, width), lambda i: (first + i, 0))
    return pl.pallas_call(
        _combine_kernel,
        out_shape=jax.ShapeDtypeStruct(out_prev.shape, jnp.float32),
        grid=(n_steps,),
        in_specs=[_offsets_spec(t, lambda i: (i, 0, 0)),
                  _offsets_spec(t, lambda i: (jnp.minimum(i + 1, n_steps - 1), 0, 0)),
                  tok(d), tok(TOP_K), pl.BlockSpec((1, d), lambda i: (0, 0)),
                  pl.BlockSpec(memory_space=pl.ANY), pl.BlockSpec(memory_space=pl.ANY)],
        out_specs=tok(d, first_block),
        scratch_shapes=[pltpu.VMEM((TOP_K * t * ROW_TILE, LANES), y.dtype),
                        pltpu.VMEM((TOP_K * t * ROW_TILE, LANES), y.dtype),
                        pltpu.SemaphoreType.DMA((2,))],
        input_output_aliases={6: 0},
        compiler_params=pltpu.CompilerParams(
            dimension_semantics=("arbitrary",), vmem_limit_bytes=VMEM_LIMIT),
        name="combine",
    )(off, off, base, gates, gfin, y, out_prev)


def _group_layout(counts, idx, rank):
    r = EXPERT_ROWS
    n_g = idx.shape[1]
    n_blocks = (n_g * TOP_K) // r + N_EXPERTS
    padded = (counts + r - 1) // r * r
    ends = jnp.cumsum(padded)
    starts = (ends - padded).astype(jnp.int32)
    blk_row0 = jnp.arange(n_blocks, dtype=jnp.int32) * r
    blk_e = jnp.minimum(jnp.sum((ends[None, :] <= blk_row0[:, None]).astype(jnp.int32), axis=1), N_EXPERTS - 1)
    onehot = (blk_e[:, None] == jnp.arange(N_EXPERTS, dtype=jnp.int32)[None, :]).astype(jnp.int32)
    blk_rows = jnp.clip(onehot @ counts - (blk_row0 - onehot @ starts), 0, r).astype(jnp.int32)
    dest = rank + jnp.sum(jnp.where(idx[None] == jnp.arange(N_EXPERTS, dtype=jnp.int32)[:, None, None],
                                    starts[:, None, None], 0), axis=0)
    blk_i = jnp.minimum(jnp.arange(n_blocks, dtype=jnp.int32), ends[-1].astype(jnp.int32) // r - 1)
    return (blk_e, blk_rows, blk_i), (dest * ROW_TILE).T.reshape(-1), n_blocks * r


def kernel(x, meta_tokens, norm_mix_g, w_in, lb_table, hgrn_norm_g, conv_w, conv_norm_g, w_out,
           norm_ffn_g, w_router, router_bias, w_gate_e, w_up_e, w_down_e, w_gate_s, w_up_s, w_down_s,
           norm_final_g):
    bsz, seq, d = x.shape
    n = bsz * seq
    assert TOKEN_GROUPS == 2 and bsz % TOKEN_GROUPS == 0
    b_g = bsz // TOKEN_GROUPS
    n_g = n // TOKEN_GROUPS
    bf = jnp.bfloat16
    assert seq % MIX_ROWS == 0 and MIX_ROWS % CHUNK == 0
    assert n_g % ROUTE_ROWS == 0 and n_g % COMBINE_ROWS == 0
    assert (n_g * TOP_K) % EXPERT_ROWS == 0 and n_g % SIDE_ROWS == 0
    assert 2 <= n_g // SIDE_ROWS < (n_g * TOP_K) // EXPERT_ROWS + N_EXPERTS

    meta_pad = jnp.zeros((CHUNK, d), jnp.float32).at[CHUNK - N_META:].set(meta_tokens)
    mix = functools.partial(_mixer, meta_pad=meta_pad, gmix=norm_mix_g[0:1], w_in=w_in[0].astype(bf),
                            lb_table=lb_table, ghg=hgrn_norm_g[0:1], conv_w=conv_w[0], gcv=conv_norm_g[0:1],
                            w_out=w_out[0].astype(bf))
    route = functools.partial(_router, gffn=norm_ffn_g[0:1], w_router=w_router[0].T, bias=router_bias[0][:, None],
                              wgs=w_gate_s[0].astype(bf), wus=w_up_s[0].astype(bf), wds=w_down_s[0].astype(bf))
    steps = lambda off, t: off.reshape(n_g // t, 1, t * TOP_K)
    gfin = norm_final_g[None, :]
    wg, wu, wd = w_gate_e[0], w_up_e[0], w_down_e[0]

    h1_0 = mix(x, 0, b_g)
    base0, xn0, idx0, gate0, rank0, counts0 = route(h1_0.reshape(n_g, d))
    blk0, off0, rows0 = _group_layout(counts0[:, 0], idx0, rank0)
    h1_1, xs0 = mix(x, b_g, b_g, send=(steps(off0, MIX_ROWS), xn0, rows0))
    base1, xn1, idx1, gate1, rank1, counts1 = route(h1_1.reshape(n_g, d))
    blk1, off1, rows1 = _group_layout(counts1[:, 0], idx1, rank1)
    y0, xs1 = _experts_out(blk0, xs0, wg, wu, wd, steps(off1, SIDE_ROWS), xn1, 0, rows1)
    y1, out = _experts_in(blk1, xs1, wg, wu, wd, steps(off0, SIDE_ROWS), base0, gate0.T, gfin, y0, n)
    out = _combine(steps(off1, COMBINE_ROWS), base1, gate1.T, gfin, y1, out, n_g // COMBINE_ROWS)
    return out.reshape(bsz, seq, d)
```
